```python
import math
import jax, jax.numpy as jnp
from jax import lax
import numpy as np

D_MODEL = 1024
BATCH = 1
SEQ = 16384
DEPTH = 4
DEC_BATCH = 8
DEC_SEQ = 64
PAST_LEN = 1024

CHUNK = 64
N_EVEN = (DEPTH + 1) // 2
N_ODD = DEPTH // 2
Q_BLOCK = 128
EPS = 1e-6

A_HEADS = 8
A_DIM = 64
A_LEFT_CHUNKS = 8
A_WINDOW = A_LEFT_CHUNKS * CHUNK
A_MAX_REL = 128
B_HEADS = 8
B_NOPE = 64
B_ROPE = 32
B_V = 64
Q_LORA = 256
KV_LORA = 128
ROPE_THETA = 10000.0
C_HEADS = 4
C_DIM = 64
D_HEADS = 8
D_DIM = 64
IDX_HEADS = 8
IDX_DIM = 64
TOPK_MAX = 256
T5_BUCKETS = 32
T5_MAX_DIST = 128
T5_HEADS = C_HEADS + D_HEADS
FF_HIDDEN = -(-(8 * D_MODEL // 3) // 256) * 256
MIX_WIDTH = A_HEADS * A_DIM + B_HEADS * B_V
EVEN_SPLITS = (A_HEADS * A_DIM, A_HEADS * A_DIM, A_HEADS * A_DIM, Q_LORA, KV_LORA, B_ROPE)
ODD_SPLITS = (C_HEADS * 2 * C_DIM, C_HEADS * 2 * C_DIM, C_HEADS * 2 * C_DIM,
              D_HEADS * D_DIM, D_DIM, D_DIM, IDX_HEADS * IDX_DIM, IDX_DIM, IDX_HEADS)

kernel_name = 'chunk_streaming_hybrid_encoder_step'


def rmsnorm(x, g):
    xf = x.astype(jnp.float32)
    y = xf * lax.rsqrt(jnp.mean(xf * xf, axis=-1, keepdims=True) + EPS)
    return (y * g.astype(jnp.float32)).astype(x.dtype)


def split_cols(h, sizes):
    return jnp.split(h, np.cumsum(sizes)[:-1].tolist(), axis=-1)


def last_rows(a, n):
    t = a.shape[1]
    if t >= n:
        return a[:, t - n:]
    return jnp.pad(a, ((0, 0), (n - t, 0)) + ((0, 0),) * (a.ndim - 2))


def rope(x, pos):
    half = x.shape[-1] // 2
    freqs = jnp.power(jnp.float32(ROPE_THETA), -jnp.arange(half, dtype=jnp.float32) / half)
    ang = pos.astype(jnp.float32)[:, None] * freqs[None, :]
    cos = jnp.cos(ang)[None, :, None, :].astype(x.dtype)
    sin = jnp.sin(ang)[None, :, None, :].astype(x.dtype)
    x1, x2 = x[..., :half], x[..., half:]
    return jnp.concatenate([x1 * cos - x2 * sin, x1 * sin + x2 * cos], axis=-1)


def chunk_causal(q_pos, k_pos):
    return (k_pos[None, :] // CHUNK) <= (q_pos[:, None] // CHUNK)


def t5_bucket(rel):
    nb = T5_BUCKETS // 2
    max_exact = nb // 2
    ret = jnp.where(rel > 0, nb, 0)
    n = jnp.abs(rel)
    nf = jnp.maximum(n, 1).astype(jnp.float32)
    large = max_exact + (jnp.log(nf / max_exact) / math.log(T5_MAX_DIST / max_exact)
                         * (nb - max_exact)).astype(jnp.int32)
    large = jnp.minimum(large, nb - 1)
    return ret + jnp.where(n < max_exact, n, large)


def masked_softmax(logits, mask):
    return jax.nn.softmax(jnp.where(mask, logits.astype(jnp.float32), -jnp.inf), axis=-1)


def sweep(core, q_args, q_pos):
    t = q_pos.shape[0]
    if t <= Q_BLOCK:
        return core(*q_args, q_pos)
    nb = t // Q_BLOCK
    blocks = tuple(jnp.moveaxis(a.reshape(a.shape[0], nb, Q_BLOCK, *a.shape[2:]), 1, 0) for a in q_args)
    out = lax.map(lambda xs: core(*xs[0], xs[1]), (blocks, q_pos.reshape(nb, Q_BLOCK)))
    out = jnp.moveaxis(out, 0, 1)
    return out.reshape(out.shape[0], t, *out.shape[3:])


def band_core(q, k, v, rel, mask, relbias):
    logits = jnp.einsum('bgqhd,bgkhd->bghqk', q, k).astype(jnp.float32) * (A_DIM ** -0.5)
    bias = relbias[:, jnp.clip(rel, -A_MAX_REL, A_MAX_REL) + A_MAX_REL].astype(jnp.float32)
    p = masked_softmax(logits + bias, mask[None, :, None])
    return jnp.einsum('bghqk,bgkhd->bgqhd', p.astype(v.dtype), v)


def band_attn_prompt(q, k, v, relbias):
    b, t, h, dh = q.shape
    nc = t // CHUNK
    w = A_LEFT_CHUNKS + 1
    idx = jnp.arange(nc)[:, None] + jnp.arange(w)[None, :]

    def band(a):
        ap = jnp.pad(a, ((0, 0), (A_WINDOW, 0), (0, 0), (0, 0))).reshape(b, nc + A_LEFT_CHUNKS, CHUNK, h, dh)
        return ap[:, idx].reshape(b, nc, w * CHUNK, h, dh)

    i = jnp.arange(CHUNK)
    j = jnp.arange(w * CHUNK)
    rel = i[:, None] + A_WINDOW - j[None, :]
    k_pos = (jnp.arange(nc)[:, None] - A_LEFT_CHUNKS) * CHUNK + j[None, :]
    mask = (k_pos >= 0)[:, None, :]
    out = band_core(q.reshape(b, nc, CHUNK, h, dh), band(k), band(v), rel, mask, relbias)
    return out.reshape(b, t, h, dh)


def band_attn_sample(q, k_all, v_all, q_pos, k_pos, relbias):
    rel = q_pos[:, None] - k_pos[None, :]
    qc = q_pos[:, None] // CHUNK
    kc = k_pos[None, :] // CHUNK
    mask = ((kc <= qc) & (qc - kc <= A_LEFT_CHUNKS) & (k_pos[None, :] >= 0))[None]
    return band_core(q[:, None], k_all[:, None], v_all[:, None], rel, mask, relbias)[:, 0]


def mla_core(q_nope, q_rope, q_pos, k_nope, k_rope, v, k_pos):
    logits = (jnp.einsum('bqhn,bkhn->bhqk', q_nope, k_nope).astype(jnp.float32)
              + jnp.einsum('bqhr,bkr->bhqk', q_rope, k_rope).astype(jnp.float32)) * ((B_NOPE + B_ROPE) ** -0.5)
    p = masked_softmax(logits, chunk_causal(q_pos, k_pos))
    return jnp.einsum('bhqk,bkhv->bqhv', p.astype(v.dtype), v)


def diff_core(q, q_pos, k, v, k_pos, t5_tab, lam):
    qs = q.reshape(*q.shape[:3], 2, C_DIM)
    ks = k.reshape(*k.shape[:3], 2, C_DIM)
    logits = jnp.einsum('bqhmd,bkhmd->bhmqk', qs, ks).astype(jnp.float32) * (C_DIM ** -0.5)
    bias = t5_tab[t5_bucket(k_pos[None, :] - q_pos[:, None])]
    logits = logits + jnp.transpose(bias, (2, 0, 1))[None, :, None].astype(jnp.float32)
    p = masked_softmax(logits, chunk_causal(q_pos, k_pos))
    attn = p[:, :, 0] - lam * p[:, :, 1]
    return jnp.einsum('bhqk,bkhd->bqhd', attn.astype(v.dtype), v)


def dsa_core(q, qi, wt, q_pos, k, v, ki, k_pos, t5_tab, n_sel):
    idx_logits = jnp.einsum('bqhd,bkd->bqhk', qi, ki).astype(jnp.float32) * (IDX_DIM ** -0.5)
    score = jnp.einsum('bqh,bqhk->bqk', wt.astype(jnp.float32) * (IDX_HEADS ** -0.5), jax.nn.relu(idx_logits))
    score = jnp.where(chunk_causal(q_pos, k_pos)[None], score, -jnp.inf)
    top_val, top_idx = lax.top_k(score, n_sel)
    gather = jax.vmap(lambda a, i: a[i])
    k_sel = gather(k, top_idx)
    v_sel = gather(v, top_idx)
    pos_sel = k_pos[top_idx]
    logits = jnp.einsum('bqhd,bqnd->bhqn', q, k_sel).astype(jnp.float32) * (D_DIM ** -0.5)
    bias = t5_tab[t5_bucket(pos_sel - q_pos[None, :, None])]
    logits = logits + jnp.transpose(bias, (0, 3, 1, 2)).astype(jnp.float32)
    p = masked_softmax(logits, jnp.isfinite(top_val)[:, None])
    return jnp.einsum('bhqn,bqnd->bqhd', p.astype(v.dtype), v_sel)


def even_mixer(h, q_pos, cache, n_a, w_in, w_out, relbias, q_norm, kv_norm, w_uq, w_ukv):
    b, t, _ = h.shape
    aq, ak, av, cq, ckv, kr = split_cols(h @ w_in, EVEN_SPLITS)
    aq, ak, av = (a.reshape(b, t, A_HEADS, A_DIM) for a in (aq, ak, av))
    qb = (rmsnorm(cq, q_norm) @ w_uq).reshape(b, t, B_HEADS, B_NOPE + B_ROPE)
    q_nope = qb[..., :B_NOPE]
    q_rope = rope(qb[..., B_NOPE:], q_pos)
    ckv = rmsnorm(ckv, kv_norm)
    kr = rope(kr[:, :, None, :], q_pos)[:, :, 0]
    if cache is None:
        a_out = band_attn_prompt(aq, ak, av, relbias)
        ckv_all, kr_all, k_pos = ckv, kr, q_pos
        new_ak, new_av = last_rows(ak, n_a), last_rows(av, n_a)
    else:
        c_ak, c_av, c_ckv, c_kr = cache
        past = c_ckv.shape[1]
        ak_all = jnp.concatenate([c_ak, ak], axis=1)
        av_all = jnp.concatenate([c_av, av], axis=1)
        a_kpos = jnp.arange(past - n_a, past + t)
        a_out = band_attn_sample(aq, ak_all, av_all, q_pos, a_kpos, relbias)
        ckv_all = jnp.concatenate([c_ckv, ckv], axis=1)
        kr_all = jnp.concatenate([c_kr, kr], axis=1)
        k_pos = jnp.arange(past + t)
        new_ak, new_av = ak_all[:, -n_a:], av_all[:, -n_a:]
    kv = (ckv_all @ w_ukv).reshape(b, -1, B_HEADS, B_NOPE + B_V)
    k_nope, v = kv[..., :B_NOPE], kv[..., B_NOPE:]
    b_out = sweep(lambda qn, qr, pos: mla_core(qn, qr, pos, k_nope, kr_all, v, k_pos), (q_nope, q_rope), q_pos)
    mix = jnp.concatenate([a_out.reshape(b, t, -1), b_out.reshape(b, t, -1)], axis=-1)
    return mix @ w_out, (new_ak, new_av, ckv, kr)


def odd_mixer(h, q_pos, cache, lam_init, w_in, w_out, lq1, lk1, lq2, lk2, subln, t5_table):
    b, t, _ = h.shape
    cq, ck, cv, dq, dk, dv, dqi, dki, dw = split_cols(h @ w_in, ODD_SPLITS)
    cq, ck, cv = (a.reshape(b, t, C_HEADS, 2 * C_DIM) for a in (cq, ck, cv))
    dq = dq.reshape(b, t, D_HEADS, D_DIM)
    dqi = dqi.reshape(b, t, IDX_HEADS, IDX_DIM)
    if cache is None:
        ck_all, cv_all, dk_all, dv_all, dki_all, k_pos = ck, cv, dk, dv, dki, q_pos
    else:
        c_ck, c_cv, c_dk, c_dv, c_dki = cache
        past = c_ck.shape[1]
        ck_all = jnp.concatenate([c_ck, ck], axis=1)
        cv_all = jnp.concatenate([c_cv, cv], axis=1)
        dk_all = jnp.concatenate([c_dk, dk], axis=1)
        dv_all = jnp.concatenate([c_dv, dv], axis=1)
        dki_all = jnp.concatenate([c_dki, dki], axis=1)
        k_pos = jnp.arange(past + t)
    f32 = jnp.float32
    lam = (jnp.exp(jnp.sum(lq1.astype(f32) * lk1.astype(f32)))
           - jnp.exp(jnp.sum(lq2.astype(f32) * lk2.astype(f32))) + lam_init)
    t5_c, t5_d = t5_table[:, :C_HEADS], t5_table[:, C_HEADS:]
    c_out = sweep(lambda q, pos: diff_core(q, pos, ck_all, cv_all, k_pos, t5_c, lam), (cq,), q_pos)
    c_out = rmsnorm(c_out, subln) * (1.0 - lam_init)
    n_sel = min(TOPK_MAX, k_pos.shape[0] // 4)
    d_out = sweep(lambda q, qi, wt, pos: dsa_core(q, qi, wt, pos, dk_all, dv_all, dki_all, k_pos, t5_d, n_sel),
                  (dq, dqi, dw), q_pos)
    mix = jnp.concatenate([c_out.reshape(b, t, -1), d_out.reshape(b, t, -1)], axis=-1)
    return mix @ w_out, (ck, cv, dk, dv, dki)


def swiglu(x, wg, wu, wd):
    return (jax.nn.silu(x @ wg) * (x @ wu)) @ wd


def lambda_init(layer):
    return 0.8 - 0.6 * math.exp(-0.3 * layer)


def run_trunk(x, q_pos, caches, n_a, w):
    ev_states, od_states = [], []
    for l in range(DEPTH):
        h = rmsnorm(x, w['norm_attn'][l])
        i = l // 2
        if l % 2 == 0:
            cache = None if caches is None else tuple(c[i] for c in caches[0])
            out, st = even_mixer(h, q_pos, cache, n_a, w['even_w_in'][i], w['even_w_out'][i], w['a_relbias'][i],
                                 w['b_q_norm'][i], w['b_kv_norm'][i], w['b_w_uq'][i], w['b_w_ukv'][i])
            ev_states.append(st)
        else:
            cache = None if caches is None else tuple(c[i] for c in caches[1])
            out, st = odd_mixer(h, q_pos, cache, lambda_init(l), w['odd_w_in'][i], w['odd_w_out'][i],
                                w['c_lambda_q1'][i], w['c_lambda_k1'][i], w['c_lambda_q2'][i], w['c_lambda_k2'][i],
                                w['c_subln'][i], w['t5_table'])
            od_states.append(st)
        x = x + out
        x = x + swiglu(rmsnorm(x, w['norm_ffn'][l]), w['ffn_w_gate'][l], w['ffn_w_up'][l], w['ffn_w_down'][l])
    return rmsnorm(x, w['final_norm']), ev_states, od_states


def stack_states(states, j):
    return jnp.stack([s[j] for s in states], axis=0)


def setup_inputs(seed: int = 0) -> dict:
    key = jax.random.key(seed)
    ks = iter(jax.random.split(key, 48))

    def nrm(shape, scale=1.0):
        return jax.random.normal(next(ks), shape, jnp.float32) * scale

    def gain(shape):
        return 1.0 + nrm(shape, 0.05)

    a_cache = min(A_WINDOW, PAST_LEN)
    e_in = sum(EVEN_SPLITS)
    o_in = sum(ODD_SPLITS)
    return {
        'x_prompt': nrm((BATCH, SEQ, D_MODEL)),
        'x_sample': nrm((DEC_BATCH, DEC_SEQ, D_MODEL)),
        'cache_a_k': nrm((N_EVEN, DEC_BATCH, a_cache, A_HEADS, A_DIM)),
        'cache_a_v': nrm((N_EVEN, DEC_BATCH, a_cache, A_HEADS, A_DIM)),
        'cache_b_ckv': nrm((N_EVEN, DEC_BATCH, PAST_LEN, KV_LORA)),
        'cache_b_krope': nrm((N_EVEN, DEC_BATCH, PAST_LEN, B_ROPE)),
        'cache_c_k': nrm((N_ODD, DEC_BATCH, PAST_LEN, C_HEADS, 2 * C_DIM)),
        'cache_c_v': nrm((N_ODD, DEC_BATCH, PAST_LEN, C_HEADS, 2 * C_DIM)),
        'cache_d_k': nrm((N_ODD, DEC_BATCH, PAST_LEN, D_DIM)),
        'cache_d_v': nrm((N_ODD, DEC_BATCH, PAST_LEN, D_DIM)),
        'cache_d_kidx': nrm((N_ODD, DEC_BATCH, PAST_LEN, IDX_DIM)),
        't5_table': nrm((T5_BUCKETS, T5_HEADS), 0.5),
        'norm_attn': gain((DEPTH, D_MODEL)),
        'norm_ffn': gain((DEPTH, D_MODEL)),
        'final_norm': gain((D_MODEL,)),
        'even_w_in': nrm((N_EVEN, D_MODEL, e_in), D_MODEL ** -0.5),
        'even_w_out': nrm((N_EVEN, MIX_WIDTH, D_MODEL), MIX_WIDTH ** -0.5),
        'a_relbias': nrm((N_EVEN, A_HEADS, 2 * A_MAX_REL + 1), 0.5),
        'b_q_norm': gain((N_EVEN, Q_LORA)),
        'b_kv_norm': gain((N_EVEN, KV_LORA)),
        'b_w_uq': nrm((N_EVEN, Q_LORA, B_HEADS * (B_NOPE + B_ROPE)), Q_LORA ** -0.5),
        'b_w_ukv': nrm((N_EVEN, KV_LORA, B_HEADS * (B_NOPE + B_V)), KV_LORA ** -0.5),
        'odd_w_in': nrm((N_ODD, D_MODEL, o_in), D_MODEL ** -0.5),
        'odd_w_out': nrm((N_ODD, MIX_WIDTH, D_MODEL), MIX_WIDTH ** -0.5),
        'c_lambda_q1': nrm((N_ODD, C_DIM), 0.1),
        'c_lambda_k1': nrm((N_ODD, C_DIM), 0.1),
        'c_lambda_q2': nrm((N_ODD, C_DIM), 0.1),
        'c_lambda_k2': nrm((N_ODD, C_DIM), 0.1),
        'c_subln': gain((N_ODD, 2 * C_DIM)),
        'ffn_w_gate': nrm((DEPTH, D_MODEL, FF_HIDDEN), D_MODEL ** -0.5),
        'ffn_w_up': nrm((DEPTH, D_MODEL, FF_HIDDEN), D_MODEL ** -0.5),
        'ffn_w_down': nrm((DEPTH, FF_HIDDEN, D_MODEL), FF_HIDDEN ** -0.5),
    }


def reference(x_prompt, x_sample, cache_a_k, cache_a_v, cache_b_ckv, cache_b_krope, cache_c_k, cache_c_v,
              cache_d_k, cache_d_v, cache_d_kidx, t5_table, norm_attn, norm_ffn, final_norm,
              even_w_in, even_w_out, a_relbias, b_q_norm, b_kv_norm, b_w_uq, b_w_ukv,
              odd_w_in, odd_w_out, c_lambda_q1, c_lambda_k1, c_lambda_q2, c_lambda_k2, c_subln,
              ffn_w_gate, ffn_w_up, ffn_w_down):
    w = dict(t5_table=t5_table, norm_attn=norm_attn, norm_ffn=norm_ffn, final_norm=final_norm,
             even_w_in=even_w_in, even_w_out=even_w_out, a_relbias=a_relbias, b_q_norm=b_q_norm,
             b_kv_norm=b_kv_norm, b_w_uq=b_w_uq, b_w_ukv=b_w_ukv, odd_w_in=odd_w_in, odd_w_out=odd_w_out,
             c_lambda_q1=c_lambda_q1, c_lambda_k1=c_lambda_k1, c_lambda_q2=c_lambda_q2, c_lambda_k2=c_lambda_k2,
             c_subln=c_subln, ffn_w_gate=ffn_w_gate, ffn_w_up=ffn_w_up, ffn_w_down=ffn_w_down)
    n_a = cache_a_k.shape[2]
    past = cache_b_ckv.shape[2]
    y_prompt, ev_p, od_p = run_trunk(x_prompt, jnp.arange(x_prompt.shape[1]), None, n_a, w)
    caches = ((cache_a_k, cache_a_v, cache_b_ckv, cache_b_krope),
              (cache_c_k, cache_c_v, cache_d_k, cache_d_v, cache_d_kidx))
    y_sample, ev_s, od_s = run_trunk(x_sample, past + jnp.arange(x_sample.shape[1]), caches, n_a, w)
    a_k_p, a_v_p, b_ckv_p, b_krope_p = (stack_states(ev_p, j) for j in range(4))
    a_k_s, a_v_s, b_ckv_s, b_krope_s = (stack_states(ev_s, j) for j in range(4))
    c_k_p, c_v_p, d_k_p, d_v_p, d_kidx_p = (stack_states(od_p, j) for j in range(5))
    c_k_s, c_v_s, d_k_s, d_v_s, d_kidx_s = (stack_states(od_s, j) for j in range(5))
    return (y_prompt, y_sample, a_k_p, a_k_s, a_v_p, a_v_s, b_ckv_p, b_ckv_s, b_krope_p, b_krope_s,
            c_k_p, c_k_s, c_v_p, c_v_s, d_k_p, d_k_s, d_v_p, d_v_s, d_kidx_p, d_kidx_s)
```

```python
import functools
import math

import numpy as np
import jax
import jax.numpy as jnp
from jax import lax
from jax.experimental import pallas as pl
from jax.experimental.pallas import tpu as pltpu

F32, BF16, I32 = jnp.float32, jnp.bfloat16, jnp.int32

D_MODEL = 1024
CHUNK = 64
EPS = 1e-6
A_HEADS, A_DIM, A_LEFT_CHUNKS, A_MAX_REL = 8, 64, 8, 128
B_HEADS, B_NOPE, B_ROPE, B_V = 8, 64, 32, 64
Q_LORA, KV_LORA = 256, 128
ROPE_THETA = 10000.0
C_HEADS, C_DIM = 4, 64
D_HEADS, D_DIM = 8, 64
IDX_HEADS, IDX_DIM = 8, 64
TOPK_MAX = 256
T5_BUCKETS, T5_MAX_DIST = 32, 128
EVEN_SPLITS = (512, 512, 512, Q_LORA, KV_LORA, B_ROPE)
ODD_SPLITS = (512, 512, 512, 512, D_DIM, D_DIM, 512, IDX_DIM, IDX_HEADS)

LANES = 128
VMEM_LIMIT = 56 * 1024 * 1024
MASKED = -1e30
UNSELECTED = -2e30
M_INIT = -1e30
INT_MIN = -(2 ** 31)
INT_MAX = 2 ** 31 - 1

DSA_TQ, DSA_TK = 128, 256


def _cparams(n_axes):
    return pltpu.CompilerParams(dimension_semantics=("arbitrary",) * n_axes, vmem_limit_bytes=VMEM_LIMIT)


def _rms(x, g):
    return x * lax.rsqrt(jnp.mean(x * x, axis=-1, keepdims=True) + EPS) * g


def _dense_kernel(*refs, norm, res, emit_h):
    it = iter(refs)
    x_ref = next(it)
    g_ref = next(it) if norm else None
    w_ref = next(it)
    r_ref = next(it) if res else None
    o_ref = next(it)
    h_ref = next(it) if emit_h else None
    x = x_ref[...]
    if norm:
        x = _rms(x.astype(F32), g_ref[...])
        if emit_h:
            h_ref[...] = x
    acc = jnp.dot(x.astype(BF16), w_ref[...], preferred_element_type=F32)
    if res:
        acc = acc + r_ref[...]
    o_ref[...] = acc


def dense(x, w, g=None, res=None, emit_h=False, tm=512):
    m, k = x.shape
    n = w.shape[1]
    assert m % tm == 0 and n % LANES == 0
    norm = g is not None
    args, specs = [x], [pl.BlockSpec((tm, k), lambda i: (i, 0))]
    if norm:
        args.append(g.reshape(1, k).astype(F32))
        specs.append(pl.BlockSpec((1, k), lambda i: (0, 0)))
    args.append(w)
    specs.append(pl.BlockSpec((k, n), lambda i: (0, 0)))
    if res is not None:
        args.append(res)
        specs.append(pl.BlockSpec((tm, n), lambda i: (i, 0)))
    out_shape = [jax.ShapeDtypeStruct((m, n), F32)]
    out_specs = [pl.BlockSpec((tm, n), lambda i: (i, 0))]
    if emit_h:
        out_shape.append(jax.ShapeDtypeStruct((m, k), F32))
        out_specs.append(pl.BlockSpec((tm, k), lambda i: (i, 0)))
    outs = pl.pallas_call(
        functools.partial(_dense_kernel, norm=norm, res=res is not None, emit_h=emit_h),
        grid=(m // tm,), in_specs=specs, out_specs=out_specs, out_shape=out_shape,
        compiler_params=_cparams(1), name="dense")(*args)
    return outs if emit_h else outs[0]


def _ffn_kernel(x_ref, g_ref, wg_ref, wu_ref, wd_ref, fg_ref, o_ref, h_sc, acc_sc, *, final):
    f = pl.program_id(1)

    @pl.when(f == 0)
    def _():
        x = x_ref[...]
        h_sc[...] = _rms(x, g_ref[...]).astype(BF16)
        acc_sc[...] = x

    h = h_sc[...]
    gate = jnp.dot(h, wg_ref[...], preferred_element_type=F32)
    up = jnp.dot(h, wu_ref[...], preferred_element_type=F32)
    a = (gate * jax.nn.sigmoid(gate) * up).astype(BF16)
    acc_sc[...] += jnp.dot(a, wd_ref[...], preferred_element_type=F32)

    @pl.when(f == pl.num_programs(1) - 1)
    def _():
        y = acc_sc[...]
        if final:
            y = _rms(y, fg_ref[...])
        o_ref[...] = y


def ffn(x, g, wg, wu, wd, final_g=None, tm=512, nf=2):
    m, d = x.shape
    hid = wg.shape[1]
    tf = hid // nf
    assert m % tm == 0 and hid % nf == 0 and tf % LANES == 0
    final = final_g is not None
    fg = (final_g if final else g).reshape(1, d).astype(F32)
    return pl.pallas_call(
        functools.partial(_ffn_kernel, final=final),
        grid=(m // tm, nf),
        in_specs=[pl.BlockSpec((tm, d), lambda i, f: (i, 0)),
                  pl.BlockSpec((1, d), lambda i, f: (0, 0)),
                  pl.BlockSpec((d, tf), lambda i, f: (0, f)),
                  pl.BlockSpec((d, tf), lambda i, f: (0, f)),
                  pl.BlockSpec((tf, d), lambda i, f: (f, 0)),
                  pl.BlockSpec((1, d), lambda i, f: (0, 0))],
        out_specs=pl.BlockSpec((tm, d), lambda i, f: (i, 0)),
        out_shape=jax.ShapeDtypeStruct((m, d), F32),
        scratch_shapes=[pltpu.VMEM((tm, d), BF16), pltpu.VMEM((tm, d), F32)],
        compiler_params=_cparams(2), name="ffn")(x, g.reshape(1, d).astype(F32), wg, wu, wd, fg)


def _flash_kernel(q_ref, kt_ref, v_ref, tile_ref, o_ref, m_sc, l_sc, acc_sc, *, n_tiles, has_far, i_off):
    i = pl.program_id(2) + i_off
    q = q_ref[0, 0]
    m_sc[...] = jnp.full(m_sc.shape, M_INIT, F32)
    l_sc[...] = jnp.zeros(l_sc.shape, F32)
    acc_sc[...] = jnp.zeros(acc_sc.shape, F32)
    j_lo = 0 if has_far else jnp.maximum(i - (n_tiles - 1), 0)

    def body(j, carry):
        t = jnp.minimum(i - j, n_tiles)
        s = jnp.dot(q, kt_ref[0, 0, j], preferred_element_type=F32) + tile_ref[0, t]
        m_prev = m_sc[...]
        m_new = jnp.maximum(m_prev, jnp.max(s, axis=-1, keepdims=True))
        alpha = jnp.exp(m_prev - m_new)
        p = jnp.exp(s - m_new)
        l_sc[...] = alpha * l_sc[...] + jnp.sum(p, axis=-1, keepdims=True)
        acc_sc[...] = alpha * acc_sc[...] + jnp.dot(p.astype(BF16), v_ref[0, 0, j], preferred_element_type=F32)
        m_sc[...] = m_new
        return carry

    lax.fori_loop(j_lo, i + 1, body, 0)
    o_ref[0, 0] = acc_sc[...] / l_sc[...]


def flash(q, kt, v, tiles, *, n_tiles, has_far, i_off=0):
    b, g, tq, dq = q.shape
    _, gk, nb, _, blk = kt.shape
    gv, dv = v.shape[1], v.shape[-1]
    gb = tiles.shape[0]
    assert tq % blk == 0 and tiles.shape[1] == n_tiles + 1
    kdiv, vdiv, bdiv = g // gk, g // gv, g // gb
    return pl.pallas_call(
        functools.partial(_flash_kernel, n_tiles=n_tiles, has_far=has_far, i_off=i_off),
        grid=(b, g, tq // blk),
        in_specs=[pl.BlockSpec((1, 1, blk, dq), lambda bi, gi, i: (bi, gi, i, 0)),
                  pl.BlockSpec((1, 1, nb, dq, blk), lambda bi, gi, i: (bi, gi // kdiv, 0, 0, 0)),
                  pl.BlockSpec((1, 1, nb, blk, dv), lambda bi, gi, i: (bi, gi // vdiv, 0, 0, 0)),
                  pl.BlockSpec((1, n_tiles + 1, blk, blk), lambda bi, gi, i: (gi // bdiv, 0, 0, 0))],
        out_specs=pl.BlockSpec((1, 1, blk, dv), lambda bi, gi, i: (bi, gi, i, 0)),
        out_shape=jax.ShapeDtypeStruct((b, g, tq, dv), F32),
        scratch_shapes=[pltpu.VMEM((blk, 1), F32), pltpu.VMEM((blk, 1), F32), pltpu.VMEM((blk, dv), F32)],
        compiler_params=_cparams(3), name="flash")(q, kt, v, tiles)


def _dsa_kernel(qi_ref, w_ref, kit_ref, q_ref, kt_ref, v_ref, tile_ref, o_ref,
                key_sc, m_sc, l_sc, acc_sc, *, i_off, n_sel, idx_bits):
    tq, tk, nh = DSA_TQ, DSA_TK, D_HEADS
    i = pl.program_id(1) + i_off
    nkb = (i * tq) // tk + 1
    row = lax.broadcasted_iota(I32, (tq, tk), 0)
    col = lax.broadcasted_iota(I32, (tq, tk), 1)
    lane = lax.broadcasted_iota(I32, (tq, LANES), 1)
    q_chunk = jnp.right_shift(i * tq + row, 6)

    qi = qi_ref[0, 0]
    w = w_ref[0, 0]

    def score_block(j, carry):
        lg = jnp.dot(qi, kit_ref[0, j], preferred_element_type=F32)
        sc = jnp.sum((jnp.maximum(lg, 0.0) * w).reshape(IDX_HEADS, tq, tk), axis=0)
        sc = jnp.where(sc == 0.0, 0.0, sc)
        bits = pltpu.bitcast(sc, I32)
        key = bits ^ (jnp.right_shift(bits, 31) & INT_MAX)
        adm = jnp.right_shift(j * tk + col, 6) <= q_chunk
        key_sc[j] = jnp.where(adm, key, INT_MIN)
        return carry

    lax.fori_loop(0, nkb, score_block, 0)

    def count(pred_fn):
        def body(j, acc):
            k = key_sc[j]
            for c in range(tk // LANES):
                acc = acc + jnp.where(pred_fn(k[:, c * LANES:(c + 1) * LANES], j * tk + c * LANES + lane), 1.0, 0.0)
            return acc
        acc = lax.fori_loop(0, nkb, body, jnp.zeros((tq, LANES), F32))
        return jnp.sum(acc, axis=-1, keepdims=True)

    def value_bit(b, thr):
        cand = thr + jnp.left_shift(jnp.int32(1), 31 - b)
        cand_b = jnp.broadcast_to(cand, (tq, LANES))
        cnt = count(lambda k, idx: k >= cand_b)
        return jnp.where(cnt >= float(n_sel), cand, thr)

    thr = lax.fori_loop(0, 32, value_bit, jnp.full((tq, 1), INT_MIN, I32))
    thr_b = jnp.broadcast_to(thr, (tq, LANES))
    n_gt = count(lambda k, idx: k > thr_b)
    n_eq = count(lambda k, idx: k == thr_b)
    need = float(n_sel) - n_gt
    real = thr > INT_MIN

    def tie_search():
        def index_bit(b, y):
            cand = y + jnp.left_shift(jnp.int32(1), idx_bits - 1 - b)
            cand_b = jnp.broadcast_to(cand, (tq, LANES))
            cnt = count(lambda k, idx: (k == thr_b) & (idx < cand_b))
            return jnp.where(cnt < need, cand, y)
        return lax.fori_loop(0, idx_bits, index_bit, jnp.zeros((tq, 1), I32))

    has_ties = jnp.max(jnp.where((n_eq > need) & real, 1.0, 0.0)) > 0.0
    last = lax.cond(has_ties, tie_search, lambda: jnp.full((tq, 1), INT_MAX, I32))
    last = jnp.where(real, last, -1)

    q = q_ref[0, 0]
    m_sc[...] = jnp.full(m_sc.shape, M_INIT, F32)
    l_sc[...] = jnp.zeros(l_sc.shape, F32)
    acc_sc[...] = jnp.zeros(acc_sc.shape, F32)

    def attend(j, carry):
        t = jnp.minimum((i * tq - j * tk) // tq, 3)
        key = key_sc[j]
        sel = (key > thr) | ((key == thr) & (j * tk + col <= last))
        selb = jnp.where(sel, 0.0, UNSELECTED)
        s = jnp.dot(q, kt_ref[0, j], preferred_element_type=F32) + tile_ref[t]
        s = (s.reshape(nh, tq, tk) + selb[None]).reshape(nh * tq, tk)
        m_prev = m_sc[...]
        m_new = jnp.maximum(m_prev, jnp.max(s, axis=-1, keepdims=True))
        alpha = jnp.exp(m_prev - m_new)
        p = jnp.exp(s - m_new)
        l_sc[...] = alpha * l_sc[...] + jnp.sum(p, axis=-1, keepdims=True)
        acc_sc[...] = alpha * acc_sc[...] + jnp.dot(p.astype(BF16), v_ref[0, j], preferred_element_type=F32)
        m_sc[...] = m_new
        return carry

    lax.fori_loop(0, nkb, attend, 0)
    o_ref[0, 0] = acc_sc[...] / l_sc[...]


def dsa(qi3, w, kit, q, kt, v, tiles, *, i_off, n_sel):
    b, nq, rows, kdim = qi3.shape
    nkb = kit.shape[1]
    dv = v.shape[-1]
    idx_bits = max(1, (nkb * DSA_TK - 1).bit_length())
    return pl.pallas_call(
        functools.partial(_dsa_kernel, i_off=i_off, n_sel=n_sel, idx_bits=idx_bits),
        grid=(b, nq),
        in_specs=[pl.BlockSpec((1, 1, rows, kdim), lambda bi, i: (bi, i, 0, 0)),
                  pl.BlockSpec((1, 1, rows, 1), lambda bi, i: (bi, i, 0, 0)),
                  pl.BlockSpec((1, nkb, kdim, DSA_TK), lambda bi, i: (bi, 0, 0, 0)),
                  pl.BlockSpec((1, 1, rows, q.shape[-1]), lambda bi, i: (bi, i, 0, 0)),
                  pl.BlockSpec((1, nkb, kt.shape[2], DSA_TK), lambda bi, i: (bi, 0, 0, 0)),
                  pl.BlockSpec((1, nkb, DSA_TK, dv), lambda bi, i: (bi, 0, 0, 0)),
                  pl.BlockSpec((4, rows, DSA_TK), lambda bi, i: (0, 0, 0))],
        out_specs=pl.BlockSpec((1, 1, rows, dv), lambda bi, i: (bi, i, 0, 0)),
        out_shape=jax.ShapeDtypeStruct((b, nq, rows, dv), F32),
        scratch_shapes=[pltpu.VMEM((nkb, DSA_TQ, DSA_TK), I32), pltpu.VMEM((rows, 1), F32),
                        pltpu.VMEM((rows, 1), F32), pltpu.VMEM((rows, dv), F32)],
        compiler_params=_cparams(2), name="dsa")(qi3, w, kit, q, kt, v, tiles)


def _rope_tables(pos):
    half = B_ROPE // 2
    freqs = jnp.power(jnp.float32(ROPE_THETA), -jnp.arange(half, dtype=F32) / half)
    ang = pos.astype(F32)[:, None] * freqs[None, :]
    return jnp.cos(ang), jnp.sin(ang)


def _rope(x, cos, sin):
    half = x.shape[-1] // 2
    shape = (x.shape[0],) + (1,) * (x.ndim - 2) + (half,)
    c, s = cos.reshape(shape), sin.reshape(shape)
    x1, x2 = x[..., :half], x[..., half:]
    return jnp.concatenate([x1 * c - x2 * s, x1 * s + x2 * c], axis=-1)


def _t5_bucket(rel):
    nb = T5_BUCKETS // 2
    max_exact = nb // 2
    ret = jnp.where(rel > 0, nb, 0)
    n = jnp.abs(rel)
    nf = jnp.maximum(n, 1).astype(F32)
    large = max_exact + (jnp.log(nf / max_exact) / math.log(T5_MAX_DIST / max_exact)
                         * (nb - max_exact)).astype(I32)
    large = jnp.minimum(large, nb - 1)
    return ret + jnp.where(n < max_exact, n, large)


def _tile_geometry(tq, tk, n_tiles):
    r = np.arange(tq)[None, :, None]
    c = np.arange(tk)[None, None, :]
    t = np.arange(n_tiles)[:, None, None]
    rel = c - r - t * tq
    dchunk = c // CHUNK - r // CHUNK - t * (tq // CHUNK)
    return rel, dchunk


def band_tiles(relbias, blk):
    n_tiles = (A_LEFT_CHUNKS * CHUNK) // blk + 1
    rel, dchunk = _tile_geometry(blk, blk, n_tiles)
    adm = (dchunk <= 0) & (-dchunk <= A_LEFT_CHUNKS)
    idx = np.clip(-rel, -A_MAX_REL, A_MAX_REL) + A_MAX_REL
    bias = relbias.astype(F32)[:, idx]
    tiles = jnp.where(adm[None], bias, MASKED)
    far = jnp.full((relbias.shape[0], 1, blk, blk), MASKED, F32)
    return jnp.concatenate([tiles, far], axis=1), n_tiles


def causal_tiles(blk):
    _, dchunk = _tile_geometry(blk, blk, 1)
    t0 = np.where(dchunk <= 0, 0.0, MASKED).astype(np.float32)
    return jnp.asarray(np.concatenate([t0, np.zeros_like(t0)], axis=0)[None]), 1


def t5_tiles(tab, tq, tk, n_tiles):
    assert (n_tiles - 1) * tq + 1 - (tk - tq) >= T5_MAX_DIST
    rel, dchunk = _tile_geometry(tq, tk, n_tiles)
    bias = jnp.moveaxis(tab.astype(F32)[_t5_bucket(jnp.asarray(rel, I32))], -1, 0)
    tiles = jnp.where((dchunk <= 0)[None], bias, MASKED)
    far = tab.astype(F32)[_t5_bucket(jnp.asarray([-(n_tiles * tq + tk)], I32))[0]]
    far = jnp.broadcast_to(far[:, None, None, None], (tab.shape[1], 1, tq, tk))
    return jnp.concatenate([tiles, far], axis=1)


def _q_heads(x):
    return jnp.transpose(x, (0, 2, 1, 3)).astype(BF16)


def _k_blocks_t(x, blk):
    b, t, g, d = x.shape
    return jnp.transpose(x.reshape(b, t // blk, blk, g, d), (0, 3, 1, 4, 2)).astype(BF16)


def _v_blocks(x, blk):
    b, t, g, d = x.shape
    return jnp.transpose(x.reshape(b, t // blk, blk, g, d), (0, 3, 1, 2, 4)).astype(BF16)


def _from_heads(o):
    b, g, t, d = o.shape
    return jnp.transpose(o, (0, 2, 1, 3)).reshape(b, t, g * d)


def _pad_rows(x, n, front=0):
    back = n - x.shape[1] - front
    return jnp.pad(x, ((0, 0), (front, back)) + ((0, 0),) * (x.ndim - 2))


def _split(h, sizes):
    out, o = [], 0
    for s in sizes:
        out.append(h[:, o:o + s])
        o += s
    return out


def _split3(x):
    hi = x.astype(BF16)
    lo = (x - hi.astype(F32)).astype(BF16)
    return hi, lo


def _dsa_call(dq, dqi, dw, dk_all, dv_all, dki_all, tiles, i_off, n_sel):
    b, tq_all = dq.shape[:2]
    nq, nkb = tq_all // DSA_TQ, dk_all.shape[1] // DSA_TK

    def stack(x):
        d = x.shape[-1]
        return jnp.transpose(x.reshape(b, nq, DSA_TQ, D_HEADS, d), (0, 1, 3, 2, 4)).reshape(b, nq, D_HEADS * DSA_TQ, d)

    qh, ql = _split3(dqi)
    kh, kl = _split3(dki_all)
    qi3 = stack(jnp.concatenate([qh, ql, qh], axis=-1))
    kit = jnp.transpose(jnp.concatenate([kh, kh, kl], axis=-1).reshape(b, nkb, DSA_TK, 3 * IDX_DIM), (0, 1, 3, 2))
    w = stack(dw[..., None])
    q = stack((dq * (D_DIM ** -0.5)).astype(BF16))
    kt = jnp.transpose(dk_all.astype(BF16).reshape(b, nkb, DSA_TK, D_DIM), (0, 1, 3, 2))
    v = dv_all.astype(BF16).reshape(b, nkb, DSA_TK, D_DIM)
    o = dsa(qi3, w, kit, q, kt, v, tiles, i_off=i_off, n_sel=n_sel)
    o = jnp.transpose(o.reshape(b, nq, D_HEADS, DSA_TQ, D_DIM), (0, 1, 3, 2, 4))
    return o.reshape(b, tq_all, D_HEADS * D_DIM)


P_BLK = 256
S_BLK = 128


def kernel(x_prompt, x_sample, cache_a_k, cache_a_v, cache_b_ckv, cache_b_krope, cache_c_k, cache_c_v, cache_d_k, cache_d_v, cache_d_kidx, t5_table, norm_attn, norm_ffn, final_norm, even_w_in, even_w_out, a_relbias, b_q_norm, b_kv_norm, b_w_uq, b_w_ukv, odd_w_in, odd_w_out, c_lambda_q1, c_lambda_k1, c_lambda_q2, c_lambda_k2, c_subln, ffn_w_gate, ffn_w_up, ffn_w_down):
    pb, pt, d = x_prompt.shape
    sb, st, _ = x_sample.shape
    assert pb == 1 and st == CHUNK
    past = cache_b_ckv.shape[2]
    n_a = cache_a_k.shape[2]
    depth = norm_attn.shape[0]
    n_p, n_s = pb * pt, sb * st
    s_len = past + st
    s_pad = -(-s_len // DSA_TK) * DSA_TK
    s_i = past // S_BLK
    s_i_dsa = past // DSA_TQ
    assert past % S_BLK == 0 and s_pad % S_BLK == 0 and n_a + st <= s_pad - (past - n_a)

    x = jnp.concatenate([x_prompt.reshape(n_p, d), x_sample.reshape(n_s, d)], axis=0)
    pos = jnp.concatenate([jnp.arange(pt), jnp.tile(past + jnp.arange(st), sb)])
    cos, sin = _rope_tables(pos)

    def rows_p(a):
        return a[:n_p].reshape(pb, pt, *a.shape[1:])

    def rows_s(a):
        return a[n_p:].reshape(sb, st, *a.shape[1:])

    def pad_cols(w):
        n = -(-w.shape[1] // LANES) * LANES
        return jnp.pad(w, ((0, 0), (0, n - w.shape[1]))).astype(BF16)

    def sample_q(a):
        return _pad_rows(a, S_BLK)

    def sample_keys(cache, new, front=0):
        return _pad_rows(jnp.concatenate([cache, new], axis=1), s_pad, front)

    t5_c, t5_d = t5_table[:, :C_HEADS], t5_table[:, C_HEADS:]
    ctiles_p, _ = causal_tiles(P_BLK)
    ctiles_s, _ = causal_tiles(S_BLK)
    c_tiles_p = t5_tiles(t5_c, P_BLK, P_BLK, 2)
    c_tiles_s = t5_tiles(t5_c, S_BLK, S_BLK, 2)
    d_tiles = t5_tiles(t5_d, DSA_TQ, DSA_TK, 3)
    d_tiles = jnp.transpose(d_tiles, (1, 0, 2, 3)).reshape(4, D_HEADS * DSA_TQ, DSA_TK)
    n_sel_p = min(TOPK_MAX, pt // 4)
    n_sel_s = min(TOPK_MAX, s_len // 4)

    outs_even = {k: [] for k in ("a_k_p", "a_k_s", "a_v_p", "a_v_s", "ckv_p", "ckv_s", "kr_p", "kr_s")}
    outs_odd = {k: [] for k in ("c_k_p", "c_k_s", "c_v_p", "c_v_s", "d_k_p", "d_k_s", "d_v_p", "d_v_s", "d_ki_p", "d_ki_s")}

    for l in range(depth):
        i = l // 2
        if l % 2 == 0:
            h = dense(x, pad_cols(even_w_in[i]), g=norm_attn[l])
            aq, ak, av, cq, ckv_raw, kr_raw = _split(h, EVEN_SPLITS)
            qb = dense(cq, b_w_uq[i].astype(BF16), g=b_q_norm[i]).reshape(-1, B_HEADS, B_NOPE + B_ROPE)
            kv_new, ckv = dense(ckv_raw, b_w_ukv[i].astype(BF16), g=b_kv_norm[i], emit_h=True)
            kr = _rope(kr_raw, cos, sin)
            q_mla = jnp.concatenate([qb[..., :B_NOPE], _rope(qb[..., B_NOPE:], cos, sin)], axis=-1)
            q_mla = q_mla * ((B_NOPE + B_ROPE) ** -0.5)
            kv_new = kv_new.reshape(-1, B_HEADS, B_NOPE + B_V)
            aq = (aq * (A_DIM ** -0.5)).reshape(-1, A_HEADS, A_DIM)
            ak = ak.reshape(-1, A_HEADS, A_DIM)
            av = av.reshape(-1, A_HEADS, A_DIM)

            def mla_keys(kn, krope):
                return jnp.concatenate([kn, jnp.broadcast_to(krope[:, :, None, :], kn.shape[:3] + (B_ROPE,))], axis=-1)

            a_tiles_p, nt_p = band_tiles(a_relbias[i], P_BLK)
            a_out_p = flash(_q_heads(rows_p(aq)), _k_blocks_t(rows_p(ak), P_BLK), _v_blocks(rows_p(av), P_BLK),
                            a_tiles_p, n_tiles=nt_p, has_far=False)
            kv_p = rows_p(kv_new)
            b_out_p = flash(_q_heads(rows_p(q_mla)), _k_blocks_t(mla_keys(kv_p[..., :B_NOPE], rows_p(kr)), P_BLK),
                            _v_blocks(kv_p[..., B_NOPE:], P_BLK), ctiles_p, n_tiles=1, has_far=True)
            a_tiles_s, nt_s = band_tiles(a_relbias[i], S_BLK)
            ak_all = jnp.concatenate([cache_a_k[i], rows_s(ak)], axis=1)
            av_all = jnp.concatenate([cache_a_v[i], rows_s(av)], axis=1)
            a_out_s = flash(_q_heads(sample_q(rows_s(aq))),
                            _k_blocks_t(_pad_rows(ak_all, s_pad, past - n_a), S_BLK),
                            _v_blocks(_pad_rows(av_all, s_pad, past - n_a), S_BLK),
                            a_tiles_s, n_tiles=nt_s, has_far=False, i_off=s_i)[:, :, :st]
            kv_c = dense(cache_b_ckv[i].reshape(sb * past, KV_LORA), b_w_ukv[i].astype(BF16))
            kv_c = kv_c.reshape(sb, past, B_HEADS, B_NOPE + B_V)
            kv_s = rows_s(kv_new)
            kn_all = sample_keys(kv_c[..., :B_NOPE], kv_s[..., :B_NOPE])
            v_all = sample_keys(kv_c[..., B_NOPE:], kv_s[..., B_NOPE:])
            kr_all = sample_keys(cache_b_krope[i], rows_s(kr))
            b_out_s = flash(_q_heads(sample_q(rows_s(q_mla))), _k_blocks_t(mla_keys(kn_all, kr_all), S_BLK),
                            _v_blocks(v_all, S_BLK), ctiles_s, n_tiles=1, has_far=True, i_off=s_i)[:, :, :st]

            mix = jnp.concatenate([
                jnp.concatenate([_from_heads(a_out_p).reshape(n_p, -1), _from_heads(b_out_p).reshape(n_p, -1)], axis=-1),
                jnp.concatenate([_from_heads(a_out_s).reshape(n_s, -1), _from_heads(b_out_s).reshape(n_s, -1)], axis=-1),
            ], axis=0)
            x = dense(mix.astype(BF16), even_w_out[i].astype(BF16), res=x)

            outs_even["a_k_p"].append(rows_p(ak)[:, pt - n_a:])
            outs_even["a_v_p"].append(rows_p(av)[:, pt - n_a:])
            outs_even["a_k_s"].append(ak_all[:, -n_a:])
            outs_even["a_v_s"].append(av_all[:, -n_a:])
            outs_even["ckv_p"].append(rows_p(ckv))
            outs_even["ckv_s"].append(rows_s(ckv))
            outs_even["kr_p"].append(rows_p(kr))
            outs_even["kr_s"].append(rows_s(kr))
        else:
            lam_init = 0.8 - 0.6 * math.exp(-0.3 * l)
            lam = (jnp.exp(jnp.sum(c_lambda_q1[i].astype(F32) * c_lambda_k1[i].astype(F32)))
                   - jnp.exp(jnp.sum(c_lambda_q2[i].astype(F32) * c_lambda_k2[i].astype(F32))) + lam_init)
            h = dense(x, pad_cols(odd_w_in[i]), g=norm_attn[l])
            cq, ck, cv, dq, dk, dv, dqi, dki, dw = _split(h, ODD_SPLITS)
            cq = (cq * (C_DIM ** -0.5)).reshape(-1, 2 * C_HEADS, C_DIM)
            ck3 = ck.reshape(-1, 2 * C_HEADS, C_DIM)
            cv3 = cv.reshape(-1, C_HEADS, 2 * C_DIM)
            dq = dq.reshape(-1, D_HEADS, D_DIM)
            dqi = dqi.reshape(-1, IDX_HEADS, IDX_DIM)
            dw = dw * ((IDX_HEADS ** -0.5) * (IDX_DIM ** -0.5))

            def diff_combine(o):
                b_, _, t_, _ = o.shape
                o = o.reshape(b_, C_HEADS, 2, t_, 2 * C_DIM)
                c = jnp.transpose(o[:, :, 0] - lam * o[:, :, 1], (0, 2, 1, 3))
                return (_rms(c, c_subln[i].astype(F32)) * (1.0 - lam_init)).reshape(b_, t_, -1)

            c_out_p = flash(_q_heads(rows_p(cq)), _k_blocks_t(rows_p(ck3), P_BLK), _v_blocks(rows_p(cv3), P_BLK),
                            c_tiles_p, n_tiles=2, has_far=True)
            d_out_p = _dsa_call(rows_p(dq), rows_p(dqi), rows_p(dw), rows_p(dk), rows_p(dv), rows_p(dki),
                                d_tiles, 0, n_sel_p)
            ck_all = sample_keys(cache_c_k[i].reshape(sb, past, 2 * C_HEADS, C_DIM), rows_s(ck3))
            cv_all = sample_keys(cache_c_v[i], rows_s(cv3))
            c_out_s = flash(_q_heads(sample_q(rows_s(cq))), _k_blocks_t(ck_all, S_BLK), _v_blocks(cv_all, S_BLK),
                            c_tiles_s, n_tiles=2, has_far=True, i_off=s_i)[:, :, :st]
            d_out_s = _dsa_call(_pad_rows(rows_s(dq), DSA_TQ), _pad_rows(rows_s(dqi), DSA_TQ),
                                _pad_rows(rows_s(dw), DSA_TQ),
                                sample_keys(cache_d_k[i], rows_s(dk)), sample_keys(cache_d_v[i], rows_s(dv)),
                                sample_keys(cache_d_kidx[i], rows_s(dki)), d_tiles, s_i_dsa, n_sel_s)[:, :st]

            mix = jnp.concatenate([
                jnp.concatenate([diff_combine(c_out_p).reshape(n_p, -1), d_out_p.reshape(n_p, -1)], axis=-1),
                jnp.concatenate([diff_combine(c_out_s).reshape(n_s, -1), d_out_s.reshape(n_s, -1)], axis=-1),
            ], axis=0)
            x = dense(mix.astype(BF16), odd_w_out[i].astype(BF16), res=x)

            for name, arr, shp in (("c_k", ck, (C_HEADS, 2 * C_DIM)), ("c_v", cv, (C_HEADS, 2 * C_DIM)),
                                   ("d_k", dk, (D_DIM,)), ("d_v", dv, (D_DIM,)), ("d_ki", dki, (IDX_DIM,))):
                outs_odd[name + "_p"].append(rows_p(arr).reshape(pb, pt, *shp))
                outs_odd[name + "_s"].append(rows_s(arr).reshape(sb, st, *shp))

        x = ffn(x, norm_ffn[l], ffn_w_gate[l].astype(BF16), ffn_w_up[l].astype(BF16), ffn_w_down[l].astype(BF16),
                final_g=final_norm if l == depth - 1 else None)

    y_prompt = x[:n_p].reshape(pb, pt, d)
    y_sample = x[n_p:].reshape(sb, st, d)
    se = {k: jnp.stack(v, axis=0) for k, v in outs_even.items()}
    so = {k: jnp.stack(v, axis=0) for k, v in outs_odd.items()}
    return (y_prompt, y_sample, se["a_k_p"], se["a_k_s"], se["a_v_p"], se["a_v_s"],
            se["ckv_p"], se["ckv_s"], se["kr_p"], se["kr_s"],
            so["c_k_p"], so["c_k_s"], so["c_v_p"], so["c_v_s"],
            so["d_k_p"], so["d_k_s"], so["d_v_p"], so["d_v_s"], so["d_ki_p"], so["d_ki_s"])
```

```python
import functools
import math

import numpy as np
import jax
import jax.numpy as jnp
from jax import lax
from jax.experimental import pallas as pl
from jax.experimental.pallas import tpu as pltpu

F32, BF16, I32 = jnp.float32, jnp.bfloat16, jnp.int32

D_MODEL = 1024
CHUNK = 64
EPS = 1e-6
A_HEADS, A_DIM, A_LEFT_CHUNKS, A_MAX_REL = 8, 64, 8, 128
B_HEADS, B_NOPE, B_ROPE, B_V = 8, 64, 32, 64
Q_LORA, KV_LORA = 256, 128
ROPE_THETA = 10000.0
C_HEADS, C_DIM = 4, 64
D_HEADS, D_DIM = 8, 64
IDX_HEADS, IDX_DIM = 8, 64
TOPK_MAX = 256
T5_BUCKETS, T5_MAX_DIST = 32, 128
EVEN_SPLITS = (512, 512, 512, Q_LORA, KV_LORA, B_ROPE)
ODD_SPLITS = (512, 512, 512, 512, D_DIM, D_DIM, 512, IDX_DIM, IDX_HEADS)

LANES = 128
SUBLANES = 8
VMEM_LIMIT = 56 * 1024 * 1024
MASKED = -1e30
UNSELECTED = -2e30
M_INIT = -1e30
INT_MIN = -(2 ** 31)
INT_MAX = 2 ** 31 - 1

DSA_TQ, DSA_TK = 128, 256
COUNT_ROWS = 8 * SUBLANES


def _cparams(n_axes):
    return pltpu.CompilerParams(dimension_semantics=("arbitrary",) * n_axes, vmem_limit_bytes=VMEM_LIMIT)


def _rms(x, g):
    return x * lax.rsqrt(jnp.mean(x * x, axis=-1, keepdims=True) + EPS) * g


def _dense_kernel(*refs, norm, res, emit_h):
    it = iter(refs)
    x_ref = next(it)
    g_ref = next(it) if norm else None
    w_ref = next(it)
    r_ref = next(it) if res else None
    o_ref = next(it)
    h_ref = next(it) if emit_h else None
    x = x_ref[...]
    if norm:
        x = _rms(x.astype(F32), g_ref[...])
        if emit_h:
            h_ref[...] = x
    acc = jnp.dot(x.astype(BF16), w_ref[...], preferred_element_type=F32)
    if res:
        acc = acc + r_ref[...]
    o_ref[...] = acc


def dense(x, w, g=None, res=None, emit_h=False, tm=512):
    m, k = x.shape
    n = w.shape[1]
    assert m % tm == 0 and n % LANES == 0
    norm = g is not None
    args, specs = [x], [pl.BlockSpec((tm, k), lambda i: (i, 0))]
    if norm:
        args.append(g.reshape(1, k).astype(F32))
        specs.append(pl.BlockSpec((1, k), lambda i: (0, 0)))
    args.append(w)
    specs.append(pl.BlockSpec((k, n), lambda i: (0, 0)))
    if res is not None:
        args.append(res)
        specs.append(pl.BlockSpec((tm, n), lambda i: (i, 0)))
    out_shape = [jax.ShapeDtypeStruct((m, n), F32)]
    out_specs = [pl.BlockSpec((tm, n), lambda i: (i, 0))]
    if emit_h:
        out_shape.append(jax.ShapeDtypeStruct((m, k), F32))
        out_specs.append(pl.BlockSpec((tm, k), lambda i: (i, 0)))
    outs = pl.pallas_call(
        functools.partial(_dense_kernel, norm=norm, res=res is not None, emit_h=emit_h),
        grid=(m // tm,), in_specs=specs, out_specs=out_specs, out_shape=out_shape,
        compiler_params=_cparams(1), name="dense")(*args)
    return outs if emit_h else outs[0]


def _ffn_kernel(x_ref, g_ref, wg_ref, wu_ref, wd_ref, fg_ref, o_ref, h_sc, acc_sc, *, final):
    f = pl.program_id(1)

    @pl.when(f == 0)
    def _():
        x = x_ref[...]
        h_sc[...] = _rms(x, g_ref[...]).astype(BF16)
        acc_sc[...] = x

    h = h_sc[...]
    gate = jnp.dot(h, wg_ref[...], preferred_element_type=F32)
    up = jnp.dot(h, wu_ref[...], preferred_element_type=F32)
    a = (gate * jax.nn.sigmoid(gate) * up).astype(BF16)
    acc_sc[...] += jnp.dot(a, wd_ref[...], preferred_element_type=F32)

    @pl.when(f == pl.num_programs(1) - 1)
    def _():
        y = acc_sc[...]
        if final:
            y = _rms(y, fg_ref[...])
        o_ref[...] = y


def ffn(x, g, wg, wu, wd, final_g=None, tm=512, nf=2):
    m, d = x.shape
    hid = wg.shape[1]
    tf = hid // nf
    assert m % tm == 0 and hid % nf == 0 and tf % LANES == 0
    final = final_g is not None
    fg = (final_g if final else g).reshape(1, d).astype(F32)
    return pl.pallas_call(
        functools.partial(_ffn_kernel, final=final),
        grid=(m // tm, nf),
        in_specs=[pl.BlockSpec((tm, d), lambda i, f: (i, 0)),
                  pl.BlockSpec((1, d), lambda i, f: (0, 0)),
                  pl.BlockSpec((d, tf), lambda i, f: (0, f)),
                  pl.BlockSpec((d, tf), lambda i, f: (0, f)),
                  pl.BlockSpec((tf, d), lambda i, f: (f, 0)),
                  pl.BlockSpec((1, d), lambda i, f: (0, 0))],
        out_specs=pl.BlockSpec((tm, d), lambda i, f: (i, 0)),
        out_shape=jax.ShapeDtypeStruct((m, d), F32),
        scratch_shapes=[pltpu.VMEM((tm, d), BF16), pltpu.VMEM((tm, d), F32)],
        compiler_params=_cparams(2), name="ffn")(x, g.reshape(1, d).astype(F32), wg, wu, wd, fg)


def _flash_kernel(qt_ref, k_ref, vt_ref, tile_ref, o_ref, m_sc, l_sc, acc_sc, *, n_tiles, has_far, i_off):
    i = pl.program_id(2) + i_off
    qt = qt_ref[0, 0, 0]
    m_sc[...] = jnp.full(m_sc.shape, M_INIT, F32)
    l_sc[...] = jnp.zeros(l_sc.shape, F32)
    acc_sc[...] = jnp.zeros(acc_sc.shape, F32)
    j_lo = 0 if has_far else jnp.maximum(i - (n_tiles - 1), 0)

    def scores(j):
        t = jnp.minimum(i - j, n_tiles)
        return jnp.dot(k_ref[0, 0, j], qt, preferred_element_type=F32) + tile_ref[0, t]

    def body(j, s):
        s_next = scores(jnp.minimum(j + 1, i))
        m_prev = m_sc[...]
        m_new = jnp.maximum(m_prev, jnp.max(s, axis=0, keepdims=True))
        alpha = jnp.exp(m_prev - m_new)
        p = jnp.exp(s - m_new)
        l_sc[...] = alpha * l_sc[...] + jnp.sum(p, axis=0, keepdims=True)
        acc_sc[...] = alpha * acc_sc[...] + jnp.dot(vt_ref[0, 0, j], p.astype(BF16), preferred_element_type=F32)
        m_sc[...] = m_new
        return s_next

    lax.fori_loop(j_lo, i + 1, body, scores(j_lo))
    o_ref[0, 0, 0] = acc_sc[...] / l_sc[...]


def flash(qt, k, vt, tiles, *, n_tiles, has_far, i_off=0):
    b, g, nq, dq, blk = qt.shape
    gk, nb = k.shape[1], k.shape[2]
    gv, dv = vt.shape[1], vt.shape[3]
    gb = tiles.shape[0]
    assert tiles.shape[1] == n_tiles + 1
    kdiv, vdiv, bdiv = g // gk, g // gv, g // gb
    return pl.pallas_call(
        functools.partial(_flash_kernel, n_tiles=n_tiles, has_far=has_far, i_off=i_off),
        grid=(b, g, nq),
        in_specs=[pl.BlockSpec((1, 1, 1, dq, blk), lambda bi, gi, i: (bi, gi, i, 0, 0)),
                  pl.BlockSpec((1, 1, nb, blk, dq), lambda bi, gi, i: (bi, gi // kdiv, 0, 0, 0)),
                  pl.BlockSpec((1, 1, nb, dv, blk), lambda bi, gi, i: (bi, gi // vdiv, 0, 0, 0)),
                  pl.BlockSpec((1, n_tiles + 1, blk, blk), lambda bi, gi, i: (gi // bdiv, 0, 0, 0))],
        out_specs=pl.BlockSpec((1, 1, 1, dv, blk), lambda bi, gi, i: (bi, gi, i, 0, 0)),
        out_shape=jax.ShapeDtypeStruct((b, g, nq, dv, blk), F32),
        scratch_shapes=[pltpu.VMEM((1, blk), F32), pltpu.VMEM((1, blk), F32), pltpu.VMEM((dv, blk), F32)],
        compiler_params=_cparams(3), name="flash")(qt, k, vt, tiles)


def _dsa_kernel(qit_ref, w_ref, ki_ref, qt_ref, k_ref, vt_ref, tile_ref, o_ref,
                key_sc, m_sc, l_sc, acc_sc, *, i_off, n_sel, idx_bits):
    tq, tk, nh = DSA_TQ, DSA_TK, D_HEADS
    i = pl.program_id(1) + i_off
    nkb = (i * tq) // tk + 1
    krow = lax.broadcasted_iota(I32, (tk, tq), 0)
    qcol = lax.broadcasted_iota(I32, (tk, tq), 1)
    q_chunk = jnp.right_shift(i * tq + qcol, 6)

    def head(x, h):
        return x[:, h * tq:(h + 1) * tq]

    qit = qit_ref[0, 0]
    w = w_ref[0, 0]

    def score_block(j, carry):
        lg = jnp.maximum(jnp.dot(ki_ref[0, j], qit, preferred_element_type=F32), 0.0) * w
        sc = head(lg, 0)
        for h in range(1, IDX_HEADS):
            sc = sc + head(lg, h)
        sc = jnp.where(sc == 0.0, 0.0, sc)
        bits = pltpu.bitcast(sc, I32)
        key = bits ^ (jnp.right_shift(bits, 31) & INT_MAX)
        adm = jnp.right_shift(j * tk + krow, 6) <= q_chunk
        key_sc[j] = jnp.where(adm, key, INT_MIN)
        return carry

    lax.fori_loop(0, nkb, score_block, 0)

    def count(pred_fn):
        def body(j, acc):
            ind = jnp.where(pred_fn(key_sc[j], j * tk + krow), 1.0, 0.0)
            return acc + jnp.sum(ind.reshape(tk // COUNT_ROWS, COUNT_ROWS, tq), axis=0)
        acc = lax.fori_loop(0, nkb, body, jnp.zeros((COUNT_ROWS, tq), F32))
        return jnp.sum(acc, axis=0, keepdims=True)

    def value_bit(b, thr):
        cand = thr + jnp.left_shift(jnp.int32(1), 31 - b)
        cnt = count(lambda k, idx: k >= cand)
        return jnp.where(cnt >= float(n_sel), cand, thr)

    thr = lax.fori_loop(0, 32, value_bit, jnp.full((1, tq), INT_MIN, I32))
    n_gt = count(lambda k, idx: k > thr)
    n_eq = count(lambda k, idx: k == thr)
    need = float(n_sel) - n_gt
    real = thr > INT_MIN

    def tie_search():
        def index_bit(b, y):
            cand = y + jnp.left_shift(jnp.int32(1), idx_bits - 1 - b)
            cnt = count(lambda k, idx: (k == thr) & (idx < cand))
            return jnp.where(cnt < need, cand, y)
        return lax.fori_loop(0, idx_bits, index_bit, jnp.zeros((1, tq), I32))

    has_ties = jnp.max(jnp.where((n_eq > need) & real, 1.0, 0.0)) > 0.0
    last = lax.cond(has_ties, tie_search, lambda: jnp.full((1, tq), INT_MAX, I32))
    last = jnp.where(real, last, -1)

    qt = qt_ref[0, 0]
    m_sc[...] = jnp.full(m_sc.shape, M_INIT, F32)
    l_sc[...] = jnp.zeros(l_sc.shape, F32)
    acc_sc[...] = jnp.zeros(acc_sc.shape, F32)

    def attend(j, carry):
        t = jnp.minimum((i * tq - j * tk) // tq, 3)
        key = key_sc[j]
        sel = (key > thr) | ((key == thr) & (j * tk + krow <= last))
        selb = jnp.where(sel, 0.0, UNSELECTED)
        s = jnp.dot(k_ref[0, j], qt, preferred_element_type=F32) + tile_ref[t]
        s = jnp.concatenate([head(s, h) + selb for h in range(nh)], axis=1)
        m_prev = m_sc[...]
        m_new = jnp.maximum(m_prev, jnp.max(s, axis=0, keepdims=True))
        alpha = jnp.exp(m_prev - m_new)
        p = jnp.exp(s - m_new)
        l_sc[...] = alpha * l_sc[...] + jnp.sum(p, axis=0, keepdims=True)
        acc_sc[...] = alpha * acc_sc[...] + jnp.dot(vt_ref[0, j], p.astype(BF16), preferred_element_type=F32)
        m_sc[...] = m_new
        return carry

    lax.fori_loop(0, nkb, attend, 0)
    o_ref[0, 0] = acc_sc[...] / l_sc[...]


def dsa(qit, w, ki, qt, k, vt, tiles, *, i_off, n_sel):
    b, nq, kdim, cols = qit.shape
    nkb = ki.shape[1]
    dk, dv = k.shape[-1], vt.shape[2]
    idx_bits = max(1, (nkb * DSA_TK - 1).bit_length())
    return pl.pallas_call(
        functools.partial(_dsa_kernel, i_off=i_off, n_sel=n_sel, idx_bits=idx_bits),
        grid=(b, nq),
        in_specs=[pl.BlockSpec((1, 1, kdim, cols), lambda bi, i: (bi, i, 0, 0)),
                  pl.BlockSpec((1, 1, 1, cols), lambda bi, i: (bi, i, 0, 0)),
                  pl.BlockSpec((1, nkb, DSA_TK, kdim), lambda bi, i: (bi, 0, 0, 0)),
                  pl.BlockSpec((1, 1, dk, cols), lambda bi, i: (bi, i, 0, 0)),
                  pl.BlockSpec((1, nkb, DSA_TK, dk), lambda bi, i: (bi, 0, 0, 0)),
                  pl.BlockSpec((1, nkb, dv, DSA_TK), lambda bi, i: (bi, 0, 0, 0)),
                  pl.BlockSpec((4, DSA_TK, cols), lambda bi, i: (0, 0, 0))],
        out_specs=pl.BlockSpec((1, 1, dv, cols), lambda bi, i: (bi, i, 0, 0)),
        out_shape=jax.ShapeDtypeStruct((b, nq, dv, cols), F32),
        scratch_shapes=[pltpu.VMEM((nkb, DSA_TK, DSA_TQ), I32), pltpu.VMEM((1, cols), F32),
                        pltpu.VMEM((1, cols), F32), pltpu.VMEM((dv, cols), F32)],
        compiler_params=_cparams(2), name="dsa")(qit, w, ki, qt, k, vt, tiles)


def _rope_tables(pos):
    half = B_ROPE // 2
    freqs = jnp.power(jnp.float32(ROPE_THETA), -jnp.arange(half, dtype=F32) / half)
    ang = pos.astype(F32)[:, None] * freqs[None, :]
    return jnp.cos(ang), jnp.sin(ang)


def _rope(x, cos, sin):
    half = x.shape[-1] // 2
    shape = (x.shape[0],) + (1,) * (x.ndim - 2) + (half,)
    c, s = cos.reshape(shape), sin.reshape(shape)
    x1, x2 = x[..., :half], x[..., half:]
    return jnp.concatenate([x1 * c - x2 * s, x1 * s + x2 * c], axis=-1)


def _t5_bucket(rel):
    nb = T5_BUCKETS // 2
    max_exact = nb // 2
    ret = jnp.where(rel > 0, nb, 0)
    n = jnp.abs(rel)
    nf = jnp.maximum(n, 1).astype(F32)
    large = max_exact + (jnp.log(nf / max_exact) / math.log(T5_MAX_DIST / max_exact)
                         * (nb - max_exact)).astype(I32)
    large = jnp.minimum(large, nb - 1)
    return ret + jnp.where(n < max_exact, n, large)


def _tile_geometry(tq, tk, n_tiles):
    r = np.arange(tq)[None, None, :]
    c = np.arange(tk)[None, :, None]
    t = np.arange(n_tiles)[:, None, None]
    rel = c - r - t * tq
    dchunk = c // CHUNK - r // CHUNK - t * (tq // CHUNK)
    return rel, dchunk


def band_tiles(relbias, blk):
    n_tiles = (A_LEFT_CHUNKS * CHUNK) // blk + 1
    rel, dchunk = _tile_geometry(blk, blk, n_tiles)
    adm = (dchunk <= 0) & (-dchunk <= A_LEFT_CHUNKS)
    idx = np.clip(-rel, -A_MAX_REL, A_MAX_REL) + A_MAX_REL
    bias = relbias.astype(F32)[:, idx]
    tiles = jnp.where(adm[None], bias, MASKED)
    far = jnp.full((relbias.shape[0], 1, blk, blk), MASKED, F32)
    return jnp.concatenate([tiles, far], axis=1), n_tiles


def causal_tiles(blk):
    _, dchunk = _tile_geometry(blk, blk, 1)
    t0 = np.where(dchunk <= 0, 0.0, MASKED).astype(np.float32)
    return jnp.asarray(np.concatenate([t0, np.zeros_like(t0)], axis=0)[None])


def t5_tiles(tab, tq, tk, n_tiles):
    assert n_tiles * tq - tk + 1 >= T5_MAX_DIST
    rel, dchunk = _tile_geometry(tq, tk, n_tiles)
    bias = jnp.moveaxis(tab.astype(F32)[_t5_bucket(jnp.asarray(rel, I32))], -1, 0)
    tiles = jnp.where((dchunk <= 0)[None], bias, MASKED)
    far = tab.astype(F32)[_t5_bucket(jnp.asarray([-(n_tiles * tq + tk)], I32))[0]]
    far = jnp.broadcast_to(far[:, None, None, None], (tab.shape[1], 1, tk, tq))
    return jnp.concatenate([tiles, far], axis=1)


def _row_blocks(x, blk):
    b, t, g, d = x.shape
    return jnp.transpose(x.reshape(b, t // blk, blk, g, d), (0, 3, 1, 2, 4)).astype(BF16)


def _col_blocks(x, blk):
    b, t, g, d = x.shape
    return jnp.transpose(x.reshape(b, t // blk, blk, g, d), (0, 3, 1, 4, 2)).astype(BF16)


def _from_col_blocks(o):
    b, g, nq, d, blk = o.shape
    return jnp.transpose(o, (0, 2, 4, 1, 3)).reshape(b, nq * blk, g * d)


def _pad_rows(x, n, front=0):
    back = n - x.shape[1] - front
    return jnp.pad(x, ((0, 0), (front, back)) + ((0, 0),) * (x.ndim - 2))


def _split(h, sizes):
    out, o = [], 0
    for s in sizes:
        out.append(h[:, o:o + s])
        o += s
    return out


def _split3(x):
    hi = x.astype(BF16)
    lo = (x - hi.astype(F32)).astype(BF16)
    return hi, lo


def _dsa_call(dq, dqi, dw, dk_all, dv_all, dki_all, tiles, i_off, n_sel):
    b, tq_all = dq.shape[:2]
    nq, nkb = tq_all // DSA_TQ, dk_all.shape[1] // DSA_TK

    def stack_t(x):
        d = x.shape[-1]
        return jnp.transpose(x.reshape(b, nq, DSA_TQ, D_HEADS, d), (0, 1, 4, 3, 2)).reshape(b, nq, d, D_HEADS * DSA_TQ)

    qh, ql = _split3(dqi)
    kh, kl = _split3(dki_all)
    qit = stack_t(jnp.concatenate([qh, ql, qh], axis=-1))
    ki = jnp.concatenate([kh, kh, kl], axis=-1).reshape(b, nkb, DSA_TK, 3 * IDX_DIM)
    w = stack_t(dw[..., None])
    qt = stack_t((dq * (D_DIM ** -0.5)).astype(BF16))
    k = dk_all.astype(BF16).reshape(b, nkb, DSA_TK, D_DIM)
    vt = jnp.transpose(dv_all.astype(BF16).reshape(b, nkb, DSA_TK, D_DIM), (0, 1, 3, 2))
    o = dsa(qit, w, ki, qt, k, vt, tiles, i_off=i_off, n_sel=n_sel)
    o = jnp.transpose(o.reshape(b, nq, D_DIM, D_HEADS, DSA_TQ), (0, 1, 4, 3, 2))
    return o.reshape(b, tq_all, D_HEADS * D_DIM)


P_BLK = 512
S_BLK = 128


def kernel(x_prompt, x_sample, cache_a_k, cache_a_v, cache_b_ckv, cache_b_krope, cache_c_k, cache_c_v, cache_d_k, cache_d_v, cache_d_kidx, t5_table, norm_attn, norm_ffn, final_norm, even_w_in, even_w_out, a_relbias, b_q_norm, b_kv_norm, b_w_uq, b_w_ukv, odd_w_in, odd_w_out, c_lambda_q1, c_lambda_k1, c_lambda_q2, c_lambda_k2, c_subln, ffn_w_gate, ffn_w_up, ffn_w_down):
    pb, pt, d = x_prompt.shape
    sb, st, _ = x_sample.shape
    assert pb == 1 and st == CHUNK
    past = cache_b_ckv.shape[2]
    n_a = cache_a_k.shape[2]
    depth = norm_attn.shape[0]
    n_p, n_s = pb * pt, sb * st
    s_len = past + st
    s_pad = -(-s_len // DSA_TK) * DSA_TK
    s_i = past // S_BLK
    s_i_dsa = past // DSA_TQ
    assert past % S_BLK == 0 and s_pad % S_BLK == 0 and past >= n_a
    p_blk = min(P_BLK, pt)

    x = jnp.concatenate([x_prompt.reshape(n_p, d), x_sample.reshape(n_s, d)], axis=0)
    pos = jnp.concatenate([jnp.arange(pt), jnp.tile(past + jnp.arange(st), sb)])
    cos, sin = _rope_tables(pos)

    def rows_p(a):
        return a[:n_p].reshape(pb, pt, *a.shape[1:])

    def rows_s(a):
        return a[n_p:].reshape(sb, st, *a.shape[1:])

    def pad_cols(w):
        n = -(-w.shape[1] // LANES) * LANES
        return jnp.pad(w, ((0, 0), (0, n - w.shape[1]))).astype(BF16)

    def sample_q(a):
        return _pad_rows(a, S_BLK)

    def sample_keys(cache, new, front=0):
        return _pad_rows(jnp.concatenate([cache, new], axis=1), s_pad, front)

    def attend(q, k, v, tiles, blk, n_tiles, has_far, i_off=0):
        o = flash(_col_blocks(q, blk), _row_blocks(k, blk), _col_blocks(v, blk), tiles,
                  n_tiles=n_tiles, has_far=has_far, i_off=i_off)
        return _from_col_blocks(o)

    t5_c, t5_d = t5_table[:, :C_HEADS], t5_table[:, C_HEADS:]
    ctiles_p = causal_tiles(p_blk)
    ctiles_s = causal_tiles(S_BLK)
    c_tiles_p = t5_tiles(t5_c, p_blk, p_blk, 2)
    c_tiles_s = t5_tiles(t5_c, S_BLK, S_BLK, 2)
    d_tiles = t5_tiles(t5_d, DSA_TQ, DSA_TK, 3)
    d_tiles = jnp.transpose(d_tiles, (1, 2, 0, 3)).reshape(4, DSA_TK, D_HEADS * DSA_TQ)
    n_sel_p = min(TOPK_MAX, pt // 4)
    n_sel_s = min(TOPK_MAX, s_len // 4)

    outs_even = {k: [] for k in ("a_k_p", "a_k_s", "a_v_p", "a_v_s", "ckv_p", "ckv_s", "kr_p", "kr_s")}
    outs_odd = {k: [] for k in ("c_k_p", "c_k_s", "c_v_p", "c_v_s", "d_k_p", "d_k_s", "d_v_p", "d_v_s", "d_ki_p", "d_ki_s")}

    for l in range(depth):
        i = l // 2
        if l % 2 == 0:
            h = dense(x, pad_cols(even_w_in[i]), g=norm_attn[l])
            aq, ak, av, cq, ckv_raw, kr_raw = _split(h, EVEN_SPLITS)
            qb = dense(cq, b_w_uq[i].astype(BF16), g=b_q_norm[i]).reshape(-1, B_HEADS, B_NOPE + B_ROPE)
            kv_new, ckv = dense(ckv_raw, b_w_ukv[i].astype(BF16), g=b_kv_norm[i], emit_h=True)
            kr = _rope(kr_raw, cos, sin)
            q_mla = jnp.concatenate([qb[..., :B_NOPE], _rope(qb[..., B_NOPE:], cos, sin)], axis=-1)
            q_mla = q_mla * ((B_NOPE + B_ROPE) ** -0.5)
            kv_new = kv_new.reshape(-1, B_HEADS, B_NOPE + B_V)
            aq = (aq * (A_DIM ** -0.5)).reshape(-1, A_HEADS, A_DIM)
            ak = ak.reshape(-1, A_HEADS, A_DIM)
            av = av.reshape(-1, A_HEADS, A_DIM)

            def mla_keys(kn, krope):
                return jnp.concatenate([kn, jnp.broadcast_to(krope[:, :, None, :], kn.shape[:3] + (B_ROPE,))], axis=-1)

            a_tiles_p, nt_p = band_tiles(a_relbias[i], p_blk)
            a_out_p = attend(rows_p(aq), rows_p(ak), rows_p(av), a_tiles_p, p_blk, nt_p, False)
            kv_p = rows_p(kv_new)
            b_out_p = attend(rows_p(q_mla), mla_keys(kv_p[..., :B_NOPE], rows_p(kr)), kv_p[..., B_NOPE:],
                             ctiles_p, p_blk, 1, True)
            a_tiles_s, nt_s = band_tiles(a_relbias[i], S_BLK)
            ak_all = jnp.concatenate([cache_a_k[i], rows_s(ak)], axis=1)
            av_all = jnp.concatenate([cache_a_v[i], rows_s(av)], axis=1)
            a_out_s = attend(sample_q(rows_s(aq)), _pad_rows(ak_all, s_pad, past - n_a),
                             _pad_rows(av_all, s_pad, past - n_a), a_tiles_s, S_BLK, nt_s, False, s_i)[:, :st]
            kv_c = dense(cache_b_ckv[i].reshape(sb * past, KV_LORA), b_w_ukv[i].astype(BF16))
            kv_c = kv_c.reshape(sb, past, B_HEADS, B_NOPE + B_V)
            kv_s = rows_s(kv_new)
            kn_all = sample_keys(kv_c[..., :B_NOPE], kv_s[..., :B_NOPE])
            v_all = sample_keys(kv_c[..., B_NOPE:], kv_s[..., B_NOPE:])
            kr_all = sample_keys(cache_b_krope[i], rows_s(kr))
            b_out_s = attend(sample_q(rows_s(q_mla)), mla_keys(kn_all, kr_all), v_all,
                             ctiles_s, S_BLK, 1, True, s_i)[:, :st]

            mix = jnp.concatenate([
                jnp.concatenate([a_out_p.reshape(n_p, -1), b_out_p.reshape(n_p, -1)], axis=-1),
                jnp.concatenate([a_out_s.reshape(n_s, -1), b_out_s.reshape(n_s, -1)], axis=-1),
            ], axis=0)
            x = dense(mix.astype(BF16), even_w_out[i].astype(BF16), res=x)

            outs_even["a_k_p"].append(rows_p(ak)[:, pt - n_a:])
            outs_even["a_v_p"].append(rows_p(av)[:, pt - n_a:])
            outs_even["a_k_s"].append(ak_all[:, -n_a:])
            outs_even["a_v_s"].append(av_all[:, -n_a:])
            outs_even["ckv_p"].append(rows_p(ckv))
            outs_even["ckv_s"].append(rows_s(ckv))
            outs_even["kr_p"].append(rows_p(kr))
            outs_even["kr_s"].append(rows_s(kr))
        else:
            lam_init = 0.8 - 0.6 * math.exp(-0.3 * l)
            lam = (jnp.exp(jnp.sum(c_lambda_q1[i].astype(F32) * c_lambda_k1[i].astype(F32)))
                   - jnp.exp(jnp.sum(c_lambda_q2[i].astype(F32) * c_lambda_k2[i].astype(F32))) + lam_init)
            h = dense(x, pad_cols(odd_w_in[i]), g=norm_attn[l])
            cq, ck, cv, dq, dk, dv, dqi, dki, dw = _split(h, ODD_SPLITS)
            cq = (cq * (C_DIM ** -0.5)).reshape(-1, 2 * C_HEADS, C_DIM)
            ck3 = ck.reshape(-1, 2 * C_HEADS, C_DIM)
            cv3 = cv.reshape(-1, C_HEADS, 2 * C_DIM)
            dq = dq.reshape(-1, D_HEADS, D_DIM)
            dqi = dqi.reshape(-1, IDX_HEADS, IDX_DIM)
            dw = dw * ((IDX_HEADS ** -0.5) * (IDX_DIM ** -0.5))

            def diff_combine(o):
                b_, t_, _ = o.shape
                o = o.reshape(b_, t_, C_HEADS, 2, 2 * C_DIM)
                c = o[:, :, :, 0] - lam * o[:, :, :, 1]
                return (_rms(c, c_subln[i].astype(F32)) * (1.0 - lam_init)).reshape(b_, t_, -1)

            c_out_p = attend(rows_p(cq), rows_p(ck3), rows_p(cv3), c_tiles_p, p_blk, 2, True)
            d_out_p = _dsa_call(rows_p(dq), rows_p(dqi), rows_p(dw), rows_p(dk), rows_p(dv), rows_p(dki),
                                d_tiles, 0, n_sel_p)
            ck_all = sample_keys(cache_c_k[i].reshape(sb, past, 2 * C_HEADS, C_DIM), rows_s(ck3))
            cv_all = sample_keys(cache_c_v[i], rows_s(cv3))
            c_out_s = attend(sample_q(rows_s(cq)), ck_all, cv_all, c_tiles_s, S_BLK, 2, True, s_i)[:, :st]
            d_out_s = _dsa_call(_pad_rows(rows_s(dq), DSA_TQ), _pad_rows(rows_s(dqi), DSA_TQ),
                                _pad_rows(rows_s(dw), DSA_TQ),
                                sample_keys(cache_d_k[i], rows_s(dk)), sample_keys(cache_d_v[i], rows_s(dv)),
                                sample_keys(cache_d_kidx[i], rows_s(dki)), d_tiles, s_i_dsa, n_sel_s)[:, :st]

            mix = jnp.concatenate([
                jnp.concatenate([diff_combine(c_out_p).reshape(n_p, -1), d_out_p.reshape(n_p, -1)], axis=-1),
                jnp.concatenate([diff_combine(c_out_s).reshape(n_s, -1), d_out_s.reshape(n_s, -1)], axis=-1),
            ], axis=0)
            x = dense(mix.astype(BF16), odd_w_out[i].astype(BF16), res=x)

            for name, arr, shp in (("c_k", ck, (C_HEADS, 2 * C_DIM)), ("c_v", cv, (C_HEADS, 2 * C_DIM)),
                                   ("d_k", dk, (D_DIM,)), ("d_v", dv, (D_DIM,)), ("d_ki", dki, (IDX_DIM,))):
                outs_odd[name + "_p"].append(rows_p(arr).reshape(pb, pt, *shp))
                outs_odd[name + "_s"].append(rows_s(arr).reshape(sb, st, *shp))

        x = ffn(x, norm_ffn[l], ffn_w_gate[l].astype(BF16), ffn_w_up[l].astype(BF16), ffn_w_down[l].astype(BF16),
                final_g=final_norm if l == depth - 1 else None)

    y_prompt = x[:n_p].reshape(pb, pt, d)
    y_sample = x[n_p:].reshape(sb, st, d)
    se = {k: jnp.stack(v, axis=0) for k, v in outs_even.items()}
    so = {k: jnp.stack(v, axis=0) for k, v in outs_odd.items()}
    return (y_prompt, y_sample, se["a_k_p"], se["a_k_s"], se["a_v_p"], se["a_v_s"],
            se["ckv_p"], se["ckv_s"], se["kr_p"], se["kr_s"],
            so["c_k_p"], so["c_k_s"], so["c_v_p"], so["c_v_s"],
            so["d_k_p"], so["d_k_s"], so["d_v_p"], so["d_v_s"], so["d_ki_p"], so["d_ki_s"])
```

```python
import functools
import math

import numpy as np
import jax
import jax.numpy as jnp
from jax import lax
from jax.experimental import pallas as pl
from jax.experimental.pallas import tpu as pltpu

F32, BF16, I32 = jnp.float32, jnp.bfloat16, jnp.int32

D_MODEL = 1024
CHUNK = 64
EPS = 1e-6
A_HEADS, A_DIM, A_LEFT_CHUNKS, A_MAX_REL = 8, 64, 8, 128
B_HEADS, B_NOPE, B_ROPE, B_V = 8, 64, 32, 64
Q_LORA, KV_LORA = 256, 128
ROPE_THETA = 10000.0
C_HEADS, C_DIM = 4, 64
D_HEADS, D_DIM = 8, 64
IDX_HEADS, IDX_DIM = 8, 64
TOPK_MAX = 256
T5_BUCKETS, T5_MAX_DIST = 32, 128
EVEN_SPLITS = (512, 512, 512, Q_LORA, KV_LORA, B_ROPE)
ODD_SPLITS = (512, 512, 512, 512, D_DIM, D_DIM, 512, IDX_DIM, IDX_HEADS)

LANES = 128
SUBLANES = 8
VMEM_LIMIT = 56 * 1024 * 1024
MASKED = -1e30
UNSELECTED = -2e30
M_INIT = -1e30
SUM_MAX, SUM_MIN = 1e18, 1e-18
LOG2E = math.log2(math.e)
INT_MIN = -(2 ** 31)
INT_MAX = 2 ** 31 - 1

DSA_TQ, DSA_TK = 128, 256
COUNT_ROWS = 8 * SUBLANES
FAR_UNROLL = 4
DSA_FAR_UNROLL = 2


def _cparams(n_axes):
    return pltpu.CompilerParams(dimension_semantics=("arbitrary",) * n_axes, vmem_limit_bytes=VMEM_LIMIT)


def _rms(x, g):
    return x * lax.rsqrt(jnp.mean(x * x, axis=-1, keepdims=True) + EPS) * g


def _dense_kernel(*refs, norm, res, emit_h):
    it = iter(refs)
    x_ref = next(it)
    g_ref = next(it) if norm else None
    w_ref = next(it)
    r_ref = next(it) if res else None
    o_ref = next(it)
    h_ref = next(it) if emit_h else None
    x = x_ref[...]
    if norm:
        x = _rms(x.astype(F32), g_ref[...])
        if emit_h:
            h_ref[...] = x
    acc = jnp.dot(x.astype(BF16), w_ref[...], preferred_element_type=F32)
    if res:
        acc = acc + r_ref[...]
    o_ref[...] = acc


def dense(x, w, g=None, res=None, emit_h=False, tm=512):
    m, k = x.shape
    n = w.shape[1]
    assert m % tm == 0 and n % LANES == 0
    norm = g is not None
    args, specs = [x], [pl.BlockSpec((tm, k), lambda i: (i, 0))]
    if norm:
        args.append(g.reshape(1, k).astype(F32))
        specs.append(pl.BlockSpec((1, k), lambda i: (0, 0)))
    args.append(w)
    specs.append(pl.BlockSpec((k, n), lambda i: (0, 0)))
    if res is not None:
        args.append(res)
        specs.append(pl.BlockSpec((tm, n), lambda i: (i, 0)))
    out_shape = [jax.ShapeDtypeStruct((m, n), F32)]
    out_specs = [pl.BlockSpec((tm, n), lambda i: (i, 0))]
    if emit_h:
        out_shape.append(jax.ShapeDtypeStruct((m, k), F32))
        out_specs.append(pl.BlockSpec((tm, k), lambda i: (i, 0)))
    outs = pl.pallas_call(
        functools.partial(_dense_kernel, norm=norm, res=res is not None, emit_h=emit_h),
        grid=(m // tm,), in_specs=specs, out_specs=out_specs, out_shape=out_shape,
        compiler_params=_cparams(1), name="dense")(*args)
    return outs if emit_h else outs[0]


def _ffn_kernel(x_ref, g_ref, wg_ref, wu_ref, wd_ref, fg_ref, o_ref, h_sc, acc_sc, *, final):
    f = pl.program_id(1)

    @pl.when(f == 0)
    def _():
        x = x_ref[...]
        h_sc[...] = _rms(x, g_ref[...]).astype(BF16)
        acc_sc[...] = x

    h = h_sc[...]
    gate = jnp.dot(h, wg_ref[...], preferred_element_type=F32)
    up = jnp.dot(h, wu_ref[...], preferred_element_type=F32)
    a = (gate * jax.nn.sigmoid(gate) * up).astype(BF16)
    acc_sc[...] += jnp.dot(a, wd_ref[...], preferred_element_type=F32)

    @pl.when(f == pl.num_programs(1) - 1)
    def _():
        y = acc_sc[...]
        if final:
            y = _rms(y, fg_ref[...])
        o_ref[...] = y


def ffn(x, g, wg, wu, wd, final_g=None, tm=512, nf=2):
    m, d = x.shape
    hid = wg.shape[1]
    tf = hid // nf
    assert m % tm == 0 and hid % nf == 0 and tf % LANES == 0
    final = final_g is not None
    fg = (final_g if final else g).reshape(1, d).astype(F32)
    return pl.pallas_call(
        functools.partial(_ffn_kernel, final=final),
        grid=(m // tm, nf),
        in_specs=[pl.BlockSpec((tm, d), lambda i, f: (i, 0)),
                  pl.BlockSpec((1, d), lambda i, f: (0, 0)),
                  pl.BlockSpec((d, tf), lambda i, f: (0, f)),
                  pl.BlockSpec((d, tf), lambda i, f: (0, f)),
                  pl.BlockSpec((tf, d), lambda i, f: (f, 0)),
                  pl.BlockSpec((1, d), lambda i, f: (0, 0))],
        out_specs=pl.BlockSpec((tm, d), lambda i, f: (i, 0)),
        out_shape=jax.ShapeDtypeStruct((m, d), F32),
        scratch_shapes=[pltpu.VMEM((tm, d), BF16), pltpu.VMEM((tm, d), F32)],
        compiler_params=_cparams(2), name="ffn")(x, g.reshape(1, d).astype(F32), wg, wu, wd, fg)


def _flash_kernel(qt_ref, k_ref, vt_ref, tile_ref, o_ref, m_sc, l_sc, acc_sc, *, n_tiles, has_far, i_off):
    i = pl.program_id(2) + i_off
    qt = qt_ref[0, 0, 0]
    j_lo = 0 if has_far else jnp.maximum(i - (n_tiles - 1), 0)

    def raw(j):
        return jnp.dot(k_ref[0, 0, j], qt, preferred_element_type=F32)

    def scores(j):
        return raw(j) + tile_ref[0, jnp.minimum(i - j, n_tiles)]

    def reset():
        l_sc[...] = jnp.zeros(l_sc.shape, F32)
        acc_sc[...] = jnp.zeros(acc_sc.shape, F32)

    def accumulate(j, p):
        l_sc[...] += jnp.sum(p, axis=0, keepdims=True)
        acc_sc[...] += jnp.dot(vt_ref[0, 0, j], p.astype(BF16), preferred_element_type=F32)

    reset()
    m = jnp.max(scores(j_lo), axis=0, keepdims=True)
    if has_far:
        m_far = m - tile_ref[0, n_tiles, 0:1, :]

        n_far = jnp.maximum(i - (n_tiles - 1), 0)

        def far_group(jj, carry):
            ps = [jnp.exp2(raw(FAR_UNROLL * jj + u) - m_far) for u in range(FAR_UNROLL)]
            for u in range(FAR_UNROLL):
                accumulate(FAR_UNROLL * jj + u, ps[u])
            return carry

        def far_single(j, carry):
            accumulate(j, jnp.exp2(raw(j) - m_far))
            return carry

        n_grouped = (n_far // FAR_UNROLL) * FAR_UNROLL
        lax.fori_loop(0, n_far // FAR_UNROLL, far_group, 0)
        lax.fori_loop(n_grouped, n_far, far_single, 0)
    for t in reversed(range(n_tiles)):
        @pl.when(i - t >= 0)
        def _(t=t):
            accumulate(i - t, jnp.exp2(raw(i - t) + tile_ref[0, t] - m))

    l = l_sc[...]
    in_range = (jnp.max(l) < SUM_MAX) & (jnp.min(l) > SUM_MIN)

    @pl.when(jnp.logical_not(in_range))
    def _():
        reset()
        m_sc[...] = jnp.full(m_sc.shape, M_INIT, F32)

        def body(j, carry):
            s = scores(j)
            m_prev = m_sc[...]
            m_new = jnp.maximum(m_prev, jnp.max(s, axis=0, keepdims=True))
            alpha = jnp.exp2(m_prev - m_new)
            p = jnp.exp2(s - m_new)
            l_sc[...] = alpha * l_sc[...] + jnp.sum(p, axis=0, keepdims=True)
            acc_sc[...] = alpha * acc_sc[...] + jnp.dot(vt_ref[0, 0, j], p.astype(BF16), preferred_element_type=F32)
            m_sc[...] = m_new
            return carry

        lax.fori_loop(j_lo, i + 1, body, 0)

    o_ref[0, 0, 0] = acc_sc[...] / l_sc[...]


def flash(qt, k, vt, tiles, *, n_tiles, has_far, i_off=0):
    b, g, nq, dq, blk = qt.shape
    gk, nb = k.shape[1], k.shape[2]
    gv, dv = vt.shape[1], vt.shape[3]
    gb = tiles.shape[0]
    assert tiles.shape[1] == n_tiles + 1
    kdiv, vdiv, bdiv = g // gk, g // gv, g // gb
    return pl.pallas_call(
        functools.partial(_flash_kernel, n_tiles=n_tiles, has_far=has_far, i_off=i_off),
        grid=(b, g, nq),
        in_specs=[pl.BlockSpec((1, 1, 1, dq, blk), lambda bi, gi, i: (bi, gi, i, 0, 0)),
                  pl.BlockSpec((1, 1, nb, blk, dq), lambda bi, gi, i: (bi, gi // kdiv, 0, 0, 0)),
                  pl.BlockSpec((1, 1, nb, dv, blk), lambda bi, gi, i: (bi, gi // vdiv, 0, 0, 0)),
                  pl.BlockSpec((1, n_tiles + 1, blk, blk), lambda bi, gi, i: (gi // bdiv, 0, 0, 0))],
        out_specs=pl.BlockSpec((1, 1, 1, dv, blk), lambda bi, gi, i: (bi, gi, i, 0, 0)),
        out_shape=jax.ShapeDtypeStruct((b, g, nq, dv, blk), F32),
        scratch_shapes=[pltpu.VMEM((1, blk), F32), pltpu.VMEM((1, blk), F32), pltpu.VMEM((dv, blk), F32)],
        compiler_params=_cparams(3), name="flash")(qt, k, vt, tiles)


def _dsa_kernel(qit_ref, w_ref, ki_ref, qt_ref, k_ref, vt_ref, tile_ref, o_ref,
                key_sc, m_sc, l_sc, acc_sc, *, i_off, n_sel, idx_bits):
    tq, tk, nh = DSA_TQ, DSA_TK, D_HEADS
    i = pl.program_id(1) + i_off
    nkb = (i * tq) // tk + 1
    krow = lax.broadcasted_iota(I32, (tk, tq), 0)
    qcol = lax.broadcasted_iota(I32, (tk, tq), 1)
    q_chunk = jnp.right_shift(i * tq + qcol, 6)

    def head(x, h):
        return x[:, h * tq:(h + 1) * tq]

    qit = qit_ref[0, 0]
    w = w_ref[0, 0]

    def score_block(j, carry):
        lg = jnp.maximum(jnp.dot(ki_ref[0, j], qit, preferred_element_type=F32), 0.0) * w
        sc = head(lg, 0)
        for h in range(1, IDX_HEADS):
            sc = sc + head(lg, h)
        sc = jnp.where(sc == 0.0, 0.0, sc)
        bits = pltpu.bitcast(sc, I32)
        key = bits ^ (jnp.right_shift(bits, 31) & INT_MAX)
        adm = jnp.right_shift(j * tk + krow, 6) <= q_chunk
        key_sc[j] = jnp.where(adm, key, INT_MIN)
        return carry

    lax.fori_loop(0, nkb, score_block, 0)

    def count(pred_fn):
        def body(j, acc):
            ind = jnp.where(pred_fn(key_sc[j], j * tk + krow), 1.0, 0.0)
            return acc + jnp.sum(ind.reshape(tk // COUNT_ROWS, COUNT_ROWS, tq), axis=0)
        acc = lax.fori_loop(0, nkb, body, jnp.zeros((COUNT_ROWS, tq), F32))
        return jnp.sum(acc, axis=0, keepdims=True)

    def value_bit(b, thr):
        cand = thr + jnp.left_shift(jnp.int32(1), 31 - b)
        cnt = count(lambda k, idx: k >= cand)
        return jnp.where(cnt >= float(n_sel), cand, thr)

    thr = lax.fori_loop(0, 32, value_bit, jnp.full((1, tq), INT_MIN, I32))
    n_gt = count(lambda k, idx: k > thr)
    n_eq = count(lambda k, idx: k == thr)
    need = float(n_sel) - n_gt
    real = thr > INT_MIN

    def tie_search():
        def index_bit(b, y):
            cand = y + jnp.left_shift(jnp.int32(1), idx_bits - 1 - b)
            cnt = count(lambda k, idx: (k == thr) & (idx < cand))
            return jnp.where(cnt < need, cand, y)
        return lax.fori_loop(0, idx_bits, index_bit, jnp.zeros((1, tq), I32))

    has_ties = jnp.max(jnp.where((n_eq > need) & real, 1.0, 0.0)) > 0.0
    last = lax.cond(has_ties, tie_search, lambda: jnp.full((1, tq), INT_MAX, I32))
    last = jnp.where(real, last, -1)

    qt = qt_ref[0, 0]
    n_near = 3

    def raw(j):
        return jnp.dot(k_ref[0, j], qt, preferred_element_type=F32)

    def tile_of(j):
        return jnp.minimum((i * tq - j * tk) // tq, n_near)

    def select(j, s):
        key = key_sc[j]
        sel = (key > thr) | ((key == thr) & (j * tk + krow <= last))
        selb = jnp.where(sel, 0.0, UNSELECTED)
        return jnp.concatenate([head(s, h) + selb for h in range(nh)], axis=1)

    def reset():
        l_sc[...] = jnp.zeros(l_sc.shape, F32)
        acc_sc[...] = jnp.zeros(acc_sc.shape, F32)

    def accumulate(j, p):
        l_sc[...] += jnp.sum(p, axis=0, keepdims=True)
        acc_sc[...] += jnp.dot(vt_ref[0, j], p.astype(BF16), preferred_element_type=F32)

    reset()
    m = jnp.max(raw(0) + tile_ref[tile_of(0)], axis=0, keepdims=True)
    m_far = m - tile_ref[n_near, 0:1, :]
    n_far = jnp.maximum(i - 1, 0) // 2

    def far_group(jj, carry):
        ps = [jnp.exp2(select(DSA_FAR_UNROLL * jj + u, raw(DSA_FAR_UNROLL * jj + u)) - m_far)
              for u in range(DSA_FAR_UNROLL)]
        for u in range(DSA_FAR_UNROLL):
            accumulate(DSA_FAR_UNROLL * jj + u, ps[u])
        return carry

    def far_single(j, carry):
        accumulate(j, jnp.exp2(select(j, raw(j)) - m_far))
        return carry

    def near(j, carry):
        accumulate(j, jnp.exp2(select(j, raw(j) + tile_ref[tile_of(j)]) - m))
        return carry

    n_grouped = (n_far // DSA_FAR_UNROLL) * DSA_FAR_UNROLL
    lax.fori_loop(0, n_far // DSA_FAR_UNROLL, far_group, 0)
    lax.fori_loop(n_grouped, n_far, far_single, 0)
    lax.fori_loop(n_far, nkb, near, 0)

    l = l_sc[...]
    in_range = (jnp.max(l) < SUM_MAX) & (jnp.min(l) > SUM_MIN)

    @pl.when(jnp.logical_not(in_range))
    def _():
        reset()
        m_sc[...] = jnp.full(m_sc.shape, M_INIT, F32)

        def attend(j, carry):
            s = select(j, raw(j) + tile_ref[tile_of(j)])
            m_prev = m_sc[...]
            m_new = jnp.maximum(m_prev, jnp.max(s, axis=0, keepdims=True))
            alpha = jnp.exp2(m_prev - m_new)
            p = jnp.exp2(s - m_new)
            l_sc[...] = alpha * l_sc[...] + jnp.sum(p, axis=0, keepdims=True)
            acc_sc[...] = alpha * acc_sc[...] + jnp.dot(vt_ref[0, j], p.astype(BF16), preferred_element_type=F32)
            m_sc[...] = m_new
            return carry

        lax.fori_loop(0, nkb, attend, 0)

    o_ref[0, 0] = acc_sc[...] / l_sc[...]


def dsa(qit, w, ki, qt, k, vt, tiles, *, i_off, n_sel):
    b, nq, kdim, cols = qit.shape
    nkb = ki.shape[1]
    dk, dv = k.shape[-1], vt.shape[2]
    idx_bits = max(1, (nkb * DSA_TK - 1).bit_length())
    return pl.pallas_call(
        functools.partial(_dsa_kernel, i_off=i_off, n_sel=n_sel, idx_bits=idx_bits),
        grid=(b, nq),
        in_specs=[pl.BlockSpec((1, 1, kdim, cols), lambda bi, i: (bi, i, 0, 0)),
                  pl.BlockSpec((1, 1, 1, cols), lambda bi, i: (bi, i, 0, 0)),
                  pl.BlockSpec((1, nkb, DSA_TK, kdim), lambda bi, i: (bi, 0, 0, 0)),
                  pl.BlockSpec((1, 1, dk, cols), lambda bi, i: (bi, i, 0, 0)),
                  pl.BlockSpec((1, nkb, DSA_TK, dk), lambda bi, i: (bi, 0, 0, 0)),
                  pl.BlockSpec((1, nkb, dv, DSA_TK), lambda bi, i: (bi, 0, 0, 0)),
                  pl.BlockSpec((4, DSA_TK, cols), lambda bi, i: (0, 0, 0))],
        out_specs=pl.BlockSpec((1, 1, dv, cols), lambda bi, i: (bi, i, 0, 0)),
        out_shape=jax.ShapeDtypeStruct((b, nq, dv, cols), F32),
        scratch_shapes=[pltpu.VMEM((nkb, DSA_TK, DSA_TQ), I32), pltpu.VMEM((1, cols), F32),
                        pltpu.VMEM((1, cols), F32), pltpu.VMEM((dv, cols), F32)],
        compiler_params=_cparams(2), name="dsa")(qit, w, ki, qt, k, vt, tiles)


def _rope_tables(pos):
    half = B_ROPE // 2
    freqs = jnp.power(jnp.float32(ROPE_THETA), -jnp.arange(half, dtype=F32) / half)
    ang = pos.astype(F32)[:, None] * freqs[None, :]
    return jnp.cos(ang), jnp.sin(ang)


def _rope(x, cos, sin):
    half = x.shape[-1] // 2
    shape = (x.shape[0],) + (1,) * (x.ndim - 2) + (half,)
    c, s = cos.reshape(shape), sin.reshape(shape)
    x1, x2 = x[..., :half], x[..., half:]
    return jnp.concatenate([x1 * c - x2 * s, x1 * s + x2 * c], axis=-1)


def _t5_bucket(rel):
    nb = T5_BUCKETS // 2
    max_exact = nb // 2
    ret = jnp.where(rel > 0, nb, 0)
    n = jnp.abs(rel)
    nf = jnp.maximum(n, 1).astype(F32)
    large = max_exact + (jnp.log(nf / max_exact) / math.log(T5_MAX_DIST / max_exact)
                         * (nb - max_exact)).astype(I32)
    large = jnp.minimum(large, nb - 1)
    return ret + jnp.where(n < max_exact, n, large)


def _chunk_diff(tq, tk, n_tiles):
    r = np.arange(tq)[None, None, :]
    c = np.arange(tk)[None, :, None]
    t = np.arange(n_tiles)[:, None, None]
    return c // CHUNK - r // CHUNK - t * (tq // CHUNK)


def _strip_distances(tq, tk, n_tiles):
    y = np.arange(tq + tk)[None, :]
    t = np.arange(n_tiles)[:, None]
    return np.where(y < tq, -y, tq + tk - y) - t * tq


def _toeplitz(strip, tq, tk):
    n = tq + tk
    lead = strip.shape[:-1]
    rows = jnp.broadcast_to(strip[..., None, :], lead + (tk, n)).reshape(lead + (tk * n,))
    return rows[..., :tk * (n - 1)].reshape(lead + (tk, n - 1))[..., :tq]


def band_tiles(relbias, blk):
    n_tiles = (A_LEFT_CHUNKS * CHUNK) // blk + 1
    dchunk = _chunk_diff(blk, blk, n_tiles)
    adm = (dchunk <= 0) & (-dchunk <= A_LEFT_CHUNKS)
    idx = np.clip(-_strip_distances(blk, blk, n_tiles), -A_MAX_REL, A_MAX_REL) + A_MAX_REL
    bias = _toeplitz(relbias.astype(F32)[:, idx] * LOG2E, blk, blk)
    tiles = jnp.where(adm[None], bias, MASKED)
    far = jnp.full((relbias.shape[0], 1, blk, blk), MASKED, F32)
    return jnp.concatenate([tiles, far], axis=1), n_tiles


def causal_tiles(blk):
    t0 = np.where(_chunk_diff(blk, blk, 1) <= 0, 0.0, MASKED).astype(np.float32)
    return jnp.asarray(np.concatenate([t0, np.zeros_like(t0)], axis=0)[None])


def t5_tiles(tab, tq, tk, n_tiles):
    assert n_tiles * tq - tk + 1 >= T5_MAX_DIST
    tab = tab.astype(F32) * LOG2E
    strip = jnp.moveaxis(tab[_t5_bucket(jnp.asarray(_strip_distances(tq, tk, n_tiles), I32))], -1, 0)
    tiles = jnp.where((_chunk_diff(tq, tk, n_tiles) <= 0)[None], _toeplitz(strip, tq, tk), MASKED)
    far = tab[_t5_bucket(jnp.asarray([-(n_tiles * tq + tk)], I32))[0]]
    far = jnp.broadcast_to(far[:, None, None, None], (tab.shape[1], 1, tk, tq))
    return jnp.concatenate([tiles, far], axis=1)


def _row_blocks(x, blk):
    b, t, g, d = x.shape
    return jnp.transpose(x.reshape(b, t // blk, blk, g, d), (0, 3, 1, 2, 4)).astype(BF16)


def _col_blocks(x, blk):
    b, t, g, d = x.shape
    return jnp.transpose(x.reshape(b, t // blk, blk, g, d), (0, 3, 1, 4, 2)).astype(BF16)


def _from_col_blocks(o):
    b, g, nq, d, blk = o.shape
    return jnp.transpose(o, (0, 2, 4, 1, 3)).reshape(b, nq * blk, g * d)


def _pad_rows(x, n, front=0):
    back = n - x.shape[1] - front
    return jnp.pad(x, ((0, 0), (front, back)) + ((0, 0),) * (x.ndim - 2))


def _split(h, sizes):
    out, o = [], 0
    for s in sizes:
        out.append(h[:, o:o + s])
        o += s
    return out


def _split3(x):
    hi = x.astype(BF16)
    lo = (x - hi.astype(F32)).astype(BF16)
    return hi, lo


def _dsa_call(dq, dqi, dw, dk_all, dv_all, dki_all, tiles, i_off, n_sel):
    b, tq_all = dq.shape[:2]
    nq, nkb = tq_all // DSA_TQ, dk_all.shape[1] // DSA_TK

    def stack_t(x):
        d = x.shape[-1]
        return jnp.transpose(x.reshape(b, nq, DSA_TQ, D_HEADS, d), (0, 1, 4, 3, 2)).reshape(b, nq, d, D_HEADS * DSA_TQ)

    qh, ql = _split3(dqi)
    kh, kl = _split3(dki_all)
    qit = stack_t(jnp.concatenate([qh, ql, qh], axis=-1))
    ki = jnp.concatenate([kh, kh, kl], axis=-1).reshape(b, nkb, DSA_TK, 3 * IDX_DIM)
    w = stack_t(dw[..., None])
    qt = stack_t((dq * (D_DIM ** -0.5 * LOG2E)).astype(BF16))
    k = dk_all.astype(BF16).reshape(b, nkb, DSA_TK, D_DIM)
    vt = jnp.transpose(dv_all.astype(BF16).reshape(b, nkb, DSA_TK, D_DIM), (0, 1, 3, 2))
    o = dsa(qit, w, ki, qt, k, vt, tiles, i_off=i_off, n_sel=n_sel)
    o = jnp.transpose(o.reshape(b, nq, D_DIM, D_HEADS, DSA_TQ), (0, 1, 4, 3, 2))
    return o.reshape(b, tq_all, D_HEADS * D_DIM)


P_BLK = 512
S_BLK = 128


def kernel(x_prompt, x_sample, cache_a_k, cache_a_v, cache_b_ckv, cache_b_krope, cache_c_k, cache_c_v, cache_d_k, cache_d_v, cache_d_kidx, t5_table, norm_attn, norm_ffn, final_norm, even_w_in, even_w_out, a_relbias, b_q_norm, b_kv_norm, b_w_uq, b_w_ukv, odd_w_in, odd_w_out, c_lambda_q1, c_lambda_k1, c_lambda_q2, c_lambda_k2, c_subln, ffn_w_gate, ffn_w_up, ffn_w_down):
    pb, pt, d = x_prompt.shape
    sb, st, _ = x_sample.shape
    assert pb == 1 and st == CHUNK
    past = cache_b_ckv.shape[2]
    n_a = cache_a_k.shape[2]
    depth = norm_attn.shape[0]
    n_p, n_s = pb * pt, sb * st
    s_len = past + st
    s_pad = -(-s_len // DSA_TK) * DSA_TK
    s_i = past // S_BLK
    s_i_dsa = past // DSA_TQ
    assert past % S_BLK == 0 and s_pad % S_BLK == 0 and past >= n_a
    p_blk = min(P_BLK, pt)

    x = jnp.concatenate([x_prompt.reshape(n_p, d), x_sample.reshape(n_s, d)], axis=0)
    pos = jnp.concatenate([jnp.arange(pt), jnp.tile(past + jnp.arange(st), sb)])
    cos, sin = _rope_tables(pos)

    def rows_p(a):
        return a[:n_p].reshape(pb, pt, *a.shape[1:])

    def rows_s(a):
        return a[n_p:].reshape(sb, st, *a.shape[1:])

    def pad_cols(w):
        n = -(-w.shape[1] // LANES) * LANES
        return jnp.pad(w, ((0, 0), (0, n - w.shape[1]))).astype(BF16)

    def sample_q(a):
        return _pad_rows(a, S_BLK)

    def sample_keys(cache, new, front=0):
        return _pad_rows(jnp.concatenate([cache, new], axis=1), s_pad, front)

    def attend(q, k, v, tiles, blk, n_tiles, has_far, i_off=0):
        o = flash(_col_blocks(q, blk), _row_blocks(k, blk), _col_blocks(v, blk), tiles,
                  n_tiles=n_tiles, has_far=has_far, i_off=i_off)
        return _from_col_blocks(o)

    t5_c, t5_d = t5_table[:, :C_HEADS], t5_table[:, C_HEADS:]
    ctiles_p = causal_tiles(p_blk)
    ctiles_s = causal_tiles(S_BLK)
    c_tiles_p = t5_tiles(t5_c, p_blk, p_blk, 2)
    c_tiles_s = t5_tiles(t5_c, S_BLK, S_BLK, 2)
    d_tiles = t5_tiles(t5_d, DSA_TQ, DSA_TK, 3)
    d_tiles = jnp.transpose(d_tiles, (1, 2, 0, 3)).reshape(4, DSA_TK, D_HEADS * DSA_TQ)
    n_sel_p = min(TOPK_MAX, pt // 4)
    n_sel_s = min(TOPK_MAX, s_len // 4)

    outs_even = {k: [] for k in ("a_k_p", "a_k_s", "a_v_p", "a_v_s", "ckv_p", "ckv_s", "kr_p", "kr_s")}
    outs_odd = {k: [] for k in ("c_k_p", "c_k_s", "c_v_p", "c_v_s", "d_k_p", "d_k_s", "d_v_p", "d_v_s", "d_ki_p", "d_ki_s")}

    for l in range(depth):
        i = l // 2
        if l % 2 == 0:
            h = dense(x, pad_cols(even_w_in[i]), g=norm_attn[l])
            aq, ak, av, cq, ckv_raw, kr_raw = _split(h, EVEN_SPLITS)
            qb = dense(cq, b_w_uq[i].astype(BF16), g=b_q_norm[i]).reshape(-1, B_HEADS, B_NOPE + B_ROPE)
            kv_new, ckv = dense(ckv_raw, b_w_ukv[i].astype(BF16), g=b_kv_norm[i], emit_h=True)
            kr = _rope(kr_raw, cos, sin)
            q_mla = jnp.concatenate([qb[..., :B_NOPE], _rope(qb[..., B_NOPE:], cos, sin)], axis=-1)
            q_mla = q_mla * ((B_NOPE + B_ROPE) ** -0.5 * LOG2E)
            kv_new = kv_new.reshape(-1, B_HEADS, B_NOPE + B_V)
            aq = (aq * (A_DIM ** -0.5 * LOG2E)).reshape(-1, A_HEADS, A_DIM)
            ak = ak.reshape(-1, A_HEADS, A_DIM)
            av = av.reshape(-1, A_HEADS, A_DIM)

            def mla_keys(kn, krope):
                return jnp.concatenate([kn, jnp.broadcast_to(krope[:, :, None, :], kn.shape[:3] + (B_ROPE,))], axis=-1)

            a_tiles_p, nt_p = band_tiles(a_relbias[i], p_blk)
            a_out_p = attend(rows_p(aq), rows_p(ak), rows_p(av), a_tiles_p, p_blk, nt_p, False)
            kv_p = rows_p(kv_new)
            b_out_p = attend(rows_p(q_mla), mla_keys(kv_p[..., :B_NOPE], rows_p(kr)), kv_p[..., B_NOPE:],
                             ctiles_p, p_blk, 1, True)
            a_tiles_s, nt_s = band_tiles(a_relbias[i], S_BLK)
            ak_all = jnp.concatenate([cache_a_k[i], rows_s(ak)], axis=1)
            av_all = jnp.concatenate([cache_a_v[i], rows_s(av)], axis=1)
            a_out_s = attend(sample_q(rows_s(aq)), _pad_rows(ak_all, s_pad, past - n_a),
                             _pad_rows(av_all, s_pad, past - n_a), a_tiles_s, S_BLK, nt_s, False, s_i)[:, :st]
            kv_c = dense(cache_b_ckv[i].reshape(sb * past, KV_LORA), b_w_ukv[i].astype(BF16))
            kv_c = kv_c.reshape(sb, past, B_HEADS, B_NOPE + B_V)
            kv_s = rows_s(kv_new)
            kn_all = sample_keys(kv_c[..., :B_NOPE], kv_s[..., :B_NOPE])
            v_all = sample_keys(kv_c[..., B_NOPE:], kv_s[..., B_NOPE:])
            kr_all = sample_keys(cache_b_krope[i], rows_s(kr))
            b_out_s = attend(sample_q(rows_s(q_mla)), mla_keys(kn_all, kr_all), v_all,
                             ctiles_s, S_BLK, 1, True, s_i)[:, :st]

            mix = jnp.concatenate([
                jnp.concatenate([a_out_p.reshape(n_p, -1), b_out_p.reshape(n_p, -1)], axis=-1),
                jnp.concatenate([a_out_s.reshape(n_s, -1), b_out_s.reshape(n_s, -1)], axis=-1),
            ], axis=0)
            x = dense(mix.astype(BF16), even_w_out[i].astype(BF16), res=x)

            outs_even["a_k_p"].append(rows_p(ak)[:, pt - n_a:])
            outs_even["a_v_p"].append(rows_p(av)[:, pt - n_a:])
            outs_even["a_k_s"].append(ak_all[:, -n_a:])
            outs_even["a_v_s"].append(av_all[:, -n_a:])
            outs_even["ckv_p"].append(rows_p(ckv))
            outs_even["ckv_s"].append(rows_s(ckv))
            outs_even["kr_p"].append(rows_p(kr))
            outs_even["kr_s"].append(rows_s(kr))
        else:
            lam_init = 0.8 - 0.6 * math.exp(-0.3 * l)
            lam = (jnp.exp(jnp.sum(c_lambda_q1[i].astype(F32) * c_lambda_k1[i].astype(F32)))
                   - jnp.exp(jnp.sum(c_lambda_q2[i].astype(F32) * c_lambda_k2[i].astype(F32))) + lam_init)
            h = dense(x, pad_cols(odd_w_in[i]), g=norm_attn[l])
            cq, ck, cv, dq, dk, dv, dqi, dki, dw = _split(h, ODD_SPLITS)
            cq = (cq * (C_DIM ** -0.5 * LOG2E)).reshape(-1, 2 * C_HEADS, C_DIM)
            ck3 = ck.reshape(-1, 2 * C_HEADS, C_DIM)
            cv3 = cv.reshape(-1, C_HEADS, 2 * C_DIM)
            dq = dq.reshape(-1, D_HEADS, D_DIM)
            dqi = dqi.reshape(-1, IDX_HEADS, IDX_DIM)
            dw = dw * ((IDX_HEADS ** -0.5) * (IDX_DIM ** -0.5))

            def diff_combine(o):
                b_, t_, _ = o.shape
                o = o.reshape(b_, t_, C_HEADS, 2, 2 * C_DIM)
                c = o[:, :, :, 0] - lam * o[:, :, :, 1]
                return (_rms(c, c_subln[i].astype(F32)) * (1.0 - lam_init)).reshape(b_, t_, -1)

            c_out_p = attend(rows_p(cq), rows_p(ck3), rows_p(cv3), c_tiles_p, p_blk, 2, True)
            d_out_p = _dsa_call(rows_p(dq), rows_p(dqi), rows_p(dw), rows_p(dk), rows_p(dv), rows_p(dki),
                                d_tiles, 0, n_sel_p)
            ck_all = sample_keys(cache_c_k[i].reshape(sb, past, 2 * C_HEADS, C_DIM), rows_s(ck3))
            cv_all = sample_keys(cache_c_v[i], rows_s(cv3))
            c_out_s = attend(sample_q(rows_s(cq)), ck_all, cv_all, c_tiles_s, S_BLK, 2, True, s_i)[:, :st]
            d_out_s = _dsa_call(_pad_rows(rows_s(dq), DSA_TQ), _pad_rows(rows_s(dqi), DSA_TQ),
                                _pad_rows(rows_s(dw), DSA_TQ),
                                sample_keys(cache_d_k[i], rows_s(dk)), sample_keys(cache_d_v[i], rows_s(dv)),
                                sample_keys(cache_d_kidx[i], rows_s(dki)), d_tiles, s_i_dsa, n_sel_s)[:, :st]

            mix = jnp.concatenate([
                jnp.concatenate([diff_combine(c_out_p).reshape(n_p, -1), d_out_p.reshape(n_p, -1)], axis=-1),
                jnp.concatenate([diff_combine(c_out_s).reshape(n_s, -1), d_out_s.reshape(n_s, -1)], axis=-1),
            ], axis=0)
            x = dense(mix.astype(BF16), odd_w_out[i].astype(BF16), res=x)

            for name, arr, shp in (("c_k", ck, (C_HEADS, 2 * C_DIM)), ("c_v", cv, (C_HEADS, 2 * C_DIM)),
                                   ("d_k", dk, (D_DIM,)), ("d_v", dv, (D_DIM,)), ("d_ki", dki, (IDX_DIM,))):
                outs_odd[name + "_p"].append(rows_p(arr).reshape(pb, pt, *shp))
                outs_odd[name + "_s"].append(rows_s(arr).reshape(sb, st, *shp))

        x = ffn(x, norm_ffn[l], ffn_w_gate[l].astype(BF16), ffn_w_up[l].astype(BF16), ffn_w_down[l].astype(BF16),
                final_g=final_norm if l == depth - 1 else None)

    y_prompt = x[:n_p].reshape(pb, pt, d)
    y_sample = x[n_p:].reshape(sb, st, d)
    se = {k: jnp.stack(v, axis=0) for k, v in outs_even.items()}
    so = {k: jnp.stack(v, axis=0) for k, v in outs_odd.items()}
    return (y_prompt, y_sample, se["a_k_p"], se["a_k_s"], se["a_v_p"], se["a_v_s"],
            se["ckv_p"], se["ckv_s"], se["kr_p"], se["kr_s"],
            so["c_k_p"], so["c_k_s"], so["c_v_p"], so["c_v_s"],
            so["d_k_p"], so["d_k_s"], so["d_v_p"], so["d_v_s"], so["d_ki_p"], so["d_ki_s"])
```

```python
import functools
import math

import numpy as np
import jax
import jax.numpy as jnp
from jax import lax
from jax.experimental import pallas as pl
from jax.experimental.pallas import tpu as pltpu

F32, BF16, I32 = jnp.float32, jnp.bfloat16, jnp.int32

D_MODEL = 1024
CHUNK = 64
EPS = 1e-6
A_HEADS, A_DIM, A_LEFT_CHUNKS, A_MAX_REL = 8, 64, 8, 128
B_HEADS, B_NOPE, B_ROPE, B_V = 8, 64, 32, 64
Q_LORA, KV_LORA = 256, 128
ROPE_THETA = 10000.0
C_HEADS, C_DIM = 4, 64
D_HEADS, D_DIM = 8, 64
IDX_HEADS, IDX_DIM = 8, 64
TOPK_MAX = 256
T5_BUCKETS, T5_MAX_DIST = 32, 128
EVEN_SPLITS = (512, 512, 512, Q_LORA, KV_LORA, B_ROPE)
ODD_SPLITS = (512, 512, 512, 512, D_DIM, D_DIM, 512, IDX_DIM, IDX_HEADS)

LANES = 128
SUBLANES = 8
VMEM_LIMIT = 56 * 1024 * 1024
MASKED = -1e30
UNSELECTED = -2e30
M_INIT = -1e30
SUM_MAX, SUM_MIN = 1e18, 1e-18
LOG2E = math.log2(math.e)
INT_MIN = -(2 ** 31)
INT_MAX = 2 ** 31 - 1

DSA_TQ, DSA_TK = 128, 256
COUNT_ROWS = 8 * SUBLANES
FAR_UNROLL = 4
DSA_FAR_UNROLL = 2


def _cparams(n_axes):
    return pltpu.CompilerParams(dimension_semantics=("arbitrary",) * n_axes, vmem_limit_bytes=VMEM_LIMIT)


def _rms(x, g):
    return x * lax.rsqrt(jnp.mean(x * x, axis=-1, keepdims=True) + EPS) * g


def _dense_kernel(*refs, norm, res, emit_h):
    it = iter(refs)
    x_ref = next(it)
    g_ref = next(it) if norm else None
    w_ref = next(it)
    r_ref = next(it) if res else None
    o_ref = next(it)
    h_ref = next(it) if emit_h else None
    x = x_ref[...]
    if norm:
        x = _rms(x.astype(F32), g_ref[...])
        if emit_h:
            h_ref[...] = x
    acc = jnp.dot(x.astype(BF16), w_ref[...], preferred_element_type=F32)
    if res:
        acc = acc + r_ref[...]
    o_ref[...] = acc


def dense(x, w, g=None, res=None, emit_h=False, tm=512):
    m, k = x.shape
    n = w.shape[1]
    assert m % tm == 0 and n % LANES == 0
    norm = g is not None
    args, specs = [x], [pl.BlockSpec((tm, k), lambda i: (i, 0))]
    if norm:
        args.append(g.reshape(1, k).astype(F32))
        specs.append(pl.BlockSpec((1, k), lambda i: (0, 0)))
    args.append(w)
    specs.append(pl.BlockSpec((k, n), lambda i: (0, 0)))
    if res is not None:
        args.append(res)
        specs.append(pl.BlockSpec((tm, n), lambda i: (i, 0)))
    out_shape = [jax.ShapeDtypeStruct((m, n), F32)]
    out_specs = [pl.BlockSpec((tm, n), lambda i: (i, 0))]
    if emit_h:
        out_shape.append(jax.ShapeDtypeStruct((m, k), F32))
        out_specs.append(pl.BlockSpec((tm, k), lambda i: (i, 0)))
    outs = pl.pallas_call(
        functools.partial(_dense_kernel, norm=norm, res=res is not None, emit_h=emit_h),
        grid=(m // tm,), in_specs=specs, out_specs=out_specs, out_shape=out_shape,
        compiler_params=_cparams(1), name="dense")(*args)
    return outs if emit_h else outs[0]


def _ffn_kernel(x_ref, g_ref, wg_ref, wu_ref, wd_ref, fg_ref, o_ref, h_sc, acc_sc, *, final):
    f = pl.program_id(1)

    @pl.when(f == 0)
    def _():
        x = x_ref[...]
        h_sc[...] = _rms(x, g_ref[...]).astype(BF16)
        acc_sc[...] = x

    h = h_sc[...]
    gate = jnp.dot(h, wg_ref[...], preferred_element_type=F32)
    up = jnp.dot(h, wu_ref[...], preferred_element_type=F32)
    a = (gate * jax.nn.sigmoid(gate) * up).astype(BF16)
    acc_sc[...] += jnp.dot(a, wd_ref[...], preferred_element_type=F32)

    @pl.when(f == pl.num_programs(1) - 1)
    def _():
        y = acc_sc[...]
        if final:
            y = _rms(y, fg_ref[...])
        o_ref[...] = y


def ffn(x, g, wg, wu, wd, final_g=None, tm=512, nf=2):
    m, d = x.shape
    hid = wg.shape[1]
    tf = hid // nf
    assert m % tm == 0 and hid % nf == 0 and tf % LANES == 0
    final = final_g is not None
    fg = (final_g if final else g).reshape(1, d).astype(F32)
    return pl.pallas_call(
        functools.partial(_ffn_kernel, final=final),
        grid=(m // tm, nf),
        in_specs=[pl.BlockSpec((tm, d), lambda i, f: (i, 0)),
                  pl.BlockSpec((1, d), lambda i, f: (0, 0)),
                  pl.BlockSpec((d, tf), lambda i, f: (0, f)),
                  pl.BlockSpec((d, tf), lambda i, f: (0, f)),
                  pl.BlockSpec((tf, d), lambda i, f: (f, 0)),
                  pl.BlockSpec((1, d), lambda i, f: (0, 0))],
        out_specs=pl.BlockSpec((tm, d), lambda i, f: (i, 0)),
        out_shape=jax.ShapeDtypeStruct((m, d), F32),
        scratch_shapes=[pltpu.VMEM((tm, d), BF16), pltpu.VMEM((tm, d), F32)],
        compiler_params=_cparams(2), name="ffn")(x, g.reshape(1, d).astype(F32), wg, wu, wd, fg)


def _flash_kernel(qt_ref, k_ref, vt_ref, tile_ref, o_ref, m_sc, l_sc, acc_sc, *, n_tiles, has_far, i_off):
    i = pl.program_id(2) + i_off
    qt = qt_ref[0, 0, 0]
    j_lo = 0 if has_far else jnp.maximum(i - (n_tiles - 1), 0)

    def raw(j):
        return jnp.dot(k_ref[0, 0, j], qt, preferred_element_type=F32)

    def scores(j):
        return raw(j) + tile_ref[0, jnp.minimum(i - j, n_tiles)]

    def reset():
        l_sc[...] = jnp.zeros(l_sc.shape, F32)
        acc_sc[...] = jnp.zeros(acc_sc.shape, F32)

    def accumulate(j, p):
        l_sc[...] += jnp.sum(p, axis=0, keepdims=True)
        acc_sc[...] += jnp.dot(vt_ref[0, 0, j], p.astype(BF16), preferred_element_type=F32)

    reset()
    m = jnp.max(scores(j_lo), axis=0, keepdims=True)
    if has_far:
        m_far = m - tile_ref[0, n_tiles, 0:1, :]

        n_far = jnp.maximum(i - (n_tiles - 1), 0)

        def far_group(jj, carry):
            ps = [jnp.exp2(raw(FAR_UNROLL * jj + u) - m_far) for u in range(FAR_UNROLL)]
            for u in range(FAR_UNROLL):
                accumulate(FAR_UNROLL * jj + u, ps[u])
            return carry

        def far_single(j, carry):
            accumulate(j, jnp.exp2(raw(j) - m_far))
            return carry

        n_grouped = (n_far // FAR_UNROLL) * FAR_UNROLL
        lax.fori_loop(0, n_far // FAR_UNROLL, far_group, 0)
        lax.fori_loop(n_grouped, n_far, far_single, 0)
    for t in reversed(range(n_tiles)):
        @pl.when(i - t >= 0)
        def _(t=t):
            accumulate(i - t, jnp.exp2(raw(i - t) + tile_ref[0, t] - m))

    l = l_sc[...]
    in_range = (jnp.max(l) < SUM_MAX) & (jnp.min(l) > SUM_MIN)

    @pl.when(jnp.logical_not(in_range))
    def _():
        reset()
        m_sc[...] = jnp.full(m_sc.shape, M_INIT, F32)

        def body(j, carry):
            s = scores(j)
            m_prev = m_sc[...]
            m_new = jnp.maximum(m_prev, jnp.max(s, axis=0, keepdims=True))
            alpha = jnp.exp2(m_prev - m_new)
            p = jnp.exp2(s - m_new)
            l_sc[...] = alpha * l_sc[...] + jnp.sum(p, axis=0, keepdims=True)
            acc_sc[...] = alpha * acc_sc[...] + jnp.dot(vt_ref[0, 0, j], p.astype(BF16), preferred_element_type=F32)
            m_sc[...] = m_new
            return carry

        lax.fori_loop(j_lo, i + 1, body, 0)

    o_ref[0, 0, 0] = acc_sc[...] / l_sc[...]


def flash(qt, k, vt, tiles, *, n_tiles, has_far, i_off=0):
    b, g, nq, dq, blk = qt.shape
    gk, nb = k.shape[1], k.shape[2]
    gv, dv = vt.shape[1], vt.shape[3]
    gb = tiles.shape[0]
    assert tiles.shape[1] == n_tiles + 1
    kdiv, vdiv, bdiv = g // gk, g // gv, g // gb
    return pl.pallas_call(
        functools.partial(_flash_kernel, n_tiles=n_tiles, has_far=has_far, i_off=i_off),
        grid=(b, g, nq),
        in_specs=[pl.BlockSpec((1, 1, 1, dq, blk), lambda bi, gi, i: (bi, gi, i, 0, 0)),
                  pl.BlockSpec((1, 1, nb, blk, dq), lambda bi, gi, i: (bi, gi // kdiv, 0, 0, 0)),
                  pl.BlockSpec((1, 1, nb, dv, blk), lambda bi, gi, i: (bi, gi // vdiv, 0, 0, 0)),
                  pl.BlockSpec((1, n_tiles + 1, blk, blk), lambda bi, gi, i: (gi // bdiv, 0, 0, 0))],
        out_specs=pl.BlockSpec((1, 1, 1, dv, blk), lambda bi, gi, i: (bi, gi, i, 0, 0)),
        out_shape=jax.ShapeDtypeStruct((b, g, nq, dv, blk), F32),
        scratch_shapes=[pltpu.VMEM((1, blk), F32), pltpu.VMEM((1, blk), F32), pltpu.VMEM((dv, blk), F32)],
        compiler_params=_cparams(3), name="flash")(qt, k, vt, tiles)


def _dsa_kernel(qit_ref, w_ref, ki_ref, qt_ref, k_ref, vt_ref, tile_ref, o_ref,
                key_sc, m_sc, l_sc, acc_sc, *, i_off, n_sel, idx_bits):
    tq, tk, nh = DSA_TQ, DSA_TK, D_HEADS
    i = pl.program_id(1) + i_off
    nkb = (i * tq) // tk + 1
    krow = lax.broadcasted_iota(I32, (tk, tq), 0)
    qcol = lax.broadcasted_iota(I32, (tk, tq), 1)
    q_chunk = jnp.right_shift(i * tq + qcol, 6)

    def head(x, h):
        return x[:, h * tq:(h + 1) * tq]

    qit = qit_ref[0, 0]
    w = w_ref[0, 0]

    def score_block(j):
        lg = jnp.maximum(jnp.dot(ki_ref[0, j], qit, preferred_element_type=F32), 0.0) * w
        sc = head(lg, 0)
        for h in range(1, IDX_HEADS):
            sc = sc + head(lg, h)
        sc = jnp.where(sc == 0.0, 0.0, sc)
        bits = pltpu.bitcast(sc, I32)
        key = bits ^ (jnp.right_shift(bits, 31) & INT_MAX)
        adm = jnp.right_shift(j * tk + krow, 6) <= q_chunk
        key_sc[j] = jnp.where(adm, key, INT_MIN)

    def score_pair(jj, carry):
        score_block(2 * jj)
        score_block(2 * jj + 1)
        return carry

    def score_single(j, carry):
        score_block(j)
        return carry

    lax.fori_loop(0, nkb // 2, score_pair, 0)
    lax.fori_loop((nkb // 2) * 2, nkb, score_single, 0)

    def count(pred_fn):
        def body(j, acc):
            ind = jnp.where(pred_fn(key_sc[j], j * tk + krow), 1.0, 0.0)
            return acc + jnp.sum(ind.reshape(tk // COUNT_ROWS, COUNT_ROWS, tq), axis=0)
        acc = lax.fori_loop(0, nkb, body, jnp.zeros((COUNT_ROWS, tq), F32))
        return jnp.sum(acc, axis=0, keepdims=True)

    target = float(n_sel)
    qpos = i * tq + lax.broadcasted_iota(I32, (1, tq), 1)
    n_adm = ((jnp.right_shift(qpos, 6) + 1) * CHUNK).astype(F32)
    real = n_adm > target

    def search_cond(state):
        thr, c_thr, b = state
        settled = (c_thr == target) | jnp.logical_not(real)
        return (b < 32) & (jnp.min(jnp.where(settled, 1.0, 0.0)) < 1.0)

    def value_bit(state):
        thr, c_thr, b = state
        cand = thr + jnp.left_shift(jnp.int32(1), 31 - b)
        cnt = count(lambda k, idx: k >= cand)
        ge = cnt >= target
        return jnp.where(ge, cand, thr), jnp.where(ge, cnt, c_thr), b + 1

    thr, c_thr, _ = lax.while_loop(search_cond, value_bit,
                                   (jnp.full((1, tq), INT_MIN, I32), n_adm, jnp.int32(0)))

    def tie_search():
        need = target - count(lambda k, idx: k > thr)

        def index_bit(b, y):
            cand = y + jnp.left_shift(jnp.int32(1), idx_bits - 1 - b)
            cnt = count(lambda k, idx: (k == thr) & (idx < cand))
            return jnp.where(cnt < need, cand, y)
        return lax.fori_loop(0, idx_bits, index_bit, jnp.zeros((1, tq), I32))

    has_ties = jnp.max(jnp.where((c_thr > target) & real, 1.0, 0.0)) > 0.0
    last = lax.cond(has_ties, tie_search, lambda: jnp.full((1, tq), INT_MAX, I32))
    last = jnp.where(real, last, -1)

    qt = qt_ref[0, 0]
    n_near = 3

    def raw(j):
        return jnp.dot(k_ref[0, j], qt, preferred_element_type=F32)

    def tile_of(j):
        return jnp.minimum((i * tq - j * tk) // tq, n_near)

    def select(j, s):
        key = key_sc[j]
        sel = (key > thr) | ((key == thr) & (j * tk + krow <= last))
        selb = jnp.where(sel, 0.0, UNSELECTED)
        return jnp.concatenate([head(s, h) + selb for h in range(nh)], axis=1)

    def reset():
        l_sc[...] = jnp.zeros(l_sc.shape, F32)
        acc_sc[...] = jnp.zeros(acc_sc.shape, F32)

    def accumulate(j, p):
        l_sc[...] += jnp.sum(p, axis=0, keepdims=True)
        acc_sc[...] += jnp.dot(vt_ref[0, j], p.astype(BF16), preferred_element_type=F32)

    reset()
    m = jnp.max(raw(0) + tile_ref[tile_of(0)], axis=0, keepdims=True)
    m_far = m - tile_ref[n_near, 0:1, :]
    n_far = jnp.maximum(i - 1, 0) // 2

    def far_group(jj, carry):
        ps = [jnp.exp2(select(DSA_FAR_UNROLL * jj + u, raw(DSA_FAR_UNROLL * jj + u)) - m_far)
              for u in range(DSA_FAR_UNROLL)]
        for u in range(DSA_FAR_UNROLL):
            accumulate(DSA_FAR_UNROLL * jj + u, ps[u])
        return carry

    def far_single(j, carry):
        accumulate(j, jnp.exp2(select(j, raw(j)) - m_far))
        return carry

    def near(j, carry):
        accumulate(j, jnp.exp2(select(j, raw(j) + tile_ref[tile_of(j)]) - m))
        return carry

    n_grouped = (n_far // DSA_FAR_UNROLL) * DSA_FAR_UNROLL
    lax.fori_loop(0, n_far // DSA_FAR_UNROLL, far_group, 0)
    lax.fori_loop(n_grouped, n_far, far_single, 0)
    lax.fori_loop(n_far, nkb, near, 0)

    l = l_sc[...]
    in_range = (jnp.max(l) < SUM_MAX) & (jnp.min(l) > SUM_MIN)

    @pl.when(jnp.logical_not(in_range))
    def _():
        reset()
        m_sc[...] = jnp.full(m_sc.shape, M_INIT, F32)

        def attend(j, carry):
            s = select(j, raw(j) + tile_ref[tile_of(j)])
            m_prev = m_sc[...]
            m_new = jnp.maximum(m_prev, jnp.max(s, axis=0, keepdims=True))
            alpha = jnp.exp2(m_prev - m_new)
            p = jnp.exp2(s - m_new)
            l_sc[...] = alpha * l_sc[...] + jnp.sum(p, axis=0, keepdims=True)
            acc_sc[...] = alpha * acc_sc[...] + jnp.dot(vt_ref[0, j], p.astype(BF16), preferred_element_type=F32)
            m_sc[...] = m_new
            return carry

        lax.fori_loop(0, nkb, attend, 0)

    o_ref[0, 0] = acc_sc[...] / l_sc[...]


def dsa(qit, w, ki, qt, k, vt, tiles, *, i_off, n_sel):
    b, nq, kdim, cols = qit.shape
    nkb = ki.shape[1]
    dk, dv = k.shape[-1], vt.shape[2]
    idx_bits = max(1, (nkb * DSA_TK - 1).bit_length())
    return pl.pallas_call(
        functools.partial(_dsa_kernel, i_off=i_off, n_sel=n_sel, idx_bits=idx_bits),
        grid=(b, nq),
        in_specs=[pl.BlockSpec((1, 1, kdim, cols), lambda bi, i: (bi, i, 0, 0)),
                  pl.BlockSpec((1, 1, 1, cols), lambda bi, i: (bi, i, 0, 0)),
                  pl.BlockSpec((1, nkb, DSA_TK, kdim), lambda bi, i: (bi, 0, 0, 0)),
                  pl.BlockSpec((1, 1, dk, cols), lambda bi, i: (bi, i, 0, 0)),
                  pl.BlockSpec((1, nkb, DSA_TK, dk), lambda bi, i: (bi, 0, 0, 0)),
                  pl.BlockSpec((1, nkb, dv, DSA_TK), lambda bi, i: (bi, 0, 0, 0)),
                  pl.BlockSpec((4, DSA_TK, cols), lambda bi, i: (0, 0, 0))],
        out_specs=pl.BlockSpec((1, 1, dv, cols), lambda bi, i: (bi, i, 0, 0)),
        out_shape=jax.ShapeDtypeStruct((b, nq, dv, cols), F32),
        scratch_shapes=[pltpu.VMEM((nkb, DSA_TK, DSA_TQ), I32), pltpu.VMEM((1, cols), F32),
                        pltpu.VMEM((1, cols), F32), pltpu.VMEM((dv, cols), F32)],
        compiler_params=_cparams(2), name="dsa")(qit, w, ki, qt, k, vt, tiles)


def _rope_tables(pos):
    half = B_ROPE // 2
    freqs = jnp.power(jnp.float32(ROPE_THETA), -jnp.arange(half, dtype=F32) / half)
    ang = pos.astype(F32)[:, None] * freqs[None, :]
    return jnp.cos(ang), jnp.sin(ang)


def _rope(x, cos, sin):
    half = x.shape[-1] // 2
    shape = (x.shape[0],) + (1,) * (x.ndim - 2) + (half,)
    c, s = cos.reshape(shape), sin.reshape(shape)
    x1, x2 = x[..., :half], x[..., half:]
    return jnp.concatenate([x1 * c - x2 * s, x1 * s + x2 * c], axis=-1)


def _t5_bucket(rel):
    nb = T5_BUCKETS // 2
    max_exact = nb // 2
    ret = jnp.where(rel > 0, nb, 0)
    n = jnp.abs(rel)
    nf = jnp.maximum(n, 1).astype(F32)
    large = max_exact + (jnp.log(nf / max_exact) / math.log(T5_MAX_DIST / max_exact)
                         * (nb - max_exact)).astype(I32)
    large = jnp.minimum(large, nb - 1)
    return ret + jnp.where(n < max_exact, n, large)


def _chunk_diff(tq, tk, n_tiles):
    r = np.arange(tq)[None, None, :]
    c = np.arange(tk)[None, :, None]
    t = np.arange(n_tiles)[:, None, None]
    return c // CHUNK - r // CHUNK - t * (tq // CHUNK)


def _strip_distances(tq, tk, n_tiles):
    y = np.arange(tq + tk)[None, :]
    t = np.arange(n_tiles)[:, None]
    return np.where(y < tq, -y, tq + tk - y) - t * tq


def _toeplitz(strip, tq, tk):
    n = tq + tk
    lead = strip.shape[:-1]
    rows = jnp.broadcast_to(strip[..., None, :], lead + (tk, n)).reshape(lead + (tk * n,))
    return rows[..., :tk * (n - 1)].reshape(lead + (tk, n - 1))[..., :tq]


def band_tiles(relbias, blk):
    n_tiles = (A_LEFT_CHUNKS * CHUNK) // blk + 1
    dchunk = _chunk_diff(blk, blk, n_tiles)
    adm = (dchunk <= 0) & (-dchunk <= A_LEFT_CHUNKS)
    idx = np.clip(-_strip_distances(blk, blk, n_tiles), -A_MAX_REL, A_MAX_REL) + A_MAX_REL
    bias = _toeplitz(relbias.astype(F32)[:, idx] * LOG2E, blk, blk)
    tiles = jnp.where(adm[None], bias, MASKED)
    far = jnp.full((relbias.shape[0], 1, blk, blk), MASKED, F32)
    return jnp.concatenate([tiles, far], axis=1), n_tiles


def causal_tiles(blk):
    t0 = np.where(_chunk_diff(blk, blk, 1) <= 0, 0.0, MASKED).astype(np.float32)
    return jnp.asarray(np.concatenate([t0, np.zeros_like(t0)], axis=0)[None])


def t5_tiles(tab, tq, tk, n_tiles):
    assert n_tiles * tq - tk + 1 >= T5_MAX_DIST
    tab = tab.astype(F32) * LOG2E
    strip = jnp.moveaxis(tab[_t5_bucket(jnp.asarray(_strip_distances(tq, tk, n_tiles), I32))], -1, 0)
    tiles = jnp.where((_chunk_diff(tq, tk, n_tiles) <= 0)[None], _toeplitz(strip, tq, tk), MASKED)
    far = tab[_t5_bucket(jnp.asarray([-(n_tiles * tq + tk)], I32))[0]]
    far = jnp.broadcast_to(far[:, None, None, None], (tab.shape[1], 1, tk, tq))
    return jnp.concatenate([tiles, far], axis=1)


def _row_blocks(x, blk):
    b, t, g, d = x.shape
    return jnp.transpose(x.reshape(b, t // blk, blk, g, d), (0, 3, 1, 2, 4)).astype(BF16)


def _col_blocks(x, blk):
    b, t, g, d = x.shape
    return jnp.transpose(x.reshape(b, t // blk, blk, g, d), (0, 3, 1, 4, 2)).astype(BF16)


def _from_col_blocks(o):
    b, g, nq, d, blk = o.shape
    return jnp.transpose(o, (0, 2, 4, 1, 3)).reshape(b, nq * blk, g * d)


def _pad_rows(x, n, front=0):
    back = n - x.shape[1] - front
    return jnp.pad(x, ((0, 0), (front, back)) + ((0, 0),) * (x.ndim - 2))


def _split(h, sizes):
    out, o = [], 0
    for s in sizes:
        out.append(h[:, o:o + s])
        o += s
    return out


def _split3(x):
    hi = x.astype(BF16)
    lo = (x - hi.astype(F32)).astype(BF16)
    return hi, lo


def _dsa_call(dq, dqi, dw, dk_all, dv_all, dki_all, tiles, i_off, n_sel):
    b, tq_all = dq.shape[:2]
    nq, nkb = tq_all // DSA_TQ, dk_all.shape[1] // DSA_TK

    def stack_t(x):
        d = x.shape[-1]
        return jnp.transpose(x.reshape(b, nq, DSA_TQ, D_HEADS, d), (0, 1, 4, 3, 2)).reshape(b, nq, d, D_HEADS * DSA_TQ)

    qh, ql = _split3(dqi)
    kh, kl = _split3(dki_all)
    qit = stack_t(jnp.concatenate([qh, ql, qh], axis=-1))
    ki = jnp.concatenate([kh, kh, kl], axis=-1).reshape(b, nkb, DSA_TK, 3 * IDX_DIM)
    w = stack_t(dw[..., None])
    qt = stack_t((dq * (D_DIM ** -0.5 * LOG2E)).astype(BF16))
    k = dk_all.astype(BF16).reshape(b, nkb, DSA_TK, D_DIM)
    vt = jnp.transpose(dv_all.astype(BF16).reshape(b, nkb, DSA_TK, D_DIM), (0, 1, 3, 2))
    o = dsa(qit, w, ki, qt, k, vt, tiles, i_off=i_off, n_sel=n_sel)
    o = jnp.transpose(o.reshape(b, nq, D_DIM, D_HEADS, DSA_TQ), (0, 1, 4, 3, 2))
    return o.reshape(b, tq_all, D_HEADS * D_DIM)


P_BLK = 512
S_BLK = 128


def kernel(x_prompt, x_sample, cache_a_k, cache_a_v, cache_b_ckv, cache_b_krope, cache_c_k, cache_c_v, cache_d_k, cache_d_v, cache_d_kidx, t5_table, norm_attn, norm_ffn, final_norm, even_w_in, even_w_out, a_relbias, b_q_norm, b_kv_norm, b_w_uq, b_w_ukv, odd_w_in, odd_w_out, c_lambda_q1, c_lambda_k1, c_lambda_q2, c_lambda_k2, c_subln, ffn_w_gate, ffn_w_up, ffn_w_down):
    pb, pt, d = x_prompt.shape
    sb, st, _ = x_sample.shape
    assert pb == 1 and st == CHUNK
    past = cache_b_ckv.shape[2]
    n_a = cache_a_k.shape[2]
    depth = norm_attn.shape[0]
    n_p, n_s = pb * pt, sb * st
    s_len = past + st
    s_pad = -(-s_len // DSA_TK) * DSA_TK
    s_i = past // S_BLK
    s_i_dsa = past // DSA_TQ
    assert past % S_BLK == 0 and s_pad % S_BLK == 0 and past >= n_a
    p_blk = min(P_BLK, pt)

    x = jnp.concatenate([x_prompt.reshape(n_p, d), x_sample.reshape(n_s, d)], axis=0)
    pos = jnp.concatenate([jnp.arange(pt), jnp.tile(past + jnp.arange(st), sb)])
    cos, sin = _rope_tables(pos)

    def rows_p(a):
        return a[:n_p].reshape(pb, pt, *a.shape[1:])

    def rows_s(a):
        return a[n_p:].reshape(sb, st, *a.shape[1:])

    def pad_cols(w):
        n = -(-w.shape[1] // LANES) * LANES
        return jnp.pad(w, ((0, 0), (0, n - w.shape[1]))).astype(BF16)

    def sample_q(a):
        return _pad_rows(a, S_BLK)

    def sample_keys(cache, new, front=0):
        return _pad_rows(jnp.concatenate([cache, new], axis=1), s_pad, front)

    def attend(q, k, v, tiles, blk, n_tiles, has_far, i_off=0):
        o = flash(_col_blocks(q, blk), _row_blocks(k, blk), _col_blocks(v, blk), tiles,
                  n_tiles=n_tiles, has_far=has_far, i_off=i_off)
        return _from_col_blocks(o)

    t5_c, t5_d = t5_table[:, :C_HEADS], t5_table[:, C_HEADS:]
    ctiles_p = causal_tiles(p_blk)
    ctiles_s = causal_tiles(S_BLK)
    c_tiles_p = t5_tiles(t5_c, p_blk, p_blk, 2)
    c_tiles_s = t5_tiles(t5_c, S_BLK, S_BLK, 2)
    d_tiles = t5_tiles(t5_d, DSA_TQ, DSA_TK, 3)
    d_tiles = jnp.transpose(d_tiles, (1, 2, 0, 3)).reshape(4, DSA_TK, D_HEADS * DSA_TQ)
    n_sel_p = min(TOPK_MAX, pt // 4)
    n_sel_s = min(TOPK_MAX, s_len // 4)

    outs_even = {k: [] for k in ("a_k_p", "a_k_s", "a_v_p", "a_v_s", "ckv_p", "ckv_s", "kr_p", "kr_s")}
    outs_odd = {k: [] for k in ("c_k_p", "c_k_s", "c_v_p", "c_v_s", "d_k_p", "d_k_s", "d_v_p", "d_v_s", "d_ki_p", "d_ki_s")}

    for l in range(depth):
        i = l // 2
        if l % 2 == 0:
            h = dense(x, pad_cols(even_w_in[i]), g=norm_attn[l])
            aq, ak, av, cq, ckv_raw, kr_raw = _split(h, EVEN_SPLITS)
            qb = dense(cq, b_w_uq[i].astype(BF16), g=b_q_norm[i]).reshape(-1, B_HEADS, B_NOPE + B_ROPE)
            kv_new, ckv = dense(ckv_raw, b_w_ukv[i].astype(BF16), g=b_kv_norm[i], emit_h=True)
            kr = _rope(kr_raw, cos, sin)
            q_mla = jnp.concatenate([qb[..., :B_NOPE], _rope(qb[..., B_NOPE:], cos, sin)], axis=-1)
            q_mla = q_mla * ((B_NOPE + B_ROPE) ** -0.5 * LOG2E)
            kv_new = kv_new.reshape(-1, B_HEADS, B_NOPE + B_V)
            aq = (aq * (A_DIM ** -0.5 * LOG2E)).reshape(-1, A_HEADS, A_DIM)
            ak = ak.reshape(-1, A_HEADS, A_DIM)
            av = av.reshape(-1, A_HEADS, A_DIM)

            def mla_keys(kn, krope):
                return jnp.concatenate([kn, jnp.broadcast_to(krope[:, :, None, :], kn.shape[:3] + (B_ROPE,))], axis=-1)

            a_tiles_p, nt_p = band_tiles(a_relbias[i], p_blk)
            a_out_p = attend(rows_p(aq), rows_p(ak), rows_p(av), a_tiles_p, p_blk, nt_p, False)
            kv_p = rows_p(kv_new)
            b_out_p = attend(rows_p(q_mla), mla_keys(kv_p[..., :B_NOPE], rows_p(kr)), kv_p[..., B_NOPE:],
                             ctiles_p, p_blk, 1, True)
            a_tiles_s, nt_s = band_tiles(a_relbias[i], S_BLK)
            ak_all = jnp.concatenate([cache_a_k[i], rows_s(ak)], axis=1)
            av_all = jnp.concatenate([cache_a_v[i], rows_s(av)], axis=1)
            a_out_s = attend(sample_q(rows_s(aq)), _pad_rows(ak_all, s_pad, past - n_a),
                             _pad_rows(av_all, s_pad, past - n_a), a_tiles_s, S_BLK, nt_s, False, s_i)[:, :st]
            kv_c = dense(cache_b_ckv[i].reshape(sb * past, KV_LORA), b_w_ukv[i].astype(BF16))
            kv_c = kv_c.reshape(sb, past, B_HEADS, B_NOPE + B_V)
            kv_s = rows_s(kv_new)
            kn_all = sample_keys(kv_c[..., :B_NOPE], kv_s[..., :B_NOPE])
            v_all = sample_keys(kv_c[..., B_NOPE:], kv_s[..., B_NOPE:])
            kr_all = sample_keys(cache_b_krope[i], rows_s(kr))
            b_out_s = attend(sample_q(rows_s(q_mla)), mla_keys(kn_all, kr_all), v_all,
                             ctiles_s, S_BLK, 1, True, s_i)[:, :st]

            mix = jnp.concatenate([
                jnp.concatenate([a_out_p.reshape(n_p, -1), b_out_p.reshape(n_p, -1)], axis=-1),
                jnp.concatenate([a_out_s.reshape(n_s, -1), b_out_s.reshape(n_s, -1)], axis=-1),
            ], axis=0)
            x = dense(mix.astype(BF16), even_w_out[i].astype(BF16), res=x)

            outs_even["a_k_p"].append(rows_p(ak)[:, pt - n_a:])
            outs_even["a_v_p"].append(rows_p(av)[:, pt - n_a:])
            outs_even["a_k_s"].append(ak_all[:, -n_a:])
            outs_even["a_v_s"].append(av_all[:, -n_a:])
            outs_even["ckv_p"].append(rows_p(ckv))
            outs_even["ckv_s"].append(rows_s(ckv))
            outs_even["kr_p"].append(rows_p(kr))
            outs_even["kr_s"].append(rows_s(kr))
        else:
            lam_init = 0.8 - 0.6 * math.exp(-0.3 * l)
            lam = (jnp.exp(jnp.sum(c_lambda_q1[i].astype(F32) * c_lambda_k1[i].astype(F32)))
                   - jnp.exp(jnp.sum(c_lambda_q2[i].astype(F32) * c_lambda_k2[i].astype(F32))) + lam_init)
            h = dense(x, pad_cols(odd_w_in[i]), g=norm_attn[l])
            cq, ck, cv, dq, dk, dv, dqi, dki, dw = _split(h, ODD_SPLITS)
            cq = (cq * (C_DIM ** -0.5 * LOG2E)).reshape(-1, 2 * C_HEADS, C_DIM)
            ck3 = ck.reshape(-1, 2 * C_HEADS, C_DIM)
            cv3 = cv.reshape(-1, C_HEADS, 2 * C_DIM)
            dq = dq.reshape(-1, D_HEADS, D_DIM)
            dqi = dqi.reshape(-1, IDX_HEADS, IDX_DIM)
            dw = dw * ((IDX_HEADS ** -0.5) * (IDX_DIM ** -0.5))

            def diff_combine(o):
                b_, t_, _ = o.shape
                o = o.reshape(b_, t_, C_HEADS, 2, 2 * C_DIM)
                c = o[:, :, :, 0] - lam * o[:, :, :, 1]
                return (_rms(c, c_subln[i].astype(F32)) * (1.0 - lam_init)).reshape(b_, t_, -1)

            c_out_p = attend(rows_p(cq), rows_p(ck3), rows_p(cv3), c_tiles_p, p_blk, 2, True)
            d_out_p = _dsa_call(rows_p(dq), rows_p(dqi), rows_p(dw), rows_p(dk), rows_p(dv), rows_p(dki),
                                d_tiles, 0, n_sel_p)
            ck_all = sample_keys(cache_c_k[i].reshape(sb, past, 2 * C_HEADS, C_DIM), rows_s(ck3))
            cv_all = sample_keys(cache_c_v[i], rows_s(cv3))
            c_out_s = attend(sample_q(rows_s(cq)), ck_all, cv_all, c_tiles_s, S_BLK, 2, True, s_i)[:, :st]
            d_out_s = _dsa_call(_pad_rows(rows_s(dq), DSA_TQ), _pad_rows(rows_s(dqi), DSA_TQ),
                                _pad_rows(rows_s(dw), DSA_TQ),
                                sample_keys(cache_d_k[i], rows_s(dk)), sample_keys(cache_d_v[i], rows_s(dv)),
                                sample_keys(cache_d_kidx[i], rows_s(dki)), d_tiles, s_i_dsa, n_sel_s)[:, :st]

            mix = jnp.concatenate([
                jnp.concatenate([diff_combine(c_out_p).reshape(n_p, -1), d_out_p.reshape(n_p, -1)], axis=-1),
                jnp.concatenate([diff_combine(c_out_s).reshape(n_s, -1), d_out_s.reshape(n_s, -1)], axis=-1),
            ], axis=0)
            x = dense(mix.astype(BF16), odd_w_out[i].astype(BF16), res=x)

            for name, arr, shp in (("c_k", ck, (C_HEADS, 2 * C_DIM)), ("c_v", cv, (C_HEADS, 2 * C_DIM)),
                                   ("d_k", dk, (D_DIM,)), ("d_v", dv, (D_DIM,)), ("d_ki", dki, (IDX_DIM,))):
                outs_odd[name + "_p"].append(rows_p(arr).reshape(pb, pt, *shp))
                outs_odd[name + "_s"].append(rows_s(arr).reshape(sb, st, *shp))

        x = ffn(x, norm_ffn[l], ffn_w_gate[l].astype(BF16), ffn_w_up[l].astype(BF16), ffn_w_down[l].astype(BF16),
                final_g=final_norm if l == depth - 1 else None)

    y_prompt = x[:n_p].reshape(pb, pt, d)
    y_sample = x[n_p:].reshape(sb, st, d)
    se = {k: jnp.stack(v, axis=0) for k, v in outs_even.items()}
    so = {k: jnp.stack(v, axis=0) for k, v in outs_odd.items()}
    return (y_prompt, y_sample, se["a_k_p"], se["a_k_s"], se["a_v_p"], se["a_v_s"],
            se["ckv_p"], se["ckv_s"], se["kr_p"], se["kr_s"],
            so["c_k_p"], so["c_k_s"], so["c_v_p"], so["c_v_s"],
            so["d_k_p"], so["d_k_s"], so["d_v_p"], so["d_v_s"], so["d_ki_p"], so["d_ki_s"])
```

```python
import functools
import math

import numpy as np
import jax
import jax.numpy as jnp
from jax import lax
from jax.experimental import pallas as pl
from jax.experimental.pallas import tpu as pltpu

F32, BF16, I32 = jnp.float32, jnp.bfloat16, jnp.int32

D_MODEL = 1024
CHUNK = 64
EPS = 1e-6
A_HEADS, A_DIM, A_LEFT_CHUNKS, A_MAX_REL = 8, 64, 8, 128
B_HEADS, B_NOPE, B_ROPE, B_V = 8, 64, 32, 64
Q_LORA, KV_LORA = 256, 128
ROPE_THETA = 10000.0
C_HEADS, C_DIM = 4, 64
D_HEADS, D_DIM = 8, 64
IDX_HEADS, IDX_DIM = 8, 64
TOPK_MAX = 256
T5_BUCKETS, T5_MAX_DIST = 32, 128
EVEN_SPLITS = (512, 512, 512, Q_LORA, KV_LORA, B_ROPE)
ODD_SPLITS = (512, 512, 512, 512, D_DIM, D_DIM, 512, IDX_DIM, IDX_HEADS)

LANES = 128
SUBLANES = 8
VMEM_LIMIT = 56 * 1024 * 1024
MASKED = -1e30
UNSELECTED = -2e30
M_INIT = -1e30
SUM_MAX, SUM_MIN = 1e18, 1e-18
LOG2E = math.log2(math.e)
INT_MIN = -(2 ** 31)
INT_MAX = 2 ** 31 - 1

DSA_TQ, DSA_TK = 128, 256
COUNT_ROWS = 8 * SUBLANES
FAR_UNROLL = 4
DSA_FAR_UNROLL = 2


def _cparams(n_axes):
    return pltpu.CompilerParams(dimension_semantics=("arbitrary",) * n_axes, vmem_limit_bytes=VMEM_LIMIT)


def _rms(x, g):
    return x * lax.rsqrt(jnp.mean(x * x, axis=-1, keepdims=True) + EPS) * g


def _dense_kernel(*refs, norm, res, emit_h):
    it = iter(refs)
    x_ref = next(it)
    g_ref = next(it) if norm else None
    w_ref = next(it)
    r_ref = next(it) if res else None
    o_ref = next(it)
    h_ref = next(it) if emit_h else None
    x = x_ref[...]
    if norm:
        x = _rms(x.astype(F32), g_ref[...])
        if emit_h:
            h_ref[...] = x
    acc = jnp.dot(x.astype(BF16), w_ref[...], preferred_element_type=F32)
    if res:
        acc = acc + r_ref[...]
    o_ref[...] = acc


def dense(x, w, g=None, res=None, emit_h=False, tm=512):
    m, k = x.shape
    n = w.shape[1]
    assert m % tm == 0 and n % LANES == 0
    norm = g is not None
    args, specs = [x], [pl.BlockSpec((tm, k), lambda i: (i, 0))]
    if norm:
        args.append(g.reshape(1, k).astype(F32))
        specs.append(pl.BlockSpec((1, k), lambda i: (0, 0)))
    args.append(w)
    specs.append(pl.BlockSpec((k, n), lambda i: (0, 0)))
    if res is not None:
        args.append(res)
        specs.append(pl.BlockSpec((tm, n), lambda i: (i, 0)))
    out_shape = [jax.ShapeDtypeStruct((m, n), F32)]
    out_specs = [pl.BlockSpec((tm, n), lambda i: (i, 0))]
    if emit_h:
        out_shape.append(jax.ShapeDtypeStruct((m, k), F32))
        out_specs.append(pl.BlockSpec((tm, k), lambda i: (i, 0)))
    outs = pl.pallas_call(
        functools.partial(_dense_kernel, norm=norm, res=res is not None, emit_h=emit_h),
        grid=(m // tm,), in_specs=specs, out_specs=out_specs, out_shape=out_shape,
        compiler_params=_cparams(1), name="dense")(*args)
    return outs if emit_h else outs[0]


def _ffn_kernel(x_ref, g_ref, wg_ref, wu_ref, wd_ref, fg_ref, o_ref, h_sc, acc_sc, *, final):
    f = pl.program_id(1)

    @pl.when(f == 0)
    def _():
        x = x_ref[...]
        h_sc[...] = _rms(x, g_ref[...]).astype(BF16)
        acc_sc[...] = x

    h = h_sc[...]
    gate = jnp.dot(h, wg_ref[...], preferred_element_type=F32)
    up = jnp.dot(h, wu_ref[...], preferred_element_type=F32)
    a = (gate * jax.nn.sigmoid(gate) * up).astype(BF16)
    acc_sc[...] += jnp.dot(a, wd_ref[...], preferred_element_type=F32)

    @pl.when(f == pl.num_programs(1) - 1)
    def _():
        y = acc_sc[...]
        if final:
            y = _rms(y, fg_ref[...])
        o_ref[...] = y


def ffn(x, g, wg, wu, wd, final_g=None, tm=512, nf=2):
    m, d = x.shape
    hid = wg.shape[1]
    tf = hid // nf
    assert m % tm == 0 and hid % nf == 0 and tf % LANES == 0
    final = final_g is not None
    fg = (final_g if final else g).reshape(1, d).astype(F32)
    return pl.pallas_call(
        functools.partial(_ffn_kernel, final=final),
        grid=(m // tm, nf),
        in_specs=[pl.BlockSpec((tm, d), lambda i, f: (i, 0)),
                  pl.BlockSpec((1, d), lambda i, f: (0, 0)),
                  pl.BlockSpec((d, tf), lambda i, f: (0, f)),
                  pl.BlockSpec((d, tf), lambda i, f: (0, f)),
                  pl.BlockSpec((tf, d), lambda i, f: (f, 0)),
                  pl.BlockSpec((1, d), lambda i, f: (0, 0))],
        out_specs=pl.BlockSpec((tm, d), lambda i, f: (i, 0)),
        out_shape=jax.ShapeDtypeStruct((m, d), F32),
        scratch_shapes=[pltpu.VMEM((tm, d), BF16), pltpu.VMEM((tm, d), F32)],
        compiler_params=_cparams(2), name="ffn")(x, g.reshape(1, d).astype(F32), wg, wu, wd, fg)


def _flash_kernel(qt_ref, k_ref, vt_ref, tile_ref, o_ref, m_sc, l_sc, acc_sc, *, n_tiles, has_far, i_off):
    i = pl.program_id(2) + i_off
    qt = qt_ref[0, 0, 0]
    j_lo = 0 if has_far else jnp.maximum(i - (n_tiles - 1), 0)

    def raw(j):
        return jnp.dot(k_ref[0, 0, j], qt, preferred_element_type=F32)

    def scores(j):
        return raw(j) + tile_ref[0, jnp.minimum(i - j, n_tiles)]

    def reset():
        l_sc[...] = jnp.zeros(l_sc.shape, F32)
        acc_sc[...] = jnp.zeros(acc_sc.shape, F32)

    def accumulate(j, p):
        l_sc[...] += jnp.sum(p, axis=0, keepdims=True)
        acc_sc[...] += jnp.dot(vt_ref[0, 0, j], p.astype(BF16), preferred_element_type=F32)

    reset()
    m = jnp.max(scores(j_lo), axis=0, keepdims=True)
    if has_far:
        m_far = m - tile_ref[0, n_tiles, 0:1, :]

        n_far = jnp.maximum(i - (n_tiles - 1), 0)

        def far_group(jj, carry):
            ps = [jnp.exp2(raw(FAR_UNROLL * jj + u) - m_far) for u in range(FAR_UNROLL)]
            for u in range(FAR_UNROLL):
                accumulate(FAR_UNROLL * jj + u, ps[u])
            return carry

        def far_single(j, carry):
            accumulate(j, jnp.exp2(raw(j) - m_far))
            return carry

        n_grouped = (n_far // FAR_UNROLL) * FAR_UNROLL
        lax.fori_loop(0, n_far // FAR_UNROLL, far_group, 0)
        lax.fori_loop(n_grouped, n_far, far_single, 0)
    for t in reversed(range(n_tiles)):
        @pl.when(i - t >= 0)
        def _(t=t):
            accumulate(i - t, jnp.exp2(raw(i - t) + tile_ref[0, t] - m))

    l = l_sc[...]
    in_range = (jnp.max(l) < SUM_MAX) & (jnp.min(l) > SUM_MIN)

    @pl.when(jnp.logical_not(in_range))
    def _():
        reset()
        m_sc[...] = jnp.full(m_sc.shape, M_INIT, F32)

        def body(j, carry):
            s = scores(j)
            m_prev = m_sc[...]
            m_new = jnp.maximum(m_prev, jnp.max(s, axis=0, keepdims=True))
            alpha = jnp.exp2(m_prev - m_new)
            p = jnp.exp2(s - m_new)
            l_sc[...] = alpha * l_sc[...] + jnp.sum(p, axis=0, keepdims=True)
            acc_sc[...] = alpha * acc_sc[...] + jnp.dot(vt_ref[0, 0, j], p.astype(BF16), preferred_element_type=F32)
            m_sc[...] = m_new
            return carry

        lax.fori_loop(j_lo, i + 1, body, 0)

    o_ref[0, 0, 0] = (acc_sc[...] / l_sc[...]).astype(o_ref.dtype)


def flash(qt, k, vt, tiles, *, n_tiles, has_far, i_off=0, out_dtype=F32):
    b, g, nq, dq, blk = qt.shape
    gk, nb = k.shape[1], k.shape[2]
    gv, dv = vt.shape[1], vt.shape[3]
    gb = tiles.shape[0]
    assert tiles.shape[1] == n_tiles + 1
    kdiv, vdiv, bdiv = g // gk, g // gv, g // gb
    return pl.pallas_call(
        functools.partial(_flash_kernel, n_tiles=n_tiles, has_far=has_far, i_off=i_off),
        grid=(b, g, nq),
        in_specs=[pl.BlockSpec((1, 1, 1, dq, blk), lambda bi, gi, i: (bi, gi, i, 0, 0)),
                  pl.BlockSpec((1, 1, nb, blk, dq), lambda bi, gi, i: (bi, gi // kdiv, 0, 0, 0)),
                  pl.BlockSpec((1, 1, nb, dv, blk), lambda bi, gi, i: (bi, gi // vdiv, 0, 0, 0)),
                  pl.BlockSpec((1, n_tiles + 1, blk, blk), lambda bi, gi, i: (gi // bdiv, 0, 0, 0))],
        out_specs=pl.BlockSpec((1, 1, 1, dv, blk), lambda bi, gi, i: (bi, gi, i, 0, 0)),
        out_shape=jax.ShapeDtypeStruct((b, g, nq, dv, blk), out_dtype),
        scratch_shapes=[pltpu.VMEM((1, blk), F32), pltpu.VMEM((1, blk), F32), pltpu.VMEM((dv, blk), F32)],
        compiler_params=_cparams(3), name="flash")(qt, k, vt, tiles)


def _dsa_kernel(qit_ref, w_ref, ki_ref, qt_ref, k_ref, vt_ref, tile_ref, o_ref,
                key_sc, m_sc, l_sc, acc_sc, *, i_off, n_sel, idx_bits):
    tq, tk, nh = DSA_TQ, DSA_TK, D_HEADS
    i = pl.program_id(1) + i_off
    nkb = (i * tq) // tk + 1
    krow = lax.broadcasted_iota(I32, (tk, tq), 0)
    qcol = lax.broadcasted_iota(I32, (tk, tq), 1)
    q_chunk = jnp.right_shift(i * tq + qcol, 6)

    def head(x, h):
        return x[:, h * tq:(h + 1) * tq]

    qit = qit_ref[0, 0]
    w = w_ref[0, 0]

    def score_block(j):
        lg = jnp.maximum(jnp.dot(ki_ref[0, j], qit, preferred_element_type=F32), 0.0) * w
        sc = head(lg, 0)
        for h in range(1, IDX_HEADS):
            sc = sc + head(lg, h)
        sc = jnp.where(sc == 0.0, 0.0, sc)
        bits = pltpu.bitcast(sc, I32)
        key = bits ^ (jnp.right_shift(bits, 31) & INT_MAX)
        adm = jnp.right_shift(j * tk + krow, 6) <= q_chunk
        key_sc[j] = jnp.where(adm, key, INT_MIN)

    def score_pair(jj, carry):
        score_block(2 * jj)
        score_block(2 * jj + 1)
        return carry

    def score_single(j, carry):
        score_block(j)
        return carry

    lax.fori_loop(0, nkb // 2, score_pair, 0)
    lax.fori_loop((nkb // 2) * 2, nkb, score_single, 0)

    def count(pred_fn):
        def body(j, acc):
            ind = jnp.where(pred_fn(key_sc[j], j * tk + krow), 1.0, 0.0)
            return acc + jnp.sum(ind.reshape(tk // COUNT_ROWS, COUNT_ROWS, tq), axis=0)
        acc = lax.fori_loop(0, nkb, body, jnp.zeros((COUNT_ROWS, tq), F32))
        return jnp.sum(acc, axis=0, keepdims=True)

    target = float(n_sel)
    qpos = i * tq + lax.broadcasted_iota(I32, (1, tq), 1)
    n_adm = ((jnp.right_shift(qpos, 6) + 1) * CHUNK).astype(F32)
    real = n_adm > target

    def search_cond(state):
        thr, c_thr, b = state
        settled = (c_thr == target) | jnp.logical_not(real)
        return (b < 32) & (jnp.min(jnp.where(settled, 1.0, 0.0)) < 1.0)

    def value_bit(state):
        thr, c_thr, b = state
        cand = thr + jnp.left_shift(jnp.int32(1), 31 - b)
        cnt = count(lambda k, idx: k >= cand)
        ge = cnt >= target
        return jnp.where(ge, cand, thr), jnp.where(ge, cnt, c_thr), b + 1

    thr, c_thr, _ = lax.while_loop(search_cond, value_bit,
                                   (jnp.full((1, tq), INT_MIN, I32), n_adm, jnp.int32(0)))

    def tie_search():
        need = target - count(lambda k, idx: k > thr)

        def index_bit(b, y):
            cand = y + jnp.left_shift(jnp.int32(1), idx_bits - 1 - b)
            cnt = count(lambda k, idx: (k == thr) & (idx < cand))
            return jnp.where(cnt < need, cand, y)
        return lax.fori_loop(0, idx_bits, index_bit, jnp.zeros((1, tq), I32))

    has_ties = jnp.max(jnp.where((c_thr > target) & real, 1.0, 0.0)) > 0.0
    last = lax.cond(has_ties, tie_search, lambda: jnp.full((1, tq), INT_MAX, I32))
    last = jnp.where(real, last, -1)

    qt = qt_ref[0, 0]
    n_near = 3

    def raw(j):
        return jnp.dot(k_ref[0, j], qt, preferred_element_type=F32)

    def tile_of(j):
        return jnp.minimum((i * tq - j * tk) // tq, n_near)

    def select(j, s):
        key = key_sc[j]
        sel = (key > thr) | ((key == thr) & (j * tk + krow <= last))
        selb = jnp.where(sel, 0.0, UNSELECTED)
        return jnp.concatenate([head(s, h) + selb for h in range(nh)], axis=1)

    def reset():
        l_sc[...] = jnp.zeros(l_sc.shape, F32)
        acc_sc[...] = jnp.zeros(acc_sc.shape, F32)

    def accumulate(j, p):
        l_sc[...] += jnp.sum(p, axis=0, keepdims=True)
        acc_sc[...] += jnp.dot(vt_ref[0, j], p.astype(BF16), preferred_element_type=F32)

    reset()
    m = jnp.max(raw(0) + tile_ref[tile_of(0)], axis=0, keepdims=True)
    m_far = m - tile_ref[n_near, 0:1, :]
    n_far = jnp.maximum(i - 1, 0) // 2

    def far_group(jj, carry):
        ps = [jnp.exp2(select(DSA_FAR_UNROLL * jj + u, raw(DSA_FAR_UNROLL * jj + u)) - m_far)
              for u in range(DSA_FAR_UNROLL)]
        for u in range(DSA_FAR_UNROLL):
            accumulate(DSA_FAR_UNROLL * jj + u, ps[u])
        return carry

    def far_single(j, carry):
        accumulate(j, jnp.exp2(select(j, raw(j)) - m_far))
        return carry

    def near(j, carry):
        accumulate(j, jnp.exp2(select(j, raw(j) + tile_ref[tile_of(j)]) - m))
        return carry

    n_grouped = (n_far // DSA_FAR_UNROLL) * DSA_FAR_UNROLL
    lax.fori_loop(0, n_far // DSA_FAR_UNROLL, far_group, 0)
    lax.fori_loop(n_grouped, n_far, far_single, 0)
    lax.fori_loop(n_far, nkb, near, 0)

    l = l_sc[...]
    in_range = (jnp.max(l) < SUM_MAX) & (jnp.min(l) > SUM_MIN)

    @pl.when(jnp.logical_not(in_range))
    def _():
        reset()
        m_sc[...] = jnp.full(m_sc.shape, M_INIT, F32)

        def attend(j, carry):
            s = select(j, raw(j) + tile_ref[tile_of(j)])
            m_prev = m_sc[...]
            m_new = jnp.maximum(m_prev, jnp.max(s, axis=0, keepdims=True))
            alpha = jnp.exp2(m_prev - m_new)
            p = jnp.exp2(s - m_new)
            l_sc[...] = alpha * l_sc[...] + jnp.sum(p, axis=0, keepdims=True)
            acc_sc[...] = alpha * acc_sc[...] + jnp.dot(vt_ref[0, j], p.astype(BF16), preferred_element_type=F32)
            m_sc[...] = m_new
            return carry

        lax.fori_loop(0, nkb, attend, 0)

    o_ref[0, 0] = (acc_sc[...] / l_sc[...]).astype(o_ref.dtype)


def dsa(qit, w, ki, qt, k, vt, tiles, *, i_off, n_sel, out_dtype=F32):
    b, nq, kdim, cols = qit.shape
    nkb = ki.shape[1]
    dk, dv = k.shape[-1], vt.shape[2]
    idx_bits = max(1, (nkb * DSA_TK - 1).bit_length())
    return pl.pallas_call(
        functools.partial(_dsa_kernel, i_off=i_off, n_sel=n_sel, idx_bits=idx_bits),
        grid=(b, nq),
        in_specs=[pl.BlockSpec((1, 1, kdim, cols), lambda bi, i: (bi, i, 0, 0)),
                  pl.BlockSpec((1, 1, 1, cols), lambda bi, i: (bi, i, 0, 0)),
                  pl.BlockSpec((1, nkb, DSA_TK, kdim), lambda bi, i: (bi, 0, 0, 0)),
                  pl.BlockSpec((1, 1, dk, cols), lambda bi, i: (bi, i, 0, 0)),
                  pl.BlockSpec((1, nkb, DSA_TK, dk), lambda bi, i: (bi, 0, 0, 0)),
                  pl.BlockSpec((1, nkb, dv, DSA_TK), lambda bi, i: (bi, 0, 0, 0)),
                  pl.BlockSpec((4, DSA_TK, cols), lambda bi, i: (0, 0, 0))],
        out_specs=pl.BlockSpec((1, 1, dv, cols), lambda bi, i: (bi, i, 0, 0)),
        out_shape=jax.ShapeDtypeStruct((b, nq, dv, cols), out_dtype),
        scratch_shapes=[pltpu.VMEM((nkb, DSA_TK, DSA_TQ), I32), pltpu.VMEM((1, cols), F32),
                        pltpu.VMEM((1, cols), F32), pltpu.VMEM((dv, cols), F32)],
        compiler_params=_cparams(2), name="dsa")(qit, w, ki, qt, k, vt, tiles)


_NT = (((1,), (1,)), ((), ()))


def _heads(x, n, d):
    return x.reshape(n, d, x.shape[-1])


def _proj_even_kernel(x_ref, g_ref, w1_ref, w2t_ref, w3_ref, qn_ref, wuqt_ref, kvn_ref, wkb_ref, e1_ref, e2_ref,
                      wuvt_ref, cosn_ref, sinn_ref, cost_ref, sint_ref,
                      akv_ref, ckv_ref, y1_ref, y2_ref, qta_ref, ka_ref, vta_ref, qtb_ref, kb_ref, vtb_ref):
    nh, da, rope_half = A_HEADS, A_DIM, B_ROPE // 2
    h = _rms(x_ref[...], g_ref[...]).astype(BF16)
    h1 = jnp.dot(h, w1_ref[...], preferred_element_type=F32)
    akv_ref[...] = h1[:, :2 * nh * da]
    t2 = lax.dot_general(w2t_ref[...], h, _NT, preferred_element_type=F32)
    qta_ref[0, :, 0] = _heads(t2[:nh * da], nh, da).astype(BF16)
    vta_ref[0, :, 0] = _heads(t2[nh * da:], nh, da).astype(BF16)
    k3 = jnp.dot(h, w3_ref[...], preferred_element_type=F32)
    for hd in range(nh):
        ka_ref[0, hd, 0] = k3[:, hd * LANES:hd * LANES + da].astype(BF16)

    c0 = 2 * nh * da
    cqn = _rms(h1[:, c0:c0 + Q_LORA], qn_ref[...]).astype(BF16)
    tq = lax.dot_general(wuqt_ref[...], cqn, _NT, preferred_element_type=F32)
    n_nope, n_rope = B_HEADS * B_NOPE, B_HEADS * rope_half
    x1, x2 = tq[n_nope:n_nope + n_rope], tq[n_nope + n_rope:]
    ct, st = cost_ref[0], sint_ref[0]
    y1t, y2t = x1 * ct - x2 * st, x1 * st + x2 * ct
    for hd in range(B_HEADS):
        qtb_ref[0, hd, 0] = jnp.concatenate(
            [tq[hd * B_NOPE:(hd + 1) * B_NOPE], y1t[hd * rope_half:(hd + 1) * rope_half],
             y2t[hd * rope_half:(hd + 1) * rope_half]], axis=0).astype(BF16)

    c1 = c0 + Q_LORA
    ckv = _rms(h1[:, c1:c1 + KV_LORA], kvn_ref[...])
    ckv_ref[...] = ckv
    ckvb = ckv.astype(BF16)
    k1, k2 = h1[:, c1 + KV_LORA:c1 + KV_LORA + LANES], h1[:, c1 + KV_LORA + LANES:]
    cn, sn = cosn_ref[...], sinn_ref[...]
    y1, y2 = k1 * cn - k2 * sn, k1 * sn + k2 * cn
    y1_ref[...] = y1
    y2_ref[...] = y2
    kb = (jnp.dot(ckvb, wkb_ref[...], preferred_element_type=F32)
          + jnp.dot(y1.astype(BF16), e1_ref[...], preferred_element_type=F32)
          + jnp.dot(y2.astype(BF16), e2_ref[...], preferred_element_type=F32))
    for hd in range(B_HEADS):
        kb_ref[0, hd, 0] = kb[:, hd * LANES:hd * LANES + B_NOPE + B_ROPE].astype(BF16)
    tv = lax.dot_general(wuvt_ref[...], ckvb, _NT, preferred_element_type=F32)
    vtb_ref[0, :, 0] = _heads(tv, B_HEADS, B_V).astype(BF16)


def proj_even(x, g, w_in, q_norm, w_uq, kv_norm, w_ukv, cos, sin, blk):
    p, d = x.shape
    nq = p // blk
    nh, da, half = A_HEADS, A_DIM, B_ROPE // 2
    s_a = A_DIM ** -0.5 * LOG2E
    s_b = (B_NOPE + B_ROPE) ** -0.5 * LOG2E
    aq_w, ak_w, av_w, cq_w, ckv_w, kr_w = _split(w_in, EVEN_SPLITS)

    def lane_pad(w, n=LANES):
        return jnp.pad(w, ((0, 0), (0, n - w.shape[1])))

    def head_chunks(w, dh):
        k = w.shape[0]
        return jnp.pad(w.reshape(k, -1, dh), ((0, 0), (0, 0), (0, LANES - dh))).reshape(k, -1)

    w1 = jnp.concatenate([ak_w, av_w, cq_w, ckv_w, lane_pad(kr_w[:, :half]), lane_pad(kr_w[:, half:])], axis=1).astype(BF16)
    w2t = jnp.concatenate([aq_w * s_a, av_w], axis=1).T.astype(BF16)
    w3 = head_chunks(ak_w, da).astype(BF16)
    uq = (w_uq * s_b).reshape(Q_LORA, B_HEADS, B_NOPE + B_ROPE)
    wuqt = jnp.concatenate([uq[:, :, :B_NOPE].reshape(Q_LORA, -1), uq[:, :, B_NOPE:B_NOPE + half].reshape(Q_LORA, -1),
                            uq[:, :, B_NOPE + half:].reshape(Q_LORA, -1)], axis=1).T.astype(BF16)
    ukv = w_ukv.reshape(KV_LORA, B_HEADS, B_NOPE + B_V)
    wkb = head_chunks(ukv[:, :, :B_NOPE].reshape(KV_LORA, -1), B_NOPE).astype(BF16)
    wuvt = ukv[:, :, B_NOPE:].reshape(KV_LORA, -1).T.astype(BF16)
    eye = np.zeros((2, LANES, B_HEADS, LANES), np.float32)
    for r in range(half):
        eye[0, r, :, B_NOPE + r] = 1.0
        eye[1, r, :, B_NOPE + half + r] = 1.0
    e1, e2 = (jnp.asarray(e.reshape(LANES, B_HEADS * LANES), BF16) for e in eye)
    cosn, sinn = lane_pad(cos), lane_pad(sin)
    cost = jnp.tile(cos.reshape(nq, blk, half).transpose(0, 2, 1), (1, B_HEADS, 1))
    sint = jnp.tile(sin.reshape(nq, blk, half).transpose(0, 2, 1), (1, B_HEADS, 1))

    def full(a):
        return pl.BlockSpec(a.shape, lambda i: (0,) * a.ndim)

    def rows(n):
        return pl.BlockSpec((blk, n), lambda i: (i, 0))

    def tile3(a):
        return pl.BlockSpec((1,) + a.shape[1:], lambda i: (i, 0, 0))

    def per_block(g_, a, b_):
        return (jax.ShapeDtypeStruct((1, g_, nq, a, b_), BF16),
                pl.BlockSpec((1, g_, 1, a, b_), lambda i: (0, 0, i, 0, 0)))

    consts = [g.reshape(1, d).astype(F32), w1, w2t, w3, q_norm.reshape(1, -1).astype(F32), wuqt,
              kv_norm.reshape(1, -1).astype(F32), wkb, e1, e2, wuvt]
    blocks = [per_block(nh, da, blk), per_block(nh, blk, da), per_block(nh, da, blk),
              per_block(B_HEADS, B_NOPE + B_ROPE, blk), per_block(B_HEADS, blk, B_NOPE + B_ROPE),
              per_block(B_HEADS, B_V, blk)]
    out_shape = [jax.ShapeDtypeStruct((p, 2 * nh * da), F32), jax.ShapeDtypeStruct((p, KV_LORA), F32),
                 jax.ShapeDtypeStruct((p, LANES), F32), jax.ShapeDtypeStruct((p, LANES), F32)] + [s for s, _ in blocks]
    out_specs = [rows(2 * nh * da), rows(KV_LORA), rows(LANES), rows(LANES)] + [s for _, s in blocks]
    return pl.pallas_call(
        _proj_even_kernel, grid=(nq,),
        in_specs=[rows(d)] + [full(a) for a in consts] + [rows(LANES), rows(LANES), tile3(cost), tile3(sint)],
        out_specs=out_specs, out_shape=out_shape,
        compiler_params=_cparams(1), name="proj_even")(x, *consts, cosn, sinn, cost, sint)


def _proj_odd_kernel(x_ref, g_ref, w1_ref, w2t_ref, w3_ref,
                     st_ref, cqt_ref, kc_ref, cvt_ref, qit_ref, wq_ref, qt_ref):
    nmap, width = 2 * C_HEADS, C_HEADS * 2 * C_DIM
    h = _rms(x_ref[...], g_ref[...]).astype(BF16)
    st_ref[...] = jnp.dot(h, w1_ref[...], preferred_element_type=F32)
    t2 = lax.dot_general(w2t_ref[...], h, _NT, preferred_element_type=F32)
    cqt_ref[0, :, 0] = _heads(t2[:width], nmap, C_DIM).astype(BF16)
    cvt_ref[0, :, 0] = _heads(t2[width:2 * width], C_HEADS, 2 * C_DIM).astype(BF16)
    k3 = jnp.dot(h, w3_ref[...], preferred_element_type=F32)
    for m in range(nmap):
        kc_ref[0, m, 0] = k3[:, m * LANES:m * LANES + C_DIM].astype(BF16)

    dqt = t2[2 * width:2 * width + D_HEADS * D_DIM]
    dqit = t2[2 * width + D_HEADS * D_DIM:2 * width + D_HEADS * (D_DIM + IDX_DIM)]
    dwt = t2[2 * width + D_HEADS * (D_DIM + IDX_DIM):]

    def stack(x, rows_per_head, n_rows, b):
        cols = slice(b * DSA_TQ, (b + 1) * DSA_TQ)
        return jnp.concatenate([x[hd * rows_per_head:hd * rows_per_head + n_rows, cols] for hd in range(D_HEADS)], axis=1)

    for b in range(x_ref.shape[0] // DSA_TQ):
        qt_ref[0, b] = stack(dqt, D_DIM, D_DIM, b).astype(BF16)
        qi = stack(dqit, IDX_DIM, IDX_DIM, b)
        hi = qi.astype(BF16)
        lo = (qi - hi.astype(F32)).astype(BF16)
        qit_ref[0, b] = jnp.concatenate([hi, lo, hi], axis=0)
        wq_ref[0, b] = stack(dwt, SUBLANES, 1, b)


def proj_odd(x, g, w_in, blk):
    p, d = x.shape
    nq, nq_dsa = p // blk, p // DSA_TQ
    s_c = C_DIM ** -0.5 * LOG2E
    s_d = D_DIM ** -0.5 * LOG2E
    s_w = (IDX_HEADS ** -0.5) * (IDX_DIM ** -0.5)
    cq_w, ck_w, cv_w, dq_w, dk_w, dv_w, dqi_w, dki_w, dw_w = _split(w_in, ODD_SPLITS)

    def lane_pad(w):
        return jnp.pad(w, ((0, 0), (0, LANES - w.shape[1])))

    w1 = jnp.concatenate([ck_w, cv_w, lane_pad(dk_w), lane_pad(dv_w), lane_pad(dki_w)], axis=1).astype(BF16)
    dw_rows = jnp.pad((dw_w * s_w)[:, :, None], ((0, 0), (0, 0), (0, SUBLANES - 1))).reshape(d, -1)
    w2t = jnp.concatenate([cq_w * s_c, cv_w, dq_w * s_d, dqi_w, dw_rows], axis=1).T.astype(BF16)
    w3 = jnp.pad(ck_w.reshape(d, -1, C_DIM), ((0, 0), (0, 0), (0, LANES - C_DIM))).reshape(d, -1).astype(BF16)
    consts = [g.reshape(1, d).astype(F32), w1, w2t, w3]
    cols = D_HEADS * DSA_TQ
    per = blk // DSA_TQ

    def per_block(g_, a, b_):
        return (jax.ShapeDtypeStruct((1, g_, nq, a, b_), BF16),
                pl.BlockSpec((1, g_, 1, a, b_), lambda i: (0, 0, i, 0, 0)))

    def per_dsa(a, dtype):
        return (jax.ShapeDtypeStruct((1, nq_dsa, a, cols), dtype), pl.BlockSpec((1, per, a, cols), lambda i: (0, i, 0, 0)))

    blocks = [per_block(2 * C_HEADS, C_DIM, blk), per_block(2 * C_HEADS, blk, C_DIM), per_block(C_HEADS, 2 * C_DIM, blk),
              per_dsa(3 * IDX_DIM, BF16), per_dsa(1, F32), per_dsa(D_DIM, BF16)]
    return pl.pallas_call(
        _proj_odd_kernel, grid=(nq,),
        in_specs=[pl.BlockSpec((blk, d), lambda i: (i, 0))] + [pl.BlockSpec(a.shape, lambda i: (0, 0)) for a in consts],
        out_specs=[pl.BlockSpec((blk, w1.shape[1]), lambda i: (i, 0))] + [s for _, s in blocks],
        out_shape=[jax.ShapeDtypeStruct((p, w1.shape[1]), F32)] + [s for s, _ in blocks],
        compiler_params=_cparams(1), name="proj_odd")(x, *consts)


def _out_odd_kernel(x_ref, c_ref, d_ref, lam_ref, sub_ref, w_ref, o_ref, *, post_scale):
    blk = x_ref.shape[0]
    lam, sub = lam_ref[...], sub_ref[...]
    parts = []
    for hd in range(C_HEADS):
        c = c_ref[0, 2 * hd, 0] - lam * c_ref[0, 2 * hd + 1, 0]
        parts.append(c * lax.rsqrt(jnp.mean(c * c, axis=0, keepdims=True) + EPS) * sub * post_scale)
    for hd in range(D_HEADS):
        parts.append(jnp.concatenate([d_ref[0, b][:, hd * DSA_TQ:(hd + 1) * DSA_TQ]
                                      for b in range(blk // DSA_TQ)], axis=1).astype(F32))
    mix = jnp.concatenate(parts, axis=0).T.astype(BF16)
    o_ref[...] = x_ref[...] + jnp.dot(mix, w_ref[...], preferred_element_type=F32)


def out_odd(x, c_t, d_t, lam, subln, post_scale, w_out):
    p, d = x.shape
    _, nmap, nq, dc, blk = c_t.shape
    per = blk // DSA_TQ
    return pl.pallas_call(
        functools.partial(_out_odd_kernel, post_scale=post_scale), grid=(nq,),
        in_specs=[pl.BlockSpec((blk, d), lambda i: (i, 0)),
                  pl.BlockSpec((1, nmap, 1, dc, blk), lambda i: (0, 0, i, 0, 0)),
                  pl.BlockSpec((1, per) + d_t.shape[2:], lambda i: (0, i, 0, 0)),
                  pl.BlockSpec((1, blk), lambda i: (0, 0)),
                  pl.BlockSpec((dc, 1), lambda i: (0, 0)),
                  pl.BlockSpec(w_out.shape, lambda i: (0, 0))],
        out_specs=pl.BlockSpec((blk, d), lambda i: (i, 0)),
        out_shape=jax.ShapeDtypeStruct((p, d), F32),
        compiler_params=_cparams(1), name="out_odd")(
            x, c_t, d_t, jnp.full((1, blk), lam, F32), subln.reshape(dc, 1).astype(F32), w_out)


def _out_even_kernel(x_ref, a_ref, b_ref, w_ref, o_ref):
    blk = x_ref.shape[0]
    mix_t = jnp.concatenate([a_ref[0, :, 0].reshape(-1, blk), b_ref[0, :, 0].reshape(-1, blk)], axis=0)
    mix = mix_t.astype(F32).T.astype(BF16)
    o_ref[...] = x_ref[...] + jnp.dot(mix, w_ref[...], preferred_element_type=F32)


def out_even(x, a_t, b_t, w_out):
    p, d = x.shape
    _, g_, nq, dv, blk = a_t.shape
    blk_spec = pl.BlockSpec((1, g_, 1, dv, blk), lambda i: (0, 0, i, 0, 0))
    return pl.pallas_call(
        _out_even_kernel, grid=(nq,),
        in_specs=[pl.BlockSpec((blk, d), lambda i: (i, 0)), blk_spec, blk_spec,
                  pl.BlockSpec(w_out.shape, lambda i: (0, 0))],
        out_specs=pl.BlockSpec((blk, d), lambda i: (i, 0)),
        out_shape=jax.ShapeDtypeStruct((p, d), F32),
        compiler_params=_cparams(1), name="out_even")(x, a_t, b_t, w_out)


def _rope_tables(pos):
    half = B_ROPE // 2
    freqs = jnp.power(jnp.float32(ROPE_THETA), -jnp.arange(half, dtype=F32) / half)
    ang = pos.astype(F32)[:, None] * freqs[None, :]
    return jnp.cos(ang), jnp.sin(ang)


def _rope(x, cos, sin):
    half = x.shape[-1] // 2
    shape = (x.shape[0],) + (1,) * (x.ndim - 2) + (half,)
    c, s = cos.reshape(shape), sin.reshape(shape)
    x1, x2 = x[..., :half], x[..., half:]
    return jnp.concatenate([x1 * c - x2 * s, x1 * s + x2 * c], axis=-1)


def _t5_bucket(rel):
    nb = T5_BUCKETS // 2
    max_exact = nb // 2
    ret = jnp.where(rel > 0, nb, 0)
    n = jnp.abs(rel)
    nf = jnp.maximum(n, 1).astype(F32)
    large = max_exact + (jnp.log(nf / max_exact) / math.log(T5_MAX_DIST / max_exact)
                         * (nb - max_exact)).astype(I32)
    large = jnp.minimum(large, nb - 1)
    return ret + jnp.where(n < max_exact, n, large)


def _chunk_diff(tq, tk, n_tiles):
    r = np.arange(tq)[None, None, :]
    c = np.arange(tk)[None, :, None]
    t = np.arange(n_tiles)[:, None, None]
    return c // CHUNK - r // CHUNK - t * (tq // CHUNK)


def _strip_distances(tq, tk, n_tiles):
    y = np.arange(tq + tk)[None, :]
    t = np.arange(n_tiles)[:, None]
    return np.where(y < tq, -y, tq + tk - y) - t * tq


def _toeplitz(strip, tq, tk):
    n = tq + tk
    lead = strip.shape[:-1]
    rows = jnp.broadcast_to(strip[..., None, :], lead + (tk, n)).reshape(lead + (tk * n,))
    return rows[..., :tk * (n - 1)].reshape(lead + (tk, n - 1))[..., :tq]


def band_tiles(relbias, blk):
    n_tiles = (A_LEFT_CHUNKS * CHUNK) // blk + 1
    dchunk = _chunk_diff(blk, blk, n_tiles)
    adm = (dchunk <= 0) & (-dchunk <= A_LEFT_CHUNKS)
    idx = np.clip(-_strip_distances(blk, blk, n_tiles), -A_MAX_REL, A_MAX_REL) + A_MAX_REL
    bias = _toeplitz(relbias.astype(F32)[:, idx] * LOG2E, blk, blk)
    tiles = jnp.where(adm[None], bias, MASKED)
    far = jnp.full((relbias.shape[0], 1, blk, blk), MASKED, F32)
    return jnp.concatenate([tiles, far], axis=1), n_tiles


def causal_tiles(blk):
    t0 = np.where(_chunk_diff(blk, blk, 1) <= 0, 0.0, MASKED).astype(np.float32)
    return jnp.asarray(np.concatenate([t0, np.zeros_like(t0)], axis=0)[None])


def t5_tiles(tab, tq, tk, n_tiles):
    assert n_tiles * tq - tk + 1 >= T5_MAX_DIST
    tab = tab.astype(F32) * LOG2E
    strip = jnp.moveaxis(tab[_t5_bucket(jnp.asarray(_strip_distances(tq, tk, n_tiles), I32))], -1, 0)
    tiles = jnp.where((_chunk_diff(tq, tk, n_tiles) <= 0)[None], _toeplitz(strip, tq, tk), MASKED)
    far = tab[_t5_bucket(jnp.asarray([-(n_tiles * tq + tk)], I32))[0]]
    far = jnp.broadcast_to(far[:, None, None, None], (tab.shape[1], 1, tk, tq))
    return jnp.concatenate([tiles, far], axis=1)


def _row_blocks(x, blk):
    b, t, g, d = x.shape
    return jnp.transpose(x.reshape(b, t // blk, blk, g, d), (0, 3, 1, 2, 4)).astype(BF16)


def _col_blocks(x, blk):
    b, t, g, d = x.shape
    return jnp.transpose(x.reshape(b, t // blk, blk, g, d), (0, 3, 1, 4, 2)).astype(BF16)


def _from_col_blocks(o):
    b, g, nq, d, blk = o.shape
    return jnp.transpose(o, (0, 2, 4, 1, 3)).reshape(b, nq * blk, g * d)


def _pad_rows(x, n, front=0):
    back = n - x.shape[1] - front
    return jnp.pad(x, ((0, 0), (front, back)) + ((0, 0),) * (x.ndim - 2))


def _split(h, sizes):
    out, o = [], 0
    for s in sizes:
        out.append(h[:, o:o + s])
        o += s
    return out


def _split3(x):
    hi = x.astype(BF16)
    lo = (x - hi.astype(F32)).astype(BF16)
    return hi, lo


def _dsa_keys(dk_all, dv_all, dki_all):
    b, nkb = dk_all.shape[0], dk_all.shape[1] // DSA_TK
    kh, kl = _split3(dki_all)
    ki = jnp.concatenate([kh, kh, kl], axis=-1).reshape(b, nkb, DSA_TK, 3 * IDX_DIM)
    k = dk_all.astype(BF16).reshape(b, nkb, DSA_TK, D_DIM)
    vt = jnp.transpose(dv_all.astype(BF16).reshape(b, nkb, DSA_TK, D_DIM), (0, 1, 3, 2))
    return ki, k, vt


def _dsa_call(dq, dqi, dw, dk_all, dv_all, dki_all, tiles, i_off, n_sel):
    b, tq_all = dq.shape[:2]
    nq = tq_all // DSA_TQ

    def stack_t(x):
        d = x.shape[-1]
        return jnp.transpose(x.reshape(b, nq, DSA_TQ, D_HEADS, d), (0, 1, 4, 3, 2)).reshape(b, nq, d, D_HEADS * DSA_TQ)

    qh, ql = _split3(dqi)
    qit = stack_t(jnp.concatenate([qh, ql, qh], axis=-1))
    w = stack_t(dw[..., None])
    qt = stack_t((dq * (D_DIM ** -0.5 * LOG2E)).astype(BF16))
    ki, k, vt = _dsa_keys(dk_all, dv_all, dki_all)
    o = dsa(qit, w, ki, qt, k, vt, tiles, i_off=i_off, n_sel=n_sel)
    o = jnp.transpose(o.reshape(b, nq, D_DIM, D_HEADS, DSA_TQ), (0, 1, 4, 3, 2))
    return o.reshape(b, tq_all, D_HEADS * D_DIM)


P_BLK = 512
S_BLK = 128


def kernel(x_prompt, x_sample, cache_a_k, cache_a_v, cache_b_ckv, cache_b_krope, cache_c_k, cache_c_v, cache_d_k, cache_d_v, cache_d_kidx, t5_table, norm_attn, norm_ffn, final_norm, even_w_in, even_w_out, a_relbias, b_q_norm, b_kv_norm, b_w_uq, b_w_ukv, odd_w_in, odd_w_out, c_lambda_q1, c_lambda_k1, c_lambda_q2, c_lambda_k2, c_subln, ffn_w_gate, ffn_w_up, ffn_w_down):
    pb, pt, d = x_prompt.shape
    sb, st, _ = x_sample.shape
    assert pb == 1 and st == CHUNK
    past = cache_b_ckv.shape[2]
    n_a = cache_a_k.shape[2]
    depth = norm_attn.shape[0]
    n_p, n_s = pb * pt, sb * st
    s_len = past + st
    s_pad = -(-s_len // DSA_TK) * DSA_TK
    s_i = past // S_BLK
    s_i_dsa = past // DSA_TQ
    assert past % S_BLK == 0 and s_pad % S_BLK == 0 and past >= n_a
    p_blk = min(P_BLK, pt)

    cos_p, sin_p = _rope_tables(jnp.arange(pt))
    cos_s, sin_s = _rope_tables(jnp.tile(past + jnp.arange(st), sb))

    def rows_p(a):
        return a.reshape(pb, pt, *a.shape[1:])

    def rows_s(a):
        return a.reshape(sb, st, *a.shape[1:])

    def pad_cols(w):
        n = -(-w.shape[1] // LANES) * LANES
        return jnp.pad(w, ((0, 0), (0, n - w.shape[1]))).astype(BF16)

    def sample_q(a):
        return _pad_rows(a, S_BLK)

    def sample_keys(cache, new, front=0):
        return _pad_rows(jnp.concatenate([cache, new], axis=1), s_pad, front)

    def attend(q, k, v, tiles, blk, n_tiles, has_far, i_off=0):
        o = flash(_col_blocks(q, blk), _row_blocks(k, blk), _col_blocks(v, blk), tiles,
                  n_tiles=n_tiles, has_far=has_far, i_off=i_off)
        return _from_col_blocks(o)

    t5_c, t5_d = t5_table[:, :C_HEADS], t5_table[:, C_HEADS:]
    ctiles_p = causal_tiles(p_blk)
    ctiles_s = causal_tiles(S_BLK)
    c_tiles_p = t5_tiles(t5_c, p_blk, p_blk, 2)
    c_tiles_s = t5_tiles(t5_c, S_BLK, S_BLK, 2)
    d_tiles = t5_tiles(t5_d, DSA_TQ, DSA_TK, 3)
    d_tiles = jnp.transpose(d_tiles, (1, 2, 0, 3)).reshape(4, DSA_TK, D_HEADS * DSA_TQ)
    n_sel_p = min(TOPK_MAX, pt // 4)
    n_sel_s = min(TOPK_MAX, s_len // 4)

    outs_even = {k: [] for k in ("a_k_p", "a_k_s", "a_v_p", "a_v_s", "ckv_p", "ckv_s", "kr_p", "kr_s")}
    outs_odd = {k: [] for k in ("c_k_p", "c_k_s", "c_v_p", "c_v_s", "d_k_p", "d_k_s", "d_v_p", "d_v_s", "d_ki_p", "d_ki_s")}

    def mla_keys(kn, krope):
        return jnp.concatenate([kn, jnp.broadcast_to(krope[:, :, None, :], kn.shape[:3] + (B_ROPE,))], axis=-1)

    def even_prompt(x, l, i):
        a_tiles, nt = band_tiles(a_relbias[i], p_blk)
        akv, ckv, y1, y2, qta, ka, vta, qtb, kb, vtb = proj_even(
            x, norm_attn[l], even_w_in[i], b_q_norm[i], b_w_uq[i], b_kv_norm[i], b_w_ukv[i], cos_p, sin_p, p_blk)
        a_t = flash(qta, ka, vta, a_tiles, n_tiles=nt, has_far=False, out_dtype=BF16)
        b_t = flash(qtb, kb, vtb, ctiles_p, n_tiles=1, has_far=True, out_dtype=BF16)
        half = B_ROPE // 2
        hd = A_HEADS * A_DIM
        outs_even["a_k_p"].append(akv[pt - n_a:, :hd].reshape(pb, n_a, A_HEADS, A_DIM))
        outs_even["a_v_p"].append(akv[pt - n_a:, hd:].reshape(pb, n_a, A_HEADS, A_DIM))
        outs_even["ckv_p"].append(rows_p(ckv))
        outs_even["kr_p"].append(rows_p(jnp.concatenate([y1[:, :half], y2[:, :half]], axis=-1)))
        return out_even(x, a_t, b_t, even_w_out[i].astype(BF16))

    def even_sample(x, l, i):
        h = dense(x, pad_cols(even_w_in[i]), g=norm_attn[l])
        aq, ak, av, cq, ckv_raw, kr_raw = _split(h, EVEN_SPLITS)
        qb = dense(cq, b_w_uq[i].astype(BF16), g=b_q_norm[i]).reshape(-1, B_HEADS, B_NOPE + B_ROPE)
        kv_new, ckv = dense(ckv_raw, b_w_ukv[i].astype(BF16), g=b_kv_norm[i], emit_h=True)
        kr = _rope(kr_raw, cos_s, sin_s)
        q_mla = jnp.concatenate([qb[..., :B_NOPE], _rope(qb[..., B_NOPE:], cos_s, sin_s)], axis=-1)
        q_mla = q_mla * ((B_NOPE + B_ROPE) ** -0.5 * LOG2E)
        kv_s = rows_s(kv_new.reshape(-1, B_HEADS, B_NOPE + B_V))
        aq = (aq * (A_DIM ** -0.5 * LOG2E)).reshape(-1, A_HEADS, A_DIM)
        ak = ak.reshape(-1, A_HEADS, A_DIM)
        av = av.reshape(-1, A_HEADS, A_DIM)
        a_tiles, nt = band_tiles(a_relbias[i], S_BLK)
        ak_all = jnp.concatenate([cache_a_k[i], rows_s(ak)], axis=1)
        av_all = jnp.concatenate([cache_a_v[i], rows_s(av)], axis=1)
        a_out = attend(sample_q(rows_s(aq)), _pad_rows(ak_all, s_pad, past - n_a),
                       _pad_rows(av_all, s_pad, past - n_a), a_tiles, S_BLK, nt, False, s_i)[:, :st]
        kv_c = dense(cache_b_ckv[i].reshape(sb * past, KV_LORA), b_w_ukv[i].astype(BF16))
        kv_c = kv_c.reshape(sb, past, B_HEADS, B_NOPE + B_V)
        kn_all = sample_keys(kv_c[..., :B_NOPE], kv_s[..., :B_NOPE])
        v_all = sample_keys(kv_c[..., B_NOPE:], kv_s[..., B_NOPE:])
        kr_all = sample_keys(cache_b_krope[i], rows_s(kr))
        b_out = attend(sample_q(rows_s(q_mla)), mla_keys(kn_all, kr_all), v_all, ctiles_s, S_BLK, 1, True, s_i)[:, :st]
        mix = jnp.concatenate([a_out.reshape(n_s, -1), b_out.reshape(n_s, -1)], axis=-1)
        outs_even["a_k_s"].append(ak_all[:, -n_a:])
        outs_even["a_v_s"].append(av_all[:, -n_a:])
        outs_even["ckv_s"].append(rows_s(ckv))
        outs_even["kr_s"].append(rows_s(kr))
        return dense(mix.astype(BF16), even_w_out[i].astype(BF16), res=x)

    def diff_lambda(l, i):
        lam_init = 0.8 - 0.6 * math.exp(-0.3 * l)
        lam = (jnp.exp(jnp.sum(c_lambda_q1[i].astype(F32) * c_lambda_k1[i].astype(F32)))
               - jnp.exp(jnp.sum(c_lambda_q2[i].astype(F32) * c_lambda_k2[i].astype(F32))) + lam_init)
        return lam_init, lam

    def odd_prompt(x, l, i):
        lam_init, lam = diff_lambda(l, i)
        st_, cqt, kc, cvt, qit, wq, qt = proj_odd(x, norm_attn[l], odd_w_in[i], p_blk)
        wc = C_HEADS * 2 * C_DIM
        ck, cv = st_[:, :wc], st_[:, wc:2 * wc]
        dk, dv, dki = (st_[:, 2 * wc + n * LANES:2 * wc + n * LANES + D_DIM] for n in range(3))
        c_t = flash(cqt, kc, cvt, c_tiles_p, n_tiles=2, has_far=True)
        ki, k, vt = _dsa_keys(dk[None], dv[None], dki[None])
        d_t = dsa(qit, wq, ki, qt, k, vt, d_tiles, i_off=0, n_sel=n_sel_p, out_dtype=BF16)
        for name, arr, shp in (("c_k", ck, (C_HEADS, 2 * C_DIM)), ("c_v", cv, (C_HEADS, 2 * C_DIM)),
                               ("d_k", dk, (D_DIM,)), ("d_v", dv, (D_DIM,)), ("d_ki", dki, (IDX_DIM,))):
            outs_odd[name + "_p"].append(arr.reshape(pb, pt, *shp))
        return out_odd(x, c_t, d_t, lam, c_subln[i], 1.0 - lam_init, odd_w_out[i].astype(BF16))

    def odd_sample(x, l, i):
        lam_init, lam = diff_lambda(l, i)
        h = dense(x, pad_cols(odd_w_in[i]), g=norm_attn[l])
        cq, ck, cv, dq, dk, dv, dqi, dki, dw = _split(h, ODD_SPLITS)
        cq = (cq * (C_DIM ** -0.5 * LOG2E)).reshape(-1, 2 * C_HEADS, C_DIM)
        ck3 = ck.reshape(-1, 2 * C_HEADS, C_DIM)
        cv3 = cv.reshape(-1, C_HEADS, 2 * C_DIM)
        dq = dq.reshape(-1, D_HEADS, D_DIM)
        dqi = dqi.reshape(-1, IDX_HEADS, IDX_DIM)
        dw = dw * ((IDX_HEADS ** -0.5) * (IDX_DIM ** -0.5))

        def diff_combine(o):
            b_, t_, _ = o.shape
            o = o.reshape(b_, t_, C_HEADS, 2, 2 * C_DIM)
            c = o[:, :, :, 0] - lam * o[:, :, :, 1]
            return (_rms(c, c_subln[i].astype(F32)) * (1.0 - lam_init)).reshape(b_, t_, -1)

        rows, tag, nb_, nt_ = rows_s, "_s", sb, st
        ck_all = sample_keys(cache_c_k[i].reshape(sb, past, 2 * C_HEADS, C_DIM), rows(ck3))
        cv_all = sample_keys(cache_c_v[i], rows(cv3))
        c_out = attend(sample_q(rows(cq)), ck_all, cv_all, c_tiles_s, S_BLK, 2, True, s_i)[:, :st]
        d_out = _dsa_call(_pad_rows(rows(dq), DSA_TQ), _pad_rows(rows(dqi), DSA_TQ), _pad_rows(rows(dw), DSA_TQ),
                          sample_keys(cache_d_k[i], rows(dk)), sample_keys(cache_d_v[i], rows(dv)),
                          sample_keys(cache_d_kidx[i], rows(dki)), d_tiles, s_i_dsa, n_sel_s)[:, :st]
        mix = jnp.concatenate([diff_combine(c_out).reshape(nb_ * nt_, -1), d_out.reshape(nb_ * nt_, -1)], axis=-1)
        for name, arr, shp in (("c_k", ck, (C_HEADS, 2 * C_DIM)), ("c_v", cv, (C_HEADS, 2 * C_DIM)),
                               ("d_k", dk, (D_DIM,)), ("d_v", dv, (D_DIM,)), ("d_ki", dki, (IDX_DIM,))):
            outs_odd[name + tag].append(rows(arr).reshape(nb_, nt_, *shp))
        return dense(mix.astype(BF16), odd_w_out[i].astype(BF16), res=x)

    xp, xs = x_prompt.reshape(n_p, d), x_sample.reshape(n_s, d)
    for l in range(depth):
        i = l // 2
        if l % 2 == 0:
            xp, xs = even_prompt(xp, l, i), even_sample(xs, l, i)
        else:
            xp, xs = odd_prompt(xp, l, i), odd_sample(xs, l, i)
        wg, wu, wd = ffn_w_gate[l].astype(BF16), ffn_w_up[l].astype(BF16), ffn_w_down[l].astype(BF16)
        fg = final_norm if l == depth - 1 else None
        xp, xs = ffn(xp, norm_ffn[l], wg, wu, wd, final_g=fg), ffn(xs, norm_ffn[l], wg, wu, wd, final_g=fg)

    se = {k: jnp.stack(v, axis=0) for k, v in outs_even.items()}
    so = {k: jnp.stack(v, axis=0) for k, v in outs_odd.items()}
    return (xp.reshape(pb, pt, d), xs.reshape(sb, st, d), se["a_k_p"], se["a_k_s"], se["a_v_p"], se["a_v_s"],
            se["ckv_p"], se["ckv_s"], se["kr_p"], se["kr_s"],
            so["c_k_p"], so["c_k_s"], so["c_v_p"], so["c_v_s"],
            so["d_k_p"], so["d_k_s"], so["d_v_p"], so["d_v_s"], so["d_ki_p"], so["d_ki_s"])
```

```python
import functools
import math

import numpy as np
import jax
import jax.numpy as jnp
from jax import lax
from jax.experimental import pallas as pl
from jax.experimental.pallas import tpu as pltpu

F32, BF16, I32 = jnp.float32, jnp.bfloat16, jnp.int32

D_MODEL = 1024
CHUNK = 64
EPS = 1e-6
A_HEADS, A_DIM, A_LEFT_CHUNKS, A_MAX_REL = 8, 64, 8, 128
B_HEADS, B_NOPE, B_ROPE, B_V = 8, 64, 32, 64
Q_LORA, KV_LORA = 256, 128
ROPE_THETA = 10000.0
C_HEADS, C_DIM = 4, 64
D_HEADS, D_DIM = 8, 64
IDX_HEADS, IDX_DIM = 8, 64
TOPK_MAX = 256
T5_BUCKETS, T5_MAX_DIST = 32, 128
EVEN_SPLITS = (512, 512, 512, Q_LORA, KV_LORA, B_ROPE)
ODD_SPLITS = (512, 512, 512, 512, D_DIM, D_DIM, 512, IDX_DIM, IDX_HEADS)

LANES = 128
SUBLANES = 8
VMEM_LIMIT = 56 * 1024 * 1024
MASKED = -1e30
UNSELECTED = -2e30
M_INIT = -1e30
SUM_MAX, SUM_MIN = 1e18, 1e-18
LOG2E = math.log2(math.e)
INT_MIN = -(2 ** 31)
INT_MAX = 2 ** 31 - 1

DSA_TQ, DSA_TK = 128, 256
COUNT_ROWS = 8 * SUBLANES
FAR_UNROLL = 4
DSA_FAR_UNROLL = 4
DSA_SCORE_UNROLL = 4


def _cparams(n_axes):
    return pltpu.CompilerParams(dimension_semantics=("arbitrary",) * n_axes, vmem_limit_bytes=VMEM_LIMIT)


def _rms(x, g):
    return x * lax.rsqrt(jnp.mean(x * x, axis=-1, keepdims=True) + EPS) * g


def _dense_kernel(*refs, norm, res, emit_h):
    it = iter(refs)
    x_ref = next(it)
    g_ref = next(it) if norm else None
    w_ref = next(it)
    r_ref = next(it) if res else None
    o_ref = next(it)
    h_ref = next(it) if emit_h else None
    x = x_ref[...]
    if norm:
        x = _rms(x.astype(F32), g_ref[...])
        if emit_h:
            h_ref[...] = x
    acc = jnp.dot(x.astype(BF16), w_ref[...], preferred_element_type=F32)
    if res:
        acc = acc + r_ref[...]
    o_ref[...] = acc


def dense(x, w, g=None, res=None, emit_h=False, tm=512):
    m, k = x.shape
    n = w.shape[1]
    assert m % tm == 0 and n % LANES == 0
    norm = g is not None
    args, specs = [x], [pl.BlockSpec((tm, k), lambda i: (i, 0))]
    if norm:
        args.append(g.reshape(1, k).astype(F32))
        specs.append(pl.BlockSpec((1, k), lambda i: (0, 0)))
    args.append(w)
    specs.append(pl.BlockSpec((k, n), lambda i: (0, 0)))
    if res is not None:
        args.append(res)
        specs.append(pl.BlockSpec((tm, n), lambda i: (i, 0)))
    out_shape = [jax.ShapeDtypeStruct((m, n), F32)]
    out_specs = [pl.BlockSpec((tm, n), lambda i: (i, 0))]
    if emit_h:
        out_shape.append(jax.ShapeDtypeStruct((m, k), F32))
        out_specs.append(pl.BlockSpec((tm, k), lambda i: (i, 0)))
    outs = pl.pallas_call(
        functools.partial(_dense_kernel, norm=norm, res=res is not None, emit_h=emit_h),
        grid=(m // tm,), in_specs=specs, out_specs=out_specs, out_shape=out_shape,
        compiler_params=_cparams(1), name="dense")(*args)
    return outs if emit_h else outs[0]


def _ffn_kernel(x_ref, g_ref, wg_ref, wu_ref, wd_ref, fg_ref, o_ref, h_sc, acc_sc, *, final):
    f = pl.program_id(1)

    @pl.when(f == 0)
    def _():
        x = x_ref[...]
        h_sc[...] = _rms(x, g_ref[...]).astype(BF16)
        acc_sc[...] = x

    h = h_sc[...]
    gate = jnp.dot(h, wg_ref[...], preferred_element_type=F32)
    up = jnp.dot(h, wu_ref[...], preferred_element_type=F32)
    a = (gate * jax.nn.sigmoid(gate) * up).astype(BF16)
    acc_sc[...] += jnp.dot(a, wd_ref[...], preferred_element_type=F32)

    @pl.when(f == pl.num_programs(1) - 1)
    def _():
        y = acc_sc[...]
        if final:
            y = _rms(y, fg_ref[...])
        o_ref[...] = y


def ffn(x, g, wg, wu, wd, final_g=None, tm=512, nf=2):
    m, d = x.shape
    hid = wg.shape[1]
    tf = hid // nf
    assert m % tm == 0 and hid % nf == 0 and tf % LANES == 0
    final = final_g is not None
    fg = (final_g if final else g).reshape(1, d).astype(F32)
    return pl.pallas_call(
        functools.partial(_ffn_kernel, final=final),
        grid=(m // tm, nf),
        in_specs=[pl.BlockSpec((tm, d), lambda i, f: (i, 0)),
                  pl.BlockSpec((1, d), lambda i, f: (0, 0)),
                  pl.BlockSpec((d, tf), lambda i, f: (0, f)),
                  pl.BlockSpec((d, tf), lambda i, f: (0, f)),
                  pl.BlockSpec((tf, d), lambda i, f: (f, 0)),
                  pl.BlockSpec((1, d), lambda i, f: (0, 0))],
        out_specs=pl.BlockSpec((tm, d), lambda i, f: (i, 0)),
        out_shape=jax.ShapeDtypeStruct((m, d), F32),
        scratch_shapes=[pltpu.VMEM((tm, d), BF16), pltpu.VMEM((tm, d), F32)],
        compiler_params=_cparams(2), name="ffn")(x, g.reshape(1, d).astype(F32), wg, wu, wd, fg)


def _flash_kernel(qt_ref, k_ref, vt_ref, tile_ref, o_ref, m_sc, l_sc, acc_sc, *, n_tiles, has_far, i_off):
    i = pl.program_id(2) + i_off
    qt = qt_ref[0, 0, 0]
    j_lo = 0 if has_far else jnp.maximum(i - (n_tiles - 1), 0)

    def raw(j):
        return jnp.dot(k_ref[0, 0, j], qt, preferred_element_type=F32)

    def scores(j):
        return raw(j) + tile_ref[0, jnp.minimum(i - j, n_tiles)]

    def reset():
        l_sc[...] = jnp.zeros(l_sc.shape, F32)
        acc_sc[...] = jnp.zeros(acc_sc.shape, F32)

    def accumulate(j, p):
        l_sc[...] += jnp.sum(p, axis=0, keepdims=True)
        acc_sc[...] += jnp.dot(vt_ref[0, 0, j], p.astype(BF16), preferred_element_type=F32)

    reset()
    s_diag = raw(i) + tile_ref[0, 0]
    m = jnp.max(s_diag, axis=0, keepdims=True)
    accumulate(i, jnp.exp2(s_diag - m))
    if has_far:
        m_far = m - tile_ref[0, n_tiles, 0:1, :]

        n_far = jnp.maximum(i - (n_tiles - 1), 0)

        def far_group(jj, carry):
            ps = [jnp.exp2(raw(FAR_UNROLL * jj + u) - m_far) for u in range(FAR_UNROLL)]
            for u in range(FAR_UNROLL):
                accumulate(FAR_UNROLL * jj + u, ps[u])
            return carry

        def far_single(j, carry):
            accumulate(j, jnp.exp2(raw(j) - m_far))
            return carry

        n_grouped = (n_far // FAR_UNROLL) * FAR_UNROLL
        lax.fori_loop(0, n_far // FAR_UNROLL, far_group, 0)
        lax.fori_loop(n_grouped, n_far, far_single, 0)
    for t in range(1, n_tiles):
        @pl.when(i - t >= 0)
        def _(t=t):
            accumulate(i - t, jnp.exp2(raw(i - t) + tile_ref[0, t] - m))

    l = l_sc[...]
    in_range = (jnp.max(l) < SUM_MAX) & (jnp.min(l) > SUM_MIN)

    @pl.when(jnp.logical_not(in_range))
    def _():
        reset()
        m_sc[...] = jnp.full(m_sc.shape, M_INIT, F32)

        def body(j, carry):
            s = scores(j)
            m_prev = m_sc[...]
            m_new = jnp.maximum(m_prev, jnp.max(s, axis=0, keepdims=True))
            alpha = jnp.exp2(m_prev - m_new)
            p = jnp.exp2(s - m_new)
            l_sc[...] = alpha * l_sc[...] + jnp.sum(p, axis=0, keepdims=True)
            acc_sc[...] = alpha * acc_sc[...] + jnp.dot(vt_ref[0, 0, j], p.astype(BF16), preferred_element_type=F32)
            m_sc[...] = m_new
            return carry

        lax.fori_loop(j_lo, i + 1, body, 0)

    o_ref[0, 0, 0] = (acc_sc[...] / l_sc[...]).astype(o_ref.dtype)


def flash(qt, k, vt, tiles, *, n_tiles, has_far, i_off=0, out_dtype=F32):
    b, g, nq, dq, blk = qt.shape
    gk, nb = k.shape[1], k.shape[2]
    gv, dv = vt.shape[1], vt.shape[3]
    gb = tiles.shape[0]
    assert tiles.shape[1] == n_tiles + 1
    kdiv, vdiv, bdiv = g // gk, g // gv, g // gb
    return pl.pallas_call(
        functools.partial(_flash_kernel, n_tiles=n_tiles, has_far=has_far, i_off=i_off),
        grid=(b, g, nq),
        in_specs=[pl.BlockSpec((1, 1, 1, dq, blk), lambda bi, gi, i: (bi, gi, i, 0, 0)),
                  pl.BlockSpec((1, 1, nb, blk, dq), lambda bi, gi, i: (bi, gi // kdiv, 0, 0, 0)),
                  pl.BlockSpec((1, 1, nb, dv, blk), lambda bi, gi, i: (bi, gi // vdiv, 0, 0, 0)),
                  pl.BlockSpec((1, n_tiles + 1, blk, blk), lambda bi, gi, i: (gi // bdiv, 0, 0, 0))],
        out_specs=pl.BlockSpec((1, 1, 1, dv, blk), lambda bi, gi, i: (bi, gi, i, 0, 0)),
        out_shape=jax.ShapeDtypeStruct((b, g, nq, dv, blk), out_dtype),
        scratch_shapes=[pltpu.VMEM((1, blk), F32), pltpu.VMEM((1, blk), F32), pltpu.VMEM((dv, blk), F32)],
        compiler_params=_cparams(3), name="flash")(qt, k, vt, tiles)


def _dsa_kernel(qit_ref, w_ref, ki_ref, qt_ref, k_ref, vt_ref, tile_ref, o_ref,
                key_sc, m_sc, l_sc, acc_sc, *, i_off, n_sel, idx_bits):
    tq, tk, nh = DSA_TQ, DSA_TK, D_HEADS
    i = pl.program_id(1) + i_off
    nkb = (i * tq) // tk + 1
    krow = lax.broadcasted_iota(I32, (tk, tq), 0)
    qcol = lax.broadcasted_iota(I32, (tk, tq), 1)
    q_chunk = jnp.right_shift(i * tq + qcol, 6)

    def head(x, h):
        return x[:, h * tq:(h + 1) * tq]

    qit = qit_ref[0, 0]
    w = w_ref[0, 0]

    def score_block(j):
        lg = jnp.maximum(jnp.dot(ki_ref[0, j], qit, preferred_element_type=F32), 0.0) * w
        sc = head(lg, 0)
        for h in range(1, IDX_HEADS):
            sc = sc + head(lg, h)
        sc = jnp.where(sc == 0.0, 0.0, sc)
        bits = pltpu.bitcast(sc, I32)
        key = bits ^ (jnp.right_shift(bits, 31) & INT_MAX)
        adm = jnp.right_shift(j * tk + krow, 6) <= q_chunk
        key_sc[j] = jnp.where(adm, key, INT_MIN)

    def score_group(jj, carry):
        for u in range(DSA_SCORE_UNROLL):
            score_block(DSA_SCORE_UNROLL * jj + u)
        return carry

    def score_single(j, carry):
        score_block(j)
        return carry

    lax.fori_loop(0, nkb // DSA_SCORE_UNROLL, score_group, 0)
    lax.fori_loop((nkb // DSA_SCORE_UNROLL) * DSA_SCORE_UNROLL, nkb, score_single, 0)

    def count(pred_fn):
        def body(j, acc):
            ind = jnp.where(pred_fn(key_sc[j], j * tk + krow), 1.0, 0.0)
            return acc + jnp.sum(ind.reshape(tk // COUNT_ROWS, COUNT_ROWS, tq), axis=0)
        acc = lax.fori_loop(0, nkb, body, jnp.zeros((COUNT_ROWS, tq), F32))
        return jnp.sum(acc, axis=0, keepdims=True)

    target = float(n_sel)
    qpos = i * tq + lax.broadcasted_iota(I32, (1, tq), 1)
    n_adm = ((jnp.right_shift(qpos, 6) + 1) * CHUNK).astype(F32)
    real = n_adm > target

    def search_cond(state):
        thr, c_thr, b = state
        settled = (c_thr == target) | jnp.logical_not(real)
        return (b < 32) & (jnp.min(jnp.where(settled, 1.0, 0.0)) < 1.0)

    def value_bit(state):
        thr, c_thr, b = state
        cand = thr + jnp.left_shift(jnp.int32(1), 31 - b)
        cnt = count(lambda k, idx: k >= cand)
        ge = cnt >= target
        return jnp.where(ge, cand, thr), jnp.where(ge, cnt, c_thr), b + 1

    thr, c_thr, _ = lax.while_loop(search_cond, value_bit,
                                   (jnp.full((1, tq), INT_MIN, I32), n_adm, jnp.int32(0)))

    def tie_search():
        need = target - count(lambda k, idx: k > thr)

        def index_bit(b, y):
            cand = y + jnp.left_shift(jnp.int32(1), idx_bits - 1 - b)
            cnt = count(lambda k, idx: (k == thr) & (idx < cand))
            return jnp.where(cnt < need, cand, y)
        return lax.fori_loop(0, idx_bits, index_bit, jnp.zeros((1, tq), I32))

    has_ties = jnp.max(jnp.where((c_thr > target) & real, 1.0, 0.0)) > 0.0
    last = lax.cond(has_ties, tie_search, lambda: jnp.full((1, tq), INT_MAX, I32))
    last = jnp.where(real, last, -1)

    qt = qt_ref[0, 0]
    n_near = 3

    def raw(j):
        return jnp.dot(k_ref[0, j], qt, preferred_element_type=F32)

    def tile_of(j):
        return jnp.minimum((i * tq - j * tk) // tq, n_near)

    def select(j, s):
        key = key_sc[j]
        sel = (key > thr) | ((key == thr) & (j * tk + krow <= last))
        selb = jnp.where(sel, 0.0, UNSELECTED)
        return jnp.concatenate([head(s, h) + selb for h in range(nh)], axis=1)

    def reset():
        l_sc[...] = jnp.zeros(l_sc.shape, F32)
        acc_sc[...] = jnp.zeros(acc_sc.shape, F32)

    def accumulate(j, p):
        l_sc[...] += jnp.sum(p, axis=0, keepdims=True)
        acc_sc[...] += jnp.dot(vt_ref[0, j], p.astype(BF16), preferred_element_type=F32)

    reset()
    s_last = raw(nkb - 1) + tile_ref[tile_of(nkb - 1)]
    m = jnp.max(s_last, axis=0, keepdims=True)
    accumulate(nkb - 1, jnp.exp2(select(nkb - 1, s_last) - m))
    m_far = m - tile_ref[n_near, 0:1, :]
    n_far = jnp.maximum(i - 1, 0) // 2

    def far_group(jj, carry):
        ps = [jnp.exp2(select(DSA_FAR_UNROLL * jj + u, raw(DSA_FAR_UNROLL * jj + u)) - m_far)
              for u in range(DSA_FAR_UNROLL)]
        for u in range(DSA_FAR_UNROLL):
            accumulate(DSA_FAR_UNROLL * jj + u, ps[u])
        return carry

    def far_single(j, carry):
        accumulate(j, jnp.exp2(select(j, raw(j)) - m_far))
        return carry

    def near(j, carry):
        accumulate(j, jnp.exp2(select(j, raw(j) + tile_ref[tile_of(j)]) - m))
        return carry

    n_grouped = (n_far // DSA_FAR_UNROLL) * DSA_FAR_UNROLL
    lax.fori_loop(0, n_far // DSA_FAR_UNROLL, far_group, 0)
    lax.fori_loop(n_grouped, n_far, far_single, 0)
    lax.fori_loop(n_far, nkb - 1, near, 0)

    l = l_sc[...]
    in_range = (jnp.max(l) < SUM_MAX) & (jnp.min(l) > SUM_MIN)

    @pl.when(jnp.logical_not(in_range))
    def _():
        reset()
        m_sc[...] = jnp.full(m_sc.shape, M_INIT, F32)

        def attend(j, carry):
            s = select(j, raw(j) + tile_ref[tile_of(j)])
            m_prev = m_sc[...]
            m_new = jnp.maximum(m_prev, jnp.max(s, axis=0, keepdims=True))
            alpha = jnp.exp2(m_prev - m_new)
            p = jnp.exp2(s - m_new)
            l_sc[...] = alpha * l_sc[...] + jnp.sum(p, axis=0, keepdims=True)
            acc_sc[...] = alpha * acc_sc[...] + jnp.dot(vt_ref[0, j], p.astype(BF16), preferred_element_type=F32)
            m_sc[...] = m_new
            return carry

        lax.fori_loop(0, nkb, attend, 0)

    o_ref[0, 0] = (acc_sc[...] / l_sc[...]).astype(o_ref.dtype)


def dsa(qit, w, ki, qt, k, vt, tiles, *, i_off, n_sel, out_dtype=F32):
    b, nq, kdim, cols = qit.shape
    nkb = ki.shape[1]
    dk, dv = k.shape[-1], vt.shape[2]
    idx_bits = max(1, (nkb * DSA_TK - 1).bit_length())
    return pl.pallas_call(
        functools.partial(_dsa_kernel, i_off=i_off, n_sel=n_sel, idx_bits=idx_bits),
        grid=(b, nq),
        in_specs=[pl.BlockSpec((1, 1, kdim, cols), lambda bi, i: (bi, i, 0, 0)),
                  pl.BlockSpec((1, 1, 1, cols), lambda bi, i: (bi, i, 0, 0)),
                  pl.BlockSpec((1, nkb, DSA_TK, kdim), lambda bi, i: (bi, 0, 0, 0)),
                  pl.BlockSpec((1, 1, dk, cols), lambda bi, i: (bi, i, 0, 0)),
                  pl.BlockSpec((1, nkb, DSA_TK, dk), lambda bi, i: (bi, 0, 0, 0)),
                  pl.BlockSpec((1, nkb, dv, DSA_TK), lambda bi, i: (bi, 0, 0, 0)),
                  pl.BlockSpec((4, DSA_TK, cols), lambda bi, i: (0, 0, 0))],
        out_specs=pl.BlockSpec((1, 1, dv, cols), lambda bi, i: (bi, i, 0, 0)),
        out_shape=jax.ShapeDtypeStruct((b, nq, dv, cols), out_dtype),
        scratch_shapes=[pltpu.VMEM((nkb, DSA_TK, DSA_TQ), I32), pltpu.VMEM((1, cols), F32),
                        pltpu.VMEM((1, cols), F32), pltpu.VMEM((dv, cols), F32)],
        compiler_params=_cparams(2), name="dsa")(qit, w, ki, qt, k, vt, tiles)


_NT = (((1,), (1,)), ((), ()))


def _heads(x, n, d):
    return x.reshape(n, d, x.shape[-1])


def _proj_even_kernel(x_ref, g_ref, w1_ref, w2t_ref, w3_ref, qn_ref, wuqt_ref, kvn_ref, wkb_ref, e1_ref, e2_ref,
                      wuvt_ref, cosn_ref, sinn_ref, cost_ref, sint_ref,
                      akv_ref, ckv_ref, y1_ref, y2_ref, qta_ref, ka_ref, vta_ref, qtb_ref, kb_ref, vtb_ref):
    nh, da, rope_half = A_HEADS, A_DIM, B_ROPE // 2
    h = _rms(x_ref[...], g_ref[...]).astype(BF16)
    h1 = jnp.dot(h, w1_ref[...], preferred_element_type=F32)
    akv_ref[...] = h1[:, :2 * nh * da]
    t2 = lax.dot_general(w2t_ref[...], h, _NT, preferred_element_type=F32)
    qta_ref[0, :, 0] = _heads(t2[:nh * da], nh, da).astype(BF16)
    vta_ref[0, :, 0] = _heads(t2[nh * da:], nh, da).astype(BF16)
    k3 = jnp.dot(h, w3_ref[...], preferred_element_type=F32)
    for hd in range(nh):
        ka_ref[0, hd, 0] = k3[:, hd * LANES:hd * LANES + da].astype(BF16)

    c0 = 2 * nh * da
    cqn = _rms(h1[:, c0:c0 + Q_LORA], qn_ref[...]).astype(BF16)
    tq = lax.dot_general(wuqt_ref[...], cqn, _NT, preferred_element_type=F32)
    n_nope, n_rope = B_HEADS * B_NOPE, B_HEADS * rope_half
    x1, x2 = tq[n_nope:n_nope + n_rope], tq[n_nope + n_rope:]
    ct, st = cost_ref[0], sint_ref[0]
    y1t, y2t = x1 * ct - x2 * st, x1 * st + x2 * ct
    for hd in range(B_HEADS):
        qtb_ref[0, hd, 0] = jnp.concatenate(
            [tq[hd * B_NOPE:(hd + 1) * B_NOPE], y1t[hd * rope_half:(hd + 1) * rope_half],
             y2t[hd * rope_half:(hd + 1) * rope_half]], axis=0).astype(BF16)

    c1 = c0 + Q_LORA
    ckv = _rms(h1[:, c1:c1 + KV_LORA], kvn_ref[...])
    ckv_ref[...] = ckv
    ckvb = ckv.astype(BF16)
    k1, k2 = h1[:, c1 + KV_LORA:c1 + KV_LORA + LANES], h1[:, c1 + KV_LORA + LANES:]
    cn, sn = cosn_ref[...], sinn_ref[...]
    y1, y2 = k1 * cn - k2 * sn, k1 * sn + k2 * cn
    y1_ref[...] = y1
    y2_ref[...] = y2
    kb = (jnp.dot(ckvb, wkb_ref[...], preferred_element_type=F32)
          + jnp.dot(y1.astype(BF16), e1_ref[...], preferred_element_type=F32)
          + jnp.dot(y2.astype(BF16), e2_ref[...], preferred_element_type=F32))
    for hd in range(B_HEADS):
        kb_ref[0, hd, 0] = kb[:, hd * LANES:hd * LANES + B_NOPE + B_ROPE].astype(BF16)
    tv = lax.dot_general(wuvt_ref[...], ckvb, _NT, preferred_element_type=F32)
    vtb_ref[0, :, 0] = _heads(tv, B_HEADS, B_V).astype(BF16)


def proj_even(x, g, w_in, q_norm, w_uq, kv_norm, w_ukv, cos, sin, blk):
    p, d = x.shape
    nq = p // blk
    nh, da, half = A_HEADS, A_DIM, B_ROPE // 2
    s_a = A_DIM ** -0.5 * LOG2E
    s_b = (B_NOPE + B_ROPE) ** -0.5 * LOG2E
    aq_w, ak_w, av_w, cq_w, ckv_w, kr_w = _split(w_in, EVEN_SPLITS)

    def lane_pad(w, n=LANES):
        return jnp.pad(w, ((0, 0), (0, n - w.shape[1])))

    def head_chunks(w, dh):
        k = w.shape[0]
        return jnp.pad(w.reshape(k, -1, dh), ((0, 0), (0, 0), (0, LANES - dh))).reshape(k, -1)

    w1 = jnp.concatenate([ak_w, av_w, cq_w, ckv_w, lane_pad(kr_w[:, :half]), lane_pad(kr_w[:, half:])], axis=1).astype(BF16)
    w2t = jnp.concatenate([aq_w * s_a, av_w], axis=1).T.astype(BF16)
    w3 = head_chunks(ak_w, da).astype(BF16)
    uq = (w_uq * s_b).reshape(Q_LORA, B_HEADS, B_NOPE + B_ROPE)
    wuqt = jnp.concatenate([uq[:, :, :B_NOPE].reshape(Q_LORA, -1), uq[:, :, B_NOPE:B_NOPE + half].reshape(Q_LORA, -1),
                            uq[:, :, B_NOPE + half:].reshape(Q_LORA, -1)], axis=1).T.astype(BF16)
    ukv = w_ukv.reshape(KV_LORA, B_HEADS, B_NOPE + B_V)
    wkb = head_chunks(ukv[:, :, :B_NOPE].reshape(KV_LORA, -1), B_NOPE).astype(BF16)
    wuvt = ukv[:, :, B_NOPE:].reshape(KV_LORA, -1).T.astype(BF16)
    eye = np.zeros((2, LANES, B_HEADS, LANES), np.float32)
    for r in range(half):
        eye[0, r, :, B_NOPE + r] = 1.0
        eye[1, r, :, B_NOPE + half + r] = 1.0
    e1, e2 = (jnp.asarray(e.reshape(LANES, B_HEADS * LANES), BF16) for e in eye)
    cosn, sinn = lane_pad(cos), lane_pad(sin)
    cost = jnp.tile(cos.reshape(nq, blk, half).transpose(0, 2, 1), (1, B_HEADS, 1))
    sint = jnp.tile(sin.reshape(nq, blk, half).transpose(0, 2, 1), (1, B_HEADS, 1))

    def full(a):
        return pl.BlockSpec(a.shape, lambda i: (0,) * a.ndim)

    def rows(n):
        return pl.BlockSpec((blk, n), lambda i: (i, 0))

    def tile3(a):
        return pl.BlockSpec((1,) + a.shape[1:], lambda i: (i, 0, 0))

    def per_block(g_, a, b_):
        return (jax.ShapeDtypeStruct((1, g_, nq, a, b_), BF16),
                pl.BlockSpec((1, g_, 1, a, b_), lambda i: (0, 0, i, 0, 0)))

    consts = [g.reshape(1, d).astype(F32), w1, w2t, w3, q_norm.reshape(1, -1).astype(F32), wuqt,
              kv_norm.reshape(1, -1).astype(F32), wkb, e1, e2, wuvt]
    blocks = [per_block(nh, da, blk), per_block(nh, blk, da), per_block(nh, da, blk),
              per_block(B_HEADS, B_NOPE + B_ROPE, blk), per_block(B_HEADS, blk, B_NOPE + B_ROPE),
              per_block(B_HEADS, B_V, blk)]
    out_shape = [jax.ShapeDtypeStruct((p, 2 * nh * da), F32), jax.ShapeDtypeStruct((p, KV_LORA), F32),
                 jax.ShapeDtypeStruct((p, LANES), F32), jax.ShapeDtypeStruct((p, LANES), F32)] + [s for s, _ in blocks]
    out_specs = [rows(2 * nh * da), rows(KV_LORA), rows(LANES), rows(LANES)] + [s for _, s in blocks]
    return pl.pallas_call(
        _proj_even_kernel, grid=(nq,),
        in_specs=[rows(d)] + [full(a) for a in consts] + [rows(LANES), rows(LANES), tile3(cost), tile3(sint)],
        out_specs=out_specs, out_shape=out_shape,
        compiler_params=_cparams(1), name="proj_even")(x, *consts, cosn, sinn, cost, sint)


def _proj_odd_kernel(x_ref, g_ref, w1_ref, w2t_ref, w3_ref,
                     st_ref, cqt_ref, kc_ref, cvt_ref, qit_ref, wq_ref, qt_ref):
    nmap, width = 2 * C_HEADS, C_HEADS * 2 * C_DIM
    h = _rms(x_ref[...], g_ref[...]).astype(BF16)
    st_ref[...] = jnp.dot(h, w1_ref[...], preferred_element_type=F32)
    t2 = lax.dot_general(w2t_ref[...], h, _NT, preferred_element_type=F32)
    cqt_ref[0, :, 0] = _heads(t2[:width], nmap, C_DIM).astype(BF16)
    cvt_ref[0, :, 0] = _heads(t2[width:2 * width], C_HEADS, 2 * C_DIM).astype(BF16)
    k3 = jnp.dot(h, w3_ref[...], preferred_element_type=F32)
    for m in range(nmap):
        kc_ref[0, m, 0] = k3[:, m * LANES:m * LANES + C_DIM].astype(BF16)

    dqt = t2[2 * width:2 * width + D_HEADS * D_DIM]
    dqit = t2[2 * width + D_HEADS * D_DIM:2 * width + D_HEADS * (D_DIM + IDX_DIM)]
    dwt = t2[2 * width + D_HEADS * (D_DIM + IDX_DIM):]

    def stack(x, rows_per_head, n_rows, b):
        cols = slice(b * DSA_TQ, (b + 1) * DSA_TQ)
        return jnp.concatenate([x[hd * rows_per_head:hd * rows_per_head + n_rows, cols] for hd in range(D_HEADS)], axis=1)

    for b in range(x_ref.shape[0] // DSA_TQ):
        qt_ref[0, b] = stack(dqt, D_DIM, D_DIM, b).astype(BF16)
        qi = stack(dqit, IDX_DIM, IDX_DIM, b)
        hi = qi.astype(BF16)
        lo = (qi - hi.astype(F32)).astype(BF16)
        qit_ref[0, b] = jnp.concatenate([hi, lo, hi], axis=0)
        wq_ref[0, b] = stack(dwt, SUBLANES, 1, b)


def proj_odd(x, g, w_in, blk):
    p, d = x.shape
    nq, nq_dsa = p // blk, p // DSA_TQ
    s_c = C_DIM ** -0.5 * LOG2E
    s_d = D_DIM ** -0.5 * LOG2E
    s_w = (IDX_HEADS ** -0.5) * (IDX_DIM ** -0.5)
    cq_w, ck_w, cv_w, dq_w, dk_w, dv_w, dqi_w, dki_w, dw_w = _split(w_in, ODD_SPLITS)

    def lane_pad(w):
        return jnp.pad(w, ((0, 0), (0, LANES - w.shape[1])))

    w1 = jnp.concatenate([ck_w, cv_w, lane_pad(dk_w), lane_pad(dv_w), lane_pad(dki_w)], axis=1).astype(BF16)
    dw_rows = jnp.pad((dw_w * s_w)[:, :, None], ((0, 0), (0, 0), (0, SUBLANES - 1))).reshape(d, -1)
    w2t = jnp.concatenate([cq_w * s_c, cv_w, dq_w * s_d, dqi_w, dw_rows], axis=1).T.astype(BF16)
    w3 = jnp.pad(ck_w.reshape(d, -1, C_DIM), ((0, 0), (0, 0), (0, LANES - C_DIM))).reshape(d, -1).astype(BF16)
    consts = [g.reshape(1, d).astype(F32), w1, w2t, w3]
    cols = D_HEADS * DSA_TQ
    per = blk // DSA_TQ

    def per_block(g_, a, b_):
        return (jax.ShapeDtypeStruct((1, g_, nq, a, b_), BF16),
                pl.BlockSpec((1, g_, 1, a, b_), lambda i: (0, 0, i, 0, 0)))

    def per_dsa(a, dtype):
        return (jax.ShapeDtypeStruct((1, nq_dsa, a, cols), dtype), pl.BlockSpec((1, per, a, cols), lambda i: (0, i, 0, 0)))

    blocks = [per_block(2 * C_HEADS, C_DIM, blk), per_block(2 * C_HEADS, blk, C_DIM), per_block(C_HEADS, 2 * C_DIM, blk),
              per_dsa(3 * IDX_DIM, BF16), per_dsa(1, F32), per_dsa(D_DIM, BF16)]
    return pl.pallas_call(
        _proj_odd_kernel, grid=(nq,),
        in_specs=[pl.BlockSpec((blk, d), lambda i: (i, 0))] + [pl.BlockSpec(a.shape, lambda i: (0, 0)) for a in consts],
        out_specs=[pl.BlockSpec((blk, w1.shape[1]), lambda i: (i, 0))] + [s for _, s in blocks],
        out_shape=[jax.ShapeDtypeStruct((p, w1.shape[1]), F32)] + [s for s, _ in blocks],
        compiler_params=_cparams(1), name="proj_odd")(x, *consts)


def _out_odd_kernel(x_ref, c_ref, d_ref, lam_ref, sub_ref, w_ref, o_ref, *, post_scale):
    blk = x_ref.shape[0]
    lam, sub = lam_ref[...], sub_ref[...]
    parts = []
    for hd in range(C_HEADS):
        c = c_ref[0, 2 * hd, 0] - lam * c_ref[0, 2 * hd + 1, 0]
        parts.append(c * lax.rsqrt(jnp.mean(c * c, axis=0, keepdims=True) + EPS) * sub * post_scale)
    for hd in range(D_HEADS):
        parts.append(jnp.concatenate([d_ref[0, b][:, hd * DSA_TQ:(hd + 1) * DSA_TQ]
                                      for b in range(blk // DSA_TQ)], axis=1).astype(F32))
    mix = jnp.concatenate(parts, axis=0).T.astype(BF16)
    o_ref[...] = x_ref[...] + jnp.dot(mix, w_ref[...], preferred_element_type=F32)


def out_odd(x, c_t, d_t, lam, subln, post_scale, w_out):
    p, d = x.shape
    _, nmap, nq, dc, blk = c_t.shape
    per = blk // DSA_TQ
    return pl.pallas_call(
        functools.partial(_out_odd_kernel, post_scale=post_scale), grid=(nq,),
        in_specs=[pl.BlockSpec((blk, d), lambda i: (i, 0)),
                  pl.BlockSpec((1, nmap, 1, dc, blk), lambda i: (0, 0, i, 0, 0)),
                  pl.BlockSpec((1, per) + d_t.shape[2:], lambda i: (0, i, 0, 0)),
                  pl.BlockSpec((1, blk), lambda i: (0, 0)),
                  pl.BlockSpec((dc, 1), lambda i: (0, 0)),
                  pl.BlockSpec(w_out.shape, lambda i: (0, 0))],
        out_specs=pl.BlockSpec((blk, d), lambda i: (i, 0)),
        out_shape=jax.ShapeDtypeStruct((p, d), F32),
        compiler_params=_cparams(1), name="out_odd")(
            x, c_t, d_t, jnp.full((1, blk), lam, F32), subln.reshape(dc, 1).astype(F32), w_out)


def _out_even_kernel(x_ref, a_ref, b_ref, w_ref, o_ref):
    blk = x_ref.shape[0]
    mix_t = jnp.concatenate([a_ref[0, :, 0].reshape(-1, blk), b_ref[0, :, 0].reshape(-1, blk)], axis=0)
    mix = mix_t.astype(F32).T.astype(BF16)
    o_ref[...] = x_ref[...] + jnp.dot(mix, w_ref[...], preferred_element_type=F32)


def out_even(x, a_t, b_t, w_out):
    p, d = x.shape
    _, g_, nq, dv, blk = a_t.shape
    blk_spec = pl.BlockSpec((1, g_, 1, dv, blk), lambda i: (0, 0, i, 0, 0))
    return pl.pallas_call(
        _out_even_kernel, grid=(nq,),
        in_specs=[pl.BlockSpec((blk, d), lambda i: (i, 0)), blk_spec, blk_spec,
                  pl.BlockSpec(w_out.shape, lambda i: (0, 0))],
        out_specs=pl.BlockSpec((blk, d), lambda i: (i, 0)),
        out_shape=jax.ShapeDtypeStruct((p, d), F32),
        compiler_params=_cparams(1), name="out_even")(x, a_t, b_t, w_out)


def _rope_tables(pos):
    half = B_ROPE // 2
    freqs = jnp.power(jnp.float32(ROPE_THETA), -jnp.arange(half, dtype=F32) / half)
    ang = pos.astype(F32)[:, None] * freqs[None, :]
    return jnp.cos(ang), jnp.sin(ang)


def _rope(x, cos, sin):
    half = x.shape[-1] // 2
    shape = (x.shape[0],) + (1,) * (x.ndim - 2) + (half,)
    c, s = cos.reshape(shape), sin.reshape(shape)
    x1, x2 = x[..., :half], x[..., half:]
    return jnp.concatenate([x1 * c - x2 * s, x1 * s + x2 * c], axis=-1)


def _t5_bucket(rel):
    nb = T5_BUCKETS // 2
    max_exact = nb // 2
    ret = jnp.where(rel > 0, nb, 0)
    n = jnp.abs(rel)
    nf = jnp.maximum(n, 1).astype(F32)
    large = max_exact + (jnp.log(nf / max_exact) / math.log(T5_MAX_DIST / max_exact)
                         * (nb - max_exact)).astype(I32)
    large = jnp.minimum(large, nb - 1)
    return ret + jnp.where(n < max_exact, n, large)


def _chunk_diff(tq, tk, n_tiles):
    r = np.arange(tq)[None, None, :]
    c = np.arange(tk)[None, :, None]
    t = np.arange(n_tiles)[:, None, None]
    return c // CHUNK - r // CHUNK - t * (tq // CHUNK)


def _strip_distances(tq, tk, n_tiles):
    y = np.arange(tq + tk)[None, :]
    t = np.arange(n_tiles)[:, None]
    return np.where(y < tq, -y, tq + tk - y) - t * tq


def _toeplitz(strip, tq, tk):
    n = tq + tk
    lead = strip.shape[:-1]
    rows = jnp.broadcast_to(strip[..., None, :], lead + (tk, n)).reshape(lead + (tk * n,))
    return rows[..., :tk * (n - 1)].reshape(lead + (tk, n - 1))[..., :tq]


def band_tiles(relbias, blk):
    n_tiles = (A_LEFT_CHUNKS * CHUNK) // blk + 1
    dchunk = _chunk_diff(blk, blk, n_tiles)
    adm = (dchunk <= 0) & (-dchunk <= A_LEFT_CHUNKS)
    idx = np.clip(-_strip_distances(blk, blk, n_tiles), -A_MAX_REL, A_MAX_REL) + A_MAX_REL
    bias = _toeplitz(relbias.astype(F32)[:, idx] * LOG2E, blk, blk)
    tiles = jnp.where(adm[None], bias, MASKED)
    far = jnp.full((relbias.shape[0], 1, blk, blk), MASKED, F32)
    return jnp.concatenate([tiles, far], axis=1), n_tiles


def causal_tiles(blk):
    t0 = np.where(_chunk_diff(blk, blk, 1) <= 0, 0.0, MASKED).astype(np.float32)
    return jnp.asarray(np.concatenate([t0, np.zeros_like(t0)], axis=0)[None])


def t5_tiles(tab, tq, tk, n_tiles):
    assert n_tiles * tq - tk + 1 >= T5_MAX_DIST
    tab = tab.astype(F32) * LOG2E
    strip = jnp.moveaxis(tab[_t5_bucket(jnp.asarray(_strip_distances(tq, tk, n_tiles), I32))], -1, 0)
    tiles = jnp.where((_chunk_diff(tq, tk, n_tiles) <= 0)[None], _toeplitz(strip, tq, tk), MASKED)
    far = tab[_t5_bucket(jnp.asarray([-(n_tiles * tq + tk)], I32))[0]]
    far = jnp.broadcast_to(far[:, None, None, None], (tab.shape[1], 1, tk, tq))
    return jnp.concatenate([tiles, far], axis=1)


def _row_blocks(x, blk):
    b, t, g, d = x.shape
    return jnp.transpose(x.reshape(b, t // blk, blk, g, d), (0, 3, 1, 2, 4)).astype(BF16)


def _col_blocks(x, blk):
    b, t, g, d = x.shape
    return jnp.transpose(x.reshape(b, t // blk, blk, g, d), (0, 3, 1, 4, 2)).astype(BF16)


def _from_col_blocks(o):
    b, g, nq, d, blk = o.shape
    return jnp.transpose(o, (0, 2, 4, 1, 3)).reshape(b, nq * blk, g * d)


def _pad_rows(x, n, front=0):
    back = n - x.shape[1] - front
    return jnp.pad(x, ((0, 0), (front, back)) + ((0, 0),) * (x.ndim - 2))


def _split(h, sizes):
    out, o = [], 0
    for s in sizes:
        out.append(h[:, o:o + s])
        o += s
    return out


def _split3(x):
    hi = x.astype(BF16)
    lo = (x - hi.astype(F32)).astype(BF16)
    return hi, lo


def _dsa_keys(dk_all, dv_all, dki_all):
    b, nkb = dk_all.shape[0], dk_all.shape[1] // DSA_TK
    kh, kl = _split3(dki_all)
    ki = jnp.concatenate([kh, kh, kl], axis=-1).reshape(b, nkb, DSA_TK, 3 * IDX_DIM)
    k = dk_all.astype(BF16).reshape(b, nkb, DSA_TK, D_DIM)
    vt = jnp.transpose(dv_all.astype(BF16).reshape(b, nkb, DSA_TK, D_DIM), (0, 1, 3, 2))
    return ki, k, vt


def _dsa_call(dq, dqi, dw, dk_all, dv_all, dki_all, tiles, i_off, n_sel):
    b, tq_all = dq.shape[:2]
    nq = tq_all // DSA_TQ

    def stack_t(x):
        d = x.shape[-1]
        return jnp.transpose(x.reshape(b, nq, DSA_TQ, D_HEADS, d), (0, 1, 4, 3, 2)).reshape(b, nq, d, D_HEADS * DSA_TQ)

    qh, ql = _split3(dqi)
    qit = stack_t(jnp.concatenate([qh, ql, qh], axis=-1))
    w = stack_t(dw[..., None])
    qt = stack_t((dq * (D_DIM ** -0.5 * LOG2E)).astype(BF16))
    ki, k, vt = _dsa_keys(dk_all, dv_all, dki_all)
    o = dsa(qit, w, ki, qt, k, vt, tiles, i_off=i_off, n_sel=n_sel)
    o = jnp.transpose(o.reshape(b, nq, D_DIM, D_HEADS, DSA_TQ), (0, 1, 4, 3, 2))
    return o.reshape(b, tq_all, D_HEADS * D_DIM)


P_BLK = 512
S_BLK = 128


def kernel(x_prompt, x_sample, cache_a_k, cache_a_v, cache_b_ckv, cache_b_krope, cache_c_k, cache_c_v, cache_d_k, cache_d_v, cache_d_kidx, t5_table, norm_attn, norm_ffn, final_norm, even_w_in, even_w_out, a_relbias, b_q_norm, b_kv_norm, b_w_uq, b_w_ukv, odd_w_in, odd_w_out, c_lambda_q1, c_lambda_k1, c_lambda_q2, c_lambda_k2, c_subln, ffn_w_gate, ffn_w_up, ffn_w_down):
    pb, pt, d = x_prompt.shape
    sb, st, _ = x_sample.shape
    assert pb == 1 and st == CHUNK
    past = cache_b_ckv.shape[2]
    n_a = cache_a_k.shape[2]
    depth = norm_attn.shape[0]
    n_p, n_s = pb * pt, sb * st
    s_len = past + st
    s_pad = -(-s_len // DSA_TK) * DSA_TK
    s_i = past // S_BLK
    s_i_dsa = past // DSA_TQ
    assert past % S_BLK == 0 and s_pad % S_BLK == 0 and past >= n_a
    p_blk = min(P_BLK, pt)

    cos_p, sin_p = _rope_tables(jnp.arange(pt))
    cos_s, sin_s = _rope_tables(jnp.tile(past + jnp.arange(st), sb))

    def rows_p(a):
        return a.reshape(pb, pt, *a.shape[1:])

    def rows_s(a):
        return a.reshape(sb, st, *a.shape[1:])

    def pad_cols(w):
        n = -(-w.shape[1] // LANES) * LANES
        return jnp.pad(w, ((0, 0), (0, n - w.shape[1]))).astype(BF16)

    def sample_q(a):
        return _pad_rows(a, S_BLK)

    def sample_keys(cache, new, front=0):
        return _pad_rows(jnp.concatenate([cache, new], axis=1), s_pad, front)

    def attend(q, k, v, tiles, blk, n_tiles, has_far, i_off=0):
        o = flash(_col_blocks(q, blk), _row_blocks(k, blk), _col_blocks(v, blk), tiles,
                  n_tiles=n_tiles, has_far=has_far, i_off=i_off)
        return _from_col_blocks(o)

    t5_c, t5_d = t5_table[:, :C_HEADS], t5_table[:, C_HEADS:]
    ctiles_p = causal_tiles(p_blk)
    ctiles_s = causal_tiles(S_BLK)
    c_tiles_p = t5_tiles(t5_c, p_blk, p_blk, 2)
    c_tiles_s = t5_tiles(t5_c, S_BLK, S_BLK, 2)
    d_tiles = t5_tiles(t5_d, DSA_TQ, DSA_TK, 3)
    d_tiles = jnp.transpose(d_tiles, (1, 2, 0, 3)).reshape(4, DSA_TK, D_HEADS * DSA_TQ)
    n_sel_p = min(TOPK_MAX, pt // 4)
    n_sel_s = min(TOPK_MAX, s_len // 4)

    outs_even = {k: [] for k in ("a_k_p", "a_k_s", "a_v_p", "a_v_s", "ckv_p", "ckv_s", "kr_p", "kr_s")}
    outs_odd = {k: [] for k in ("c_k_p", "c_k_s", "c_v_p", "c_v_s", "d_k_p", "d_k_s", "d_v_p", "d_v_s", "d_ki_p", "d_ki_s")}

    def mla_keys(kn, krope):
        return jnp.concatenate([kn, jnp.broadcast_to(krope[:, :, None, :], kn.shape[:3] + (B_ROPE,))], axis=-1)

    def even_prompt(x, l, i):
        a_tiles, nt = band_tiles(a_relbias[i], p_blk)
        akv, ckv, y1, y2, qta, ka, vta, qtb, kb, vtb = proj_even(
            x, norm_attn[l], even_w_in[i], b_q_norm[i], b_w_uq[i], b_kv_norm[i], b_w_ukv[i], cos_p, sin_p, p_blk)
        a_t = flash(qta, ka, vta, a_tiles, n_tiles=nt, has_far=False, out_dtype=BF16)
        b_t = flash(qtb, kb, vtb, ctiles_p, n_tiles=1, has_far=True, out_dtype=BF16)
        half = B_ROPE // 2
        hd = A_HEADS * A_DIM
        outs_even["a_k_p"].append(akv[pt - n_a:, :hd].reshape(pb, n_a, A_HEADS, A_DIM))
        outs_even["a_v_p"].append(akv[pt - n_a:, hd:].reshape(pb, n_a, A_HEADS, A_DIM))
        outs_even["ckv_p"].append(rows_p(ckv))
        outs_even["kr_p"].append(rows_p(jnp.concatenate([y1[:, :half], y2[:, :half]], axis=-1)))
        return out_even(x, a_t, b_t, even_w_out[i].astype(BF16))

    def even_sample(x, l, i):
        h = dense(x, pad_cols(even_w_in[i]), g=norm_attn[l])
        aq, ak, av, cq, ckv_raw, kr_raw = _split(h, EVEN_SPLITS)
        qb = dense(cq, b_w_uq[i].astype(BF16), g=b_q_norm[i]).reshape(-1, B_HEADS, B_NOPE + B_ROPE)
        kv_new, ckv = dense(ckv_raw, b_w_ukv[i].astype(BF16), g=b_kv_norm[i], emit_h=True)
        kr = _rope(kr_raw, cos_s, sin_s)
        q_mla = jnp.concatenate([qb[..., :B_NOPE], _rope(qb[..., B_NOPE:], cos_s, sin_s)], axis=-1)
        q_mla = q_mla * ((B_NOPE + B_ROPE) ** -0.5 * LOG2E)
        kv_s = rows_s(kv_new.reshape(-1, B_HEADS, B_NOPE + B_V))
        aq = (aq * (A_DIM ** -0.5 * LOG2E)).reshape(-1, A_HEADS, A_DIM)
        ak = ak.reshape(-1, A_HEADS, A_DIM)
        av = av.reshape(-1, A_HEADS, A_DIM)
        a_tiles, nt = band_tiles(a_relbias[i], S_BLK)
        ak_all = jnp.concatenate([cache_a_k[i], rows_s(ak)], axis=1)
        av_all = jnp.concatenate([cache_a_v[i], rows_s(av)], axis=1)
        a_out = attend(sample_q(rows_s(aq)), _pad_rows(ak_all, s_pad, past - n_a),
                       _pad_rows(av_all, s_pad, past - n_a), a_tiles, S_BLK, nt, False, s_i)[:, :st]
        kv_c = dense(cache_b_ckv[i].reshape(sb * past, KV_LORA), b_w_ukv[i].astype(BF16))
        kv_c = kv_c.reshape(sb, past, B_HEADS, B_NOPE + B_V)
        kn_all = sample_keys(kv_c[..., :B_NOPE], kv_s[..., :B_NOPE])
        v_all = sample_keys(kv_c[..., B_NOPE:], kv_s[..., B_NOPE:])
        kr_all = sample_keys(cache_b_krope[i], rows_s(kr))
        b_out = attend(sample_q(rows_s(q_mla)), mla_keys(kn_all, kr_all), v_all, ctiles_s, S_BLK, 1, True, s_i)[:, :st]
        mix = jnp.concatenate([a_out.reshape(n_s, -1), b_out.reshape(n_s, -1)], axis=-1)
        outs_even["a_k_s"].append(ak_all[:, -n_a:])
        outs_even["a_v_s"].append(av_all[:, -n_a:])
        outs_even["ckv_s"].append(rows_s(ckv))
        outs_even["kr_s"].append(rows_s(kr))
        return dense(mix.astype(BF16), even_w_out[i].astype(BF16), res=x)

    def diff_lambda(l, i):
        lam_init = 0.8 - 0.6 * math.exp(-0.3 * l)
        lam = (jnp.exp(jnp.sum(c_lambda_q1[i].astype(F32) * c_lambda_k1[i].astype(F32)))
               - jnp.exp(jnp.sum(c_lambda_q2[i].astype(F32) * c_lambda_k2[i].astype(F32))) + lam_init)
        return lam_init, lam

    def odd_prompt(x, l, i):
        lam_init, lam = diff_lambda(l, i)
        st_, cqt, kc, cvt, qit, wq, qt = proj_odd(x, norm_attn[l], odd_w_in[i], p_blk)
        wc = C_HEADS * 2 * C_DIM
        ck, cv = st_[:, :wc], st_[:, wc:2 * wc]
        dk, dv, dki = (st_[:, 2 * wc + n * LANES:2 * wc + n * LANES + D_DIM] for n in range(3))
        c_t = flash(cqt, kc, cvt, c_tiles_p, n_tiles=2, has_far=True)
        ki, k, vt = _dsa_keys(dk[None], dv[None], dki[None])
        d_t = dsa(qit, wq, ki, qt, k, vt, d_tiles, i_off=0, n_sel=n_sel_p, out_dtype=BF16)
        for name, arr, shp in (("c_k", ck, (C_HEADS, 2 * C_DIM)), ("c_v", cv, (C_HEADS, 2 * C_DIM)),
                               ("d_k", dk, (D_DIM,)), ("d_v", dv, (D_DIM,)), ("d_ki", dki, (IDX_DIM,))):
            outs_odd[name + "_p"].append(arr.reshape(pb, pt, *shp))
        return out_odd(x, c_t, d_t, lam, c_subln[i], 1.0 - lam_init, odd_w_out[i].astype(BF16))

    def odd_sample(x, l, i):
        lam_init, lam = diff_lambda(l, i)
        h = dense(x, pad_cols(odd_w_in[i]), g=norm_attn[l])
        cq, ck, cv, dq, dk, dv, dqi, dki, dw = _split(h, ODD_SPLITS)
        cq = (cq * (C_DIM ** -0.5 * LOG2E)).reshape(-1, 2 * C_HEADS, C_DIM)
        ck3 = ck.reshape(-1, 2 * C_HEADS, C_DIM)
        cv3 = cv.reshape(-1, C_HEADS, 2 * C_DIM)
        dq = dq.reshape(-1, D_HEADS, D_DIM)
        dqi = dqi.reshape(-1, IDX_HEADS, IDX_DIM)
        dw = dw * ((IDX_HEADS ** -0.5) * (IDX_DIM ** -0.5))

        def diff_combine(o):
            b_, t_, _ = o.shape
            o = o.reshape(b_, t_, C_HEADS, 2, 2 * C_DIM)
            c = o[:, :, :, 0] - lam * o[:, :, :, 1]
            return (_rms(c, c_subln[i].astype(F32)) * (1.0 - lam_init)).reshape(b_, t_, -1)

        rows, tag, nb_, nt_ = rows_s, "_s", sb, st
        ck_all = sample_keys(cache_c_k[i].reshape(sb, past, 2 * C_HEADS, C_DIM), rows(ck3))
        cv_all = sample_keys(cache_c_v[i], rows(cv3))
        c_out = attend(sample_q(rows(cq)), ck_all, cv_all, c_tiles_s, S_BLK, 2, True, s_i)[:, :st]
        d_out = _dsa_call(_pad_rows(rows(dq), DSA_TQ), _pad_rows(rows(dqi), DSA_TQ), _pad_rows(rows(dw), DSA_TQ),
                          sample_keys(cache_d_k[i], rows(dk)), sample_keys(cache_d_v[i], rows(dv)),
                          sample_keys(cache_d_kidx[i], rows(dki)), d_tiles, s_i_dsa, n_sel_s)[:, :st]
        mix = jnp.concatenate([diff_combine(c_out).reshape(nb_ * nt_, -1), d_out.reshape(nb_ * nt_, -1)], axis=-1)
        for name, arr, shp in (("c_k", ck, (C_HEADS, 2 * C_DIM)), ("c_v", cv, (C_HEADS, 2 * C_DIM)),
                               ("d_k", dk, (D_DIM,)), ("d_v", dv, (D_DIM,)), ("d_ki", dki, (IDX_DIM,))):
            outs_odd[name + tag].append(rows(arr).reshape(nb_, nt_, *shp))
        return dense(mix.astype(BF16), odd_w_out[i].astype(BF16), res=x)

    xp, xs = x_prompt.reshape(n_p, d), x_sample.reshape(n_s, d)
    for l in range(depth):
        i = l // 2
        if l % 2 == 0:
            xp, xs = even_prompt(xp, l, i), even_sample(xs, l, i)
        else:
            xp, xs = odd_prompt(xp, l, i), odd_sample(xs, l, i)
        wg, wu, wd = ffn_w_gate[l].astype(BF16), ffn_w_up[l].astype(BF16), ffn_w_down[l].astype(BF16)
        fg = final_norm if l == depth - 1 else None
        xp, xs = ffn(xp, norm_ffn[l], wg, wu, wd, final_g=fg), ffn(xs, norm_ffn[l], wg, wu, wd, final_g=fg)

    se = {k: jnp.stack(v, axis=0) for k, v in outs_even.items()}
    so = {k: jnp.stack(v, axis=0) for k, v in outs_odd.items()}
    return (xp.reshape(pb, pt, d), xs.reshape(sb, st, d), se["a_k_p"], se["a_k_s"], se["a_v_p"], se["a_v_s"],
            se["ckv_p"], se["ckv_s"], se["kr_p"], se["kr_s"],
            so["c_k_p"], so["c_k_s"], so["c_v_p"], so["c_v_s"],
            so["d_k_p"], so["d_k_s"], so["d_v_p"], so["d_v_s"], so["d_ki_p"], so["d_ki_s"])
```

```python
import functools
import math

import numpy as np
import jax
import jax.numpy as jnp
from jax import lax
from jax.experimental import pallas as pl
from jax.experimental.pallas import tpu as pltpu

F32, BF16, I32 = jnp.float32, jnp.bfloat16, jnp.int32

D_MODEL = 1024
CHUNK = 64
EPS = 1e-6
A_HEADS, A_DIM, A_LEFT_CHUNKS, A_MAX_REL = 8, 64, 8, 128
B_HEADS, B_NOPE, B_ROPE, B_V = 8, 64, 32, 64
Q_LORA, KV_LORA = 256, 128
ROPE_THETA = 10000.0
C_HEADS, C_DIM = 4, 64
D_HEADS, D_DIM = 8, 64
IDX_HEADS, IDX_DIM = 8, 64
TOPK_MAX = 256
T5_BUCKETS, T5_MAX_DIST = 32, 128
EVEN_SPLITS = (512, 512, 512, Q_LORA, KV_LORA, B_ROPE)
ODD_SPLITS = (512, 512, 512, 512, D_DIM, D_DIM, 512, IDX_DIM, IDX_HEADS)

LANES = 128
SUBLANES = 8
VMEM_LIMIT = 56 * 1024 * 1024
MASKED = -1e30
UNSELECTED = -2e30
M_INIT = -1e30
SUM_MAX, SUM_MIN = 1e18, 1e-18
LOG2E = math.log2(math.e)
INT_MIN = -(2 ** 31)
INT_MAX = 2 ** 31 - 1

DSA_TQ, DSA_TK = 128, 256
COUNT_ROWS = 8 * SUBLANES
FAR_UNROLL = 4
DSA_FAR_UNROLL = 4
NORM_ROWS = 16
DSA_SCORE_UNROLL = 4


def _cparams(n_axes):
    return pltpu.CompilerParams(dimension_semantics=("arbitrary",) * n_axes, vmem_limit_bytes=VMEM_LIMIT)


def _rms(x, g):
    return x * lax.rsqrt(jnp.mean(x * x, axis=-1, keepdims=True) + EPS) * g


def _dense_kernel(*refs, norm, res, emit_h):
    it = iter(refs)
    x_ref = next(it)
    g_ref = next(it) if norm else None
    w_ref = next(it)
    r_ref = next(it) if res else None
    o_ref = next(it)
    h_ref = next(it) if emit_h else None
    x = x_ref[...]
    if norm:
        x = _rms(x.astype(F32), g_ref[...])
        if emit_h:
            h_ref[...] = x
    acc = jnp.dot(x.astype(BF16), w_ref[...], preferred_element_type=F32)
    if res:
        acc = acc + r_ref[...]
    o_ref[...] = acc


def dense(x, w, g=None, res=None, emit_h=False, tm=512):
    m, k = x.shape
    n = w.shape[1]
    assert m % tm == 0 and n % LANES == 0
    norm = g is not None
    args, specs = [x], [pl.BlockSpec((tm, k), lambda i: (i, 0))]
    if norm:
        args.append(g.reshape(1, k).astype(F32))
        specs.append(pl.BlockSpec((1, k), lambda i: (0, 0)))
    args.append(w)
    specs.append(pl.BlockSpec((k, n), lambda i: (0, 0)))
    if res is not None:
        args.append(res)
        specs.append(pl.BlockSpec((tm, n), lambda i: (i, 0)))
    out_shape = [jax.ShapeDtypeStruct((m, n), F32)]
    out_specs = [pl.BlockSpec((tm, n), lambda i: (i, 0))]
    if emit_h:
        out_shape.append(jax.ShapeDtypeStruct((m, k), F32))
        out_specs.append(pl.BlockSpec((tm, k), lambda i: (i, 0)))
    outs = pl.pallas_call(
        functools.partial(_dense_kernel, norm=norm, res=res is not None, emit_h=emit_h),
        grid=(m // tm,), in_specs=specs, out_specs=out_specs, out_shape=out_shape,
        compiler_params=_cparams(1), name="dense")(*args)
    return outs if emit_h else outs[0]


def _ffn_kernel(x_ref, g_ref, wg_ref, wu_ref, wd_ref, fg_ref, o_ref, h_sc, acc_sc, *, final):
    f = pl.program_id(1)

    @pl.when(f == 0)
    def _():
        x = x_ref[...]
        h_sc[...] = _rms(x, g_ref[...]).astype(BF16)
        acc_sc[...] = x

    h = h_sc[...]
    gate = jnp.dot(h, wg_ref[...], preferred_element_type=F32)
    up = jnp.dot(h, wu_ref[...], preferred_element_type=F32)
    a = (gate * jax.nn.sigmoid(gate) * up).astype(BF16)
    acc_sc[...] += jnp.dot(a, wd_ref[...], preferred_element_type=F32)

    @pl.when(f == pl.num_programs(1) - 1)
    def _():
        y = acc_sc[...]
        if final:
            y = _rms(y, fg_ref[...])
        o_ref[...] = y


def ffn(x, g, wg, wu, wd, final_g=None, tm=512, nf=2):
    m, d = x.shape
    hid = wg.shape[1]
    tf = hid // nf
    assert m % tm == 0 and hid % nf == 0 and tf % LANES == 0
    final = final_g is not None
    fg = (final_g if final else g).reshape(1, d).astype(F32)
    return pl.pallas_call(
        functools.partial(_ffn_kernel, final=final),
        grid=(m // tm, nf),
        in_specs=[pl.BlockSpec((tm, d), lambda i, f: (i, 0)),
                  pl.BlockSpec((1, d), lambda i, f: (0, 0)),
                  pl.BlockSpec((d, tf), lambda i, f: (0, f)),
                  pl.BlockSpec((d, tf), lambda i, f: (0, f)),
                  pl.BlockSpec((tf, d), lambda i, f: (f, 0)),
                  pl.BlockSpec((1, d), lambda i, f: (0, 0))],
        out_specs=pl.BlockSpec((tm, d), lambda i, f: (i, 0)),
        out_shape=jax.ShapeDtypeStruct((m, d), F32),
        scratch_shapes=[pltpu.VMEM((tm, d), BF16), pltpu.VMEM((tm, d), F32)],
        compiler_params=_cparams(2), name="ffn")(x, g.reshape(1, d).astype(F32), wg, wu, wd, fg)


def _flash_kernel(qt_ref, k_ref, vt_ref, tile_ref, o_ref, m_sc, l_sc, acc_sc, *, n_tiles, has_far, i_off):
    i = pl.program_id(2) + i_off
    qt = qt_ref[0, 0, 0]
    j_lo = 0 if has_far else jnp.maximum(i - (n_tiles - 1), 0)

    def raw(j):
        return jnp.dot(k_ref[0, 0, j], qt, preferred_element_type=F32)

    def scores(j):
        return raw(j) + tile_ref[0, jnp.minimum(i - j, n_tiles)]

    def reset():
        l_sc[...] = jnp.zeros(l_sc.shape, F32)
        acc_sc[...] = jnp.zeros(acc_sc.shape, F32)

    def accumulate(j, p):
        l_sc[...] += jnp.sum(p, axis=0, keepdims=True)
        acc_sc[...] += jnp.dot(vt_ref[0, 0, j], p.astype(BF16), preferred_element_type=F32)

    reset()
    s_diag = raw(i) + tile_ref[0, 0]
    m = jnp.max(s_diag, axis=0, keepdims=True)
    accumulate(i, jnp.exp2(s_diag - m))
    if has_far:
        m_far = m - tile_ref[0, n_tiles, 0:1, :]

        n_far = jnp.maximum(i - (n_tiles - 1), 0)

        def far_group(jj, carry):
            ps = [jnp.exp2(raw(FAR_UNROLL * jj + u) - m_far) for u in range(FAR_UNROLL)]
            for u in range(FAR_UNROLL):
                accumulate(FAR_UNROLL * jj + u, ps[u])
            return carry

        def far_single(j, carry):
            accumulate(j, jnp.exp2(raw(j) - m_far))
            return carry

        n_grouped = (n_far // FAR_UNROLL) * FAR_UNROLL
        lax.fori_loop(0, n_far // FAR_UNROLL, far_group, 0)
        lax.fori_loop(n_grouped, n_far, far_single, 0)
    for t in range(1, n_tiles):
        @pl.when(i - t >= 0)
        def _(t=t):
            accumulate(i - t, jnp.exp2(raw(i - t) + tile_ref[0, t] - m))

    l = l_sc[...]
    in_range = (jnp.max(l) < SUM_MAX) & (jnp.min(l) > SUM_MIN)

    @pl.when(jnp.logical_not(in_range))
    def _():
        reset()
        m_sc[...] = jnp.full(m_sc.shape, M_INIT, F32)

        def body(j, carry):
            s = scores(j)
            m_prev = m_sc[...]
            m_new = jnp.maximum(m_prev, jnp.max(s, axis=0, keepdims=True))
            alpha = jnp.exp2(m_prev - m_new)
            p = jnp.exp2(s - m_new)
            l_sc[...] = alpha * l_sc[...] + jnp.sum(p, axis=0, keepdims=True)
            acc_sc[...] = alpha * acc_sc[...] + jnp.dot(vt_ref[0, 0, j], p.astype(BF16), preferred_element_type=F32)
            m_sc[...] = m_new
            return carry

        lax.fori_loop(j_lo, i + 1, body, 0)

    o_ref[0, 0, 0] = (acc_sc[...] / l_sc[...]).astype(o_ref.dtype)


def flash(qt, k, vt, tiles, *, n_tiles, has_far, i_off=0, out_dtype=F32):
    b, g, nq, dq, blk = qt.shape
    gk, nb = k.shape[1], k.shape[2]
    gv, dv = vt.shape[1], vt.shape[3]
    gb = tiles.shape[0]
    assert tiles.shape[1] == n_tiles + 1
    kdiv, vdiv, bdiv = g // gk, g // gv, g // gb
    return pl.pallas_call(
        functools.partial(_flash_kernel, n_tiles=n_tiles, has_far=has_far, i_off=i_off),
        grid=(b, g, nq),
        in_specs=[pl.BlockSpec((1, 1, 1, dq, blk), lambda bi, gi, i: (bi, gi, i, 0, 0)),
                  pl.BlockSpec((1, 1, nb, blk, dq), lambda bi, gi, i: (bi, gi // kdiv, 0, 0, 0)),
                  pl.BlockSpec((1, 1, nb, dv, blk), lambda bi, gi, i: (bi, gi // vdiv, 0, 0, 0)),
                  pl.BlockSpec((1, n_tiles + 1, blk, blk), lambda bi, gi, i: (gi // bdiv, 0, 0, 0))],
        out_specs=pl.BlockSpec((1, 1, 1, dv, blk), lambda bi, gi, i: (bi, gi, i, 0, 0)),
        out_shape=jax.ShapeDtypeStruct((b, g, nq, dv, blk), out_dtype),
        scratch_shapes=[pltpu.VMEM((1, blk), F32), pltpu.VMEM((1, blk), F32), pltpu.VMEM((dv, blk), F32)],
        compiler_params=_cparams(3), name="flash")(qt, k, vt, tiles)


def _dsa_kernel(qit_ref, w_ref, ki_ref, qt_ref, k_ref, vt_ref, tile_ref, o_ref,
                key_sc, kb_sc, m_sc, acc_sc, *, i_off, n_sel, idx_bits):
    tq, tk, nh = DSA_TQ, DSA_TK, D_HEADS
    i = pl.program_id(1) + i_off
    nkb = (i * tq) // tk + 1
    krow = lax.broadcasted_iota(I32, (tk, tq), 0)
    qcol = lax.broadcasted_iota(I32, (tk, tq), 1)
    q_chunk = jnp.right_shift(i * tq + qcol, 6)

    def head(x, h):
        return x[:, h * tq:(h + 1) * tq]

    qit = qit_ref[0, 0]
    w = w_ref[0, 0]

    def score_block(j, mask_future):
        lg = jnp.maximum(jnp.dot(ki_ref[0, j], qit, preferred_element_type=F32), 0.0) * w
        sc = head(lg, 0)
        for h in range(1, IDX_HEADS):
            sc = sc + head(lg, h)
        sc = jnp.where(sc == 0.0, 0.0, sc)
        bits = pltpu.bitcast(sc, I32)
        key = bits ^ (jnp.right_shift(bits, 31) & INT_MAX)
        high = pltpu.bitcast(bits & -(2 ** 16), F32)
        if mask_future:
            adm = jnp.right_shift(j * tk + krow, 6) <= q_chunk
            key = jnp.where(adm, key, INT_MIN)
            high = jnp.where(adm, high, jnp.nan)
        key_sc[j] = key
        kb_sc[j] = high.astype(BF16)

    def score_group(jj, carry):
        for u in range(DSA_SCORE_UNROLL):
            score_block(DSA_SCORE_UNROLL * jj + u, False)
        return carry

    def score_single(j, carry):
        score_block(j, False)
        return carry

    n_full = nkb - 1
    lax.fori_loop(0, n_full // DSA_SCORE_UNROLL, score_group, 0)
    lax.fori_loop((n_full // DSA_SCORE_UNROLL) * DSA_SCORE_UNROLL, n_full, score_single, 0)
    score_block(n_full, True)

    def count(pred_fn):
        def body(j, acc):
            ind = jnp.where(pred_fn(key_sc[j], j * tk + krow), 1.0, 0.0)
            return acc + jnp.sum(ind.reshape(tk // COUNT_ROWS, COUNT_ROWS, tq), axis=0)
        acc = lax.fori_loop(0, nkb, body, jnp.zeros((COUNT_ROWS, tq), F32))
        return jnp.sum(acc, axis=0, keepdims=True)

    target = float(n_sel)
    qpos = i * tq + lax.broadcasted_iota(I32, (1, tq), 1)
    n_adm = ((jnp.right_shift(qpos, 6) + 1) * CHUNK).astype(F32)
    real = n_adm > target

    def unsettled(c_thr):
        settled = (c_thr == target) | jnp.logical_not(real)
        return jnp.min(jnp.where(settled, 1.0, 0.0)) < 1.0

    one_h, zero_h = jnp.ones((tk, tq), BF16), jnp.zeros((tk, tq), BF16)

    def count_high(cand_h):
        def body(j, acc):
            ind = jnp.where(kb_sc[j] >= cand_h, one_h, zero_h)
            return acc + sum(ind[r * COUNT_ROWS:(r + 1) * COUNT_ROWS] for r in range(tk // COUNT_ROWS))
        acc = lax.fori_loop(0, nkb, body, jnp.zeros((COUNT_ROWS, tq), BF16))
        return jnp.sum(acc.astype(F32), axis=0, keepdims=True)

    def high_bit(state):
        thr_h, c_thr, b = state
        cand = thr_h + jnp.left_shift(jnp.int32(1), 15 - b)
        pattern = cand ^ (jnp.right_shift(cand, 15) & 0x7FFF)
        cand_h = pltpu.bitcast(jnp.left_shift(pattern, 16), F32).astype(BF16)
        cnt = count_high(cand_h)
        ge = cnt >= target
        return jnp.where(ge, cand, thr_h), jnp.where(ge, cnt, c_thr), b + 1

    thr_h, c_thr, _ = lax.while_loop(lambda s: (s[2] < 16) & unsettled(s[1]), high_bit,
                                     (jnp.full((1, tq), -(2 ** 15), I32), n_adm, jnp.int32(0)))

    def low_bit(state):
        thr, c_thr, b = state
        cand = thr + jnp.left_shift(jnp.int32(1), 31 - b)
        cnt = count(lambda k, idx: k >= cand)
        ge = cnt >= target
        return jnp.where(ge, cand, thr), jnp.where(ge, cnt, c_thr), b + 1

    thr, c_thr, _ = lax.while_loop(lambda s: (s[2] < 32) & unsettled(s[1]), low_bit,
                                   (jnp.left_shift(thr_h, 16), c_thr, jnp.int32(16)))

    def tie_search():
        need = target - count(lambda k, idx: k > thr)

        def index_bit(b, y):
            cand = y + jnp.left_shift(jnp.int32(1), idx_bits - 1 - b)
            cnt = count(lambda k, idx: (k == thr) & (idx < cand))
            return jnp.where(cnt < need, cand, y)
        return lax.fori_loop(0, idx_bits, index_bit, jnp.zeros((1, tq), I32))

    has_ties = jnp.max(jnp.where((c_thr > target) & real, 1.0, 0.0)) > 0.0
    last = lax.cond(has_ties, tie_search, lambda: jnp.full((1, tq), INT_MAX, I32))
    last = jnp.where(real, last, -1)

    qt = qt_ref[0, 0]
    n_near = 3

    def raw(j):
        return jnp.dot(k_ref[0, j], qt, preferred_element_type=F32)

    def tile_of(j):
        return jnp.minimum((i * tq - j * tk) // tq, n_near)

    def select(j, s):
        key = key_sc[j]
        sel = (key > thr) | ((key == thr) & (j * tk + krow <= last))
        selb = jnp.where(sel, 0.0, UNSELECTED)
        return jnp.concatenate([head(s, h) + selb for h in range(nh)], axis=1)

    dv = o_ref.shape[2]
    def reset():
        acc_sc[...] = jnp.zeros(acc_sc.shape, F32)

    def accumulate(j, p):
        acc_sc[...] += jnp.dot(vt_ref[0, j], p.astype(BF16), preferred_element_type=F32)

    reset()
    s_last = raw(nkb - 1) + tile_ref[tile_of(nkb - 1)]
    m = jnp.max(s_last, axis=0, keepdims=True)
    accumulate(nkb - 1, jnp.exp2(select(nkb - 1, s_last) - m))
    m_far = m - tile_ref[n_near, 0:1, :]
    n_far = jnp.maximum(i - 1, 0) // 2

    def far_group(jj, carry):
        ps = [jnp.exp2(select(DSA_FAR_UNROLL * jj + u, raw(DSA_FAR_UNROLL * jj + u)) - m_far)
              for u in range(DSA_FAR_UNROLL)]
        for u in range(DSA_FAR_UNROLL):
            accumulate(DSA_FAR_UNROLL * jj + u, ps[u])
        return carry

    def far_single(j, carry):
        accumulate(j, jnp.exp2(select(j, raw(j)) - m_far))
        return carry

    def near(j, carry):
        accumulate(j, jnp.exp2(select(j, raw(j) + tile_ref[tile_of(j)]) - m))
        return carry

    n_grouped = (n_far // DSA_FAR_UNROLL) * DSA_FAR_UNROLL
    lax.fori_loop(0, n_far // DSA_FAR_UNROLL, far_group, 0)
    lax.fori_loop(n_grouped, n_far, far_single, 0)
    lax.fori_loop(n_far, nkb - 1, near, 0)

    l = acc_sc[dv:dv + 1, :]
    in_range = (jnp.max(l) < SUM_MAX) & (jnp.min(l) > SUM_MIN)

    @pl.when(jnp.logical_not(in_range))
    def _():
        reset()
        m_sc[...] = jnp.full(m_sc.shape, M_INIT, F32)

        def attend(j, carry):
            s = select(j, raw(j) + tile_ref[tile_of(j)])
            m_prev = m_sc[...]
            m_new = jnp.maximum(m_prev, jnp.max(s, axis=0, keepdims=True))
            alpha = jnp.exp2(m_prev - m_new)
            p = jnp.exp2(s - m_new)
            acc_sc[...] = alpha * acc_sc[...] + jnp.dot(vt_ref[0, j], p.astype(BF16), preferred_element_type=F32)
            m_sc[...] = m_new
            return carry

        lax.fori_loop(0, nkb, attend, 0)

    o_ref[0, 0] = (acc_sc[0:dv, :] / acc_sc[dv:dv + 1, :]).astype(o_ref.dtype)


def dsa(qit, w, ki, qt, k, vt, tiles, *, i_off, n_sel, out_dtype=F32):
    b, nq, kdim, cols = qit.shape
    nkb = ki.shape[1]
    dk, dva = k.shape[-1], vt.shape[2]
    dv = dva - NORM_ROWS
    idx_bits = max(1, (nkb * DSA_TK - 1).bit_length())
    assert (DSA_TK // COUNT_ROWS) * nkb <= 256
    return pl.pallas_call(
        functools.partial(_dsa_kernel, i_off=i_off, n_sel=n_sel, idx_bits=idx_bits),
        grid=(b, nq),
        in_specs=[pl.BlockSpec((1, 1, kdim, cols), lambda bi, i: (bi, i, 0, 0)),
                  pl.BlockSpec((1, 1, 1, cols), lambda bi, i: (bi, i, 0, 0)),
                  pl.BlockSpec((1, nkb, DSA_TK, kdim), lambda bi, i: (bi, 0, 0, 0)),
                  pl.BlockSpec((1, 1, dk, cols), lambda bi, i: (bi, i, 0, 0)),
                  pl.BlockSpec((1, nkb, DSA_TK, dk), lambda bi, i: (bi, 0, 0, 0)),
                  pl.BlockSpec((1, nkb, dva, DSA_TK), lambda bi, i: (bi, 0, 0, 0)),
                  pl.BlockSpec((4, DSA_TK, cols), lambda bi, i: (0, 0, 0))],
        out_specs=pl.BlockSpec((1, 1, dv, cols), lambda bi, i: (bi, i, 0, 0)),
        out_shape=jax.ShapeDtypeStruct((b, nq, dv, cols), out_dtype),
        scratch_shapes=[pltpu.VMEM((nkb, DSA_TK, DSA_TQ), I32), pltpu.VMEM((nkb, DSA_TK, DSA_TQ), BF16),
                        pltpu.VMEM((1, cols), F32), pltpu.VMEM((dva, cols), F32)],
        compiler_params=_cparams(2), name="dsa")(qit, w, ki, qt, k, vt, tiles)


_NT = (((1,), (1,)), ((), ()))


def _heads(x, n, d):
    return x.reshape(n, d, x.shape[-1])


def _proj_even_kernel(x_ref, g_ref, w1_ref, w2t_ref, w3_ref, qn_ref, wuqt_ref, kvn_ref, wkb_ref, e1_ref, e2_ref,
                      wuvt_ref, cosn_ref, sinn_ref, cost_ref, sint_ref,
                      akv_ref, ckv_ref, y1_ref, y2_ref, qta_ref, ka_ref, vta_ref, qtb_ref, kb_ref, vtb_ref):
    nh, da, rope_half = A_HEADS, A_DIM, B_ROPE // 2
    h = _rms(x_ref[...], g_ref[...]).astype(BF16)
    h1 = jnp.dot(h, w1_ref[...], preferred_element_type=F32)
    akv_ref[...] = h1[:, :2 * nh * da]
    t2 = lax.dot_general(w2t_ref[...], h, _NT, preferred_element_type=F32)
    qta_ref[0, :, 0] = _heads(t2[:nh * da], nh, da).astype(BF16)
    vta_ref[0, :, 0] = _heads(t2[nh * da:], nh, da).astype(BF16)
    k3 = jnp.dot(h, w3_ref[...], preferred_element_type=F32)
    for hd in range(nh):
        ka_ref[0, hd, 0] = k3[:, hd * LANES:hd * LANES + da].astype(BF16)

    c0 = 2 * nh * da
    cqn = _rms(h1[:, c0:c0 + Q_LORA], qn_ref[...]).astype(BF16)
    tq = lax.dot_general(wuqt_ref[...], cqn, _NT, preferred_element_type=F32)
    n_nope, n_rope = B_HEADS * B_NOPE, B_HEADS * rope_half
    x1, x2 = tq[n_nope:n_nope + n_rope], tq[n_nope + n_rope:]
    ct, st = cost_ref[0], sint_ref[0]
    y1t, y2t = x1 * ct - x2 * st, x1 * st + x2 * ct
    for hd in range(B_HEADS):
        qtb_ref[0, hd, 0] = jnp.concatenate(
            [tq[hd * B_NOPE:(hd + 1) * B_NOPE], y1t[hd * rope_half:(hd + 1) * rope_half],
             y2t[hd * rope_half:(hd + 1) * rope_half]], axis=0).astype(BF16)

    c1 = c0 + Q_LORA
    ckv = _rms(h1[:, c1:c1 + KV_LORA], kvn_ref[...])
    ckv_ref[...] = ckv
    ckvb = ckv.astype(BF16)
    k1, k2 = h1[:, c1 + KV_LORA:c1 + KV_LORA + LANES], h1[:, c1 + KV_LORA + LANES:]
    cn, sn = cosn_ref[...], sinn_ref[...]
    y1, y2 = k1 * cn - k2 * sn, k1 * sn + k2 * cn
    y1_ref[...] = y1
    y2_ref[...] = y2
    kb = (jnp.dot(ckvb, wkb_ref[...], preferred_element_type=F32)
          + jnp.dot(y1.astype(BF16), e1_ref[...], preferred_element_type=F32)
          + jnp.dot(y2.astype(BF16), e2_ref[...], preferred_element_type=F32))
    for hd in range(B_HEADS):
        kb_ref[0, hd, 0] = kb[:, hd * LANES:hd * LANES + B_NOPE + B_ROPE].astype(BF16)
    tv = lax.dot_general(wuvt_ref[...], ckvb, _NT, preferred_element_type=F32)
    vtb_ref[0, :, 0] = _heads(tv, B_HEADS, B_V).astype(BF16)


def proj_even(x, g, w_in, q_norm, w_uq, kv_norm, w_ukv, cos, sin, blk):
    p, d = x.shape
    nq = p // blk
    nh, da, half = A_HEADS, A_DIM, B_ROPE // 2
    s_a = A_DIM ** -0.5 * LOG2E
    s_b = (B_NOPE + B_ROPE) ** -0.5 * LOG2E
    aq_w, ak_w, av_w, cq_w, ckv_w, kr_w = _split(w_in, EVEN_SPLITS)

    def lane_pad(w, n=LANES):
        return jnp.pad(w, ((0, 0), (0, n - w.shape[1])))

    def head_chunks(w, dh):
        k = w.shape[0]
        return jnp.pad(w.reshape(k, -1, dh), ((0, 0), (0, 0), (0, LANES - dh))).reshape(k, -1)

    w1 = jnp.concatenate([ak_w, av_w, cq_w, ckv_w, lane_pad(kr_w[:, :half]), lane_pad(kr_w[:, half:])], axis=1).astype(BF16)
    w2t = jnp.concatenate([aq_w * s_a, av_w], axis=1).T.astype(BF16)
    w3 = head_chunks(ak_w, da).astype(BF16)
    uq = (w_uq * s_b).reshape(Q_LORA, B_HEADS, B_NOPE + B_ROPE)
    wuqt = jnp.concatenate([uq[:, :, :B_NOPE].reshape(Q_LORA, -1), uq[:, :, B_NOPE:B_NOPE + half].reshape(Q_LORA, -1),
                            uq[:, :, B_NOPE + half:].reshape(Q_LORA, -1)], axis=1).T.astype(BF16)
    ukv = w_ukv.reshape(KV_LORA, B_HEADS, B_NOPE + B_V)
    wkb = head_chunks(ukv[:, :, :B_NOPE].reshape(KV_LORA, -1), B_NOPE).astype(BF16)
    wuvt = ukv[:, :, B_NOPE:].reshape(KV_LORA, -1).T.astype(BF16)
    eye = np.zeros((2, LANES, B_HEADS, LANES), np.float32)
    for r in range(half):
        eye[0, r, :, B_NOPE + r] = 1.0
        eye[1, r, :, B_NOPE + half + r] = 1.0
    e1, e2 = (jnp.asarray(e.reshape(LANES, B_HEADS * LANES), BF16) for e in eye)
    cosn, sinn = lane_pad(cos), lane_pad(sin)
    cost = jnp.tile(cos.reshape(nq, blk, half).transpose(0, 2, 1), (1, B_HEADS, 1))
    sint = jnp.tile(sin.reshape(nq, blk, half).transpose(0, 2, 1), (1, B_HEADS, 1))

    def full(a):
        return pl.BlockSpec(a.shape, lambda i: (0,) * a.ndim)

    def rows(n):
        return pl.BlockSpec((blk, n), lambda i: (i, 0))

    def tile3(a):
        return pl.BlockSpec((1,) + a.shape[1:], lambda i: (i, 0, 0))

    def per_block(g_, a, b_):
        return (jax.ShapeDtypeStruct((1, g_, nq, a, b_), BF16),
                pl.BlockSpec((1, g_, 1, a, b_), lambda i: (0, 0, i, 0, 0)))

    consts = [g.reshape(1, d).astype(F32), w1, w2t, w3, q_norm.reshape(1, -1).astype(F32), wuqt,
              kv_norm.reshape(1, -1).astype(F32), wkb, e1, e2, wuvt]
    blocks = [per_block(nh, da, blk), per_block(nh, blk, da), per_block(nh, da, blk),
              per_block(B_HEADS, B_NOPE + B_ROPE, blk), per_block(B_HEADS, blk, B_NOPE + B_ROPE),
              per_block(B_HEADS, B_V, blk)]
    out_shape = [jax.ShapeDtypeStruct((p, 2 * nh * da), F32), jax.ShapeDtypeStruct((p, KV_LORA), F32),
                 jax.ShapeDtypeStruct((p, LANES), F32), jax.ShapeDtypeStruct((p, LANES), F32)] + [s for s, _ in blocks]
    out_specs = [rows(2 * nh * da), rows(KV_LORA), rows(LANES), rows(LANES)] + [s for _, s in blocks]
    return pl.pallas_call(
        _proj_even_kernel, grid=(nq,),
        in_specs=[rows(d)] + [full(a) for a in consts] + [rows(LANES), rows(LANES), tile3(cost), tile3(sint)],
        out_specs=out_specs, out_shape=out_shape,
        compiler_params=_cparams(1), name="proj_even")(x, *consts, cosn, sinn, cost, sint)


def _proj_odd_kernel(x_ref, g_ref, w1_ref, w2t_ref, w3_ref,
                     st_ref, cqt_ref, kc_ref, cvt_ref, qit_ref, wq_ref, qt_ref):
    nmap, width = 2 * C_HEADS, C_HEADS * 2 * C_DIM
    h = _rms(x_ref[...], g_ref[...]).astype(BF16)
    st_ref[...] = jnp.dot(h, w1_ref[...], preferred_element_type=F32)
    t2 = lax.dot_general(w2t_ref[...], h, _NT, preferred_element_type=F32)
    cqt_ref[0, :, 0] = _heads(t2[:width], nmap, C_DIM).astype(BF16)
    cvt_ref[0, :, 0] = _heads(t2[width:2 * width], C_HEADS, 2 * C_DIM).astype(BF16)
    k3 = jnp.dot(h, w3_ref[...], preferred_element_type=F32)
    for m in range(nmap):
        kc_ref[0, m, 0] = k3[:, m * LANES:m * LANES + C_DIM].astype(BF16)

    dqt = t2[2 * width:2 * width + D_HEADS * D_DIM]
    dqit = t2[2 * width + D_HEADS * D_DIM:2 * width + D_HEADS * (D_DIM + IDX_DIM)]
    dwt = t2[2 * width + D_HEADS * (D_DIM + IDX_DIM):]

    def stack(x, rows_per_head, n_rows, b):
        cols = slice(b * DSA_TQ, (b + 1) * DSA_TQ)
        return jnp.concatenate([x[hd * rows_per_head:hd * rows_per_head + n_rows, cols] for hd in range(D_HEADS)], axis=1)

    for b in range(x_ref.shape[0] // DSA_TQ):
        qt_ref[0, b] = stack(dqt, D_DIM, D_DIM, b).astype(BF16)
        qi = stack(dqit, IDX_DIM, IDX_DIM, b)
        hi = qi.astype(BF16)
        lo = (qi - hi.astype(F32)).astype(BF16)
        qit_ref[0, b] = jnp.concatenate([hi, lo, hi], axis=0)
        wq_ref[0, b] = stack(dwt, SUBLANES, 1, b)


def proj_odd(x, g, w_in, blk):
    p, d = x.shape
    nq, nq_dsa = p // blk, p // DSA_TQ
    s_c = C_DIM ** -0.5 * LOG2E
    s_d = D_DIM ** -0.5 * LOG2E
    s_w = (IDX_HEADS ** -0.5) * (IDX_DIM ** -0.5)
    cq_w, ck_w, cv_w, dq_w, dk_w, dv_w, dqi_w, dki_w, dw_w = _split(w_in, ODD_SPLITS)

    def lane_pad(w):
        return jnp.pad(w, ((0, 0), (0, LANES - w.shape[1])))

    w1 = jnp.concatenate([ck_w, cv_w, lane_pad(dk_w), lane_pad(dv_w), lane_pad(dki_w)], axis=1).astype(BF16)
    dw_rows = jnp.pad((dw_w * s_w)[:, :, None], ((0, 0), (0, 0), (0, SUBLANES - 1))).reshape(d, -1)
    w2t = jnp.concatenate([cq_w * s_c, cv_w, dq_w * s_d, dqi_w, dw_rows], axis=1).T.astype(BF16)
    w3 = jnp.pad(ck_w.reshape(d, -1, C_DIM), ((0, 0), (0, 0), (0, LANES - C_DIM))).reshape(d, -1).astype(BF16)
    consts = [g.reshape(1, d).astype(F32), w1, w2t, w3]
    cols = D_HEADS * DSA_TQ
    per = blk // DSA_TQ

    def per_block(g_, a, b_):
        return (jax.ShapeDtypeStruct((1, g_, nq, a, b_), BF16),
                pl.BlockSpec((1, g_, 1, a, b_), lambda i: (0, 0, i, 0, 0)))

    def per_dsa(a, dtype):
        return (jax.ShapeDtypeStruct((1, nq_dsa, a, cols), dtype), pl.BlockSpec((1, per, a, cols), lambda i: (0, i, 0, 0)))

    blocks = [per_block(2 * C_HEADS, C_DIM, blk), per_block(2 * C_HEADS, blk, C_DIM), per_block(C_HEADS, 2 * C_DIM, blk),
              per_dsa(3 * IDX_DIM, BF16), per_dsa(1, F32), per_dsa(D_DIM, BF16)]
    return pl.pallas_call(
        _proj_odd_kernel, grid=(nq,),
        in_specs=[pl.BlockSpec((blk, d), lambda i: (i, 0))] + [pl.BlockSpec(a.shape, lambda i: (0, 0)) for a in consts],
        out_specs=[pl.BlockSpec((blk, w1.shape[1]), lambda i: (i, 0))] + [s for _, s in blocks],
        out_shape=[jax.ShapeDtypeStruct((p, w1.shape[1]), F32)] + [s for s, _ in blocks],
        compiler_params=_cparams(1), name="proj_odd")(x, *consts)


def _out_odd_kernel(x_ref, c_ref, d_ref, lam_ref, sub_ref, w_ref, o_ref, *, post_scale):
    blk = x_ref.shape[0]
    lam, sub = lam_ref[...], sub_ref[...]
    parts = []
    for hd in range(C_HEADS):
        c = c_ref[0, 2 * hd, 0] - lam * c_ref[0, 2 * hd + 1, 0]
        parts.append(c * lax.rsqrt(jnp.mean(c * c, axis=0, keepdims=True) + EPS) * sub * post_scale)
    for hd in range(D_HEADS):
        parts.append(jnp.concatenate([d_ref[0, b][:, hd * DSA_TQ:(hd + 1) * DSA_TQ]
                                      for b in range(blk // DSA_TQ)], axis=1).astype(F32))
    mix = jnp.concatenate(parts, axis=0).T.astype(BF16)
    o_ref[...] = x_ref[...] + jnp.dot(mix, w_ref[...], preferred_element_type=F32)


def out_odd(x, c_t, d_t, lam, subln, post_scale, w_out):
    p, d = x.shape
    _, nmap, nq, dc, blk = c_t.shape
    per = blk // DSA_TQ
    return pl.pallas_call(
        functools.partial(_out_odd_kernel, post_scale=post_scale), grid=(nq,),
        in_specs=[pl.BlockSpec((blk, d), lambda i: (i, 0)),
                  pl.BlockSpec((1, nmap, 1, dc, blk), lambda i: (0, 0, i, 0, 0)),
                  pl.BlockSpec((1, per) + d_t.shape[2:], lambda i: (0, i, 0, 0)),
                  pl.BlockSpec((1, blk), lambda i: (0, 0)),
                  pl.BlockSpec((dc, 1), lambda i: (0, 0)),
                  pl.BlockSpec(w_out.shape, lambda i: (0, 0))],
        out_specs=pl.BlockSpec((blk, d), lambda i: (i, 0)),
        out_shape=jax.ShapeDtypeStruct((p, d), F32),
        compiler_params=_cparams(1), name="out_odd")(
            x, c_t, d_t, jnp.full((1, blk), lam, F32), subln.reshape(dc, 1).astype(F32), w_out)


def _out_even_kernel(x_ref, a_ref, b_ref, w_ref, o_ref):
    blk = x_ref.shape[0]
    mix_t = jnp.concatenate([a_ref[0, :, 0].reshape(-1, blk), b_ref[0, :, 0].reshape(-1, blk)], axis=0)
    mix = mix_t.astype(F32).T.astype(BF16)
    o_ref[...] = x_ref[...] + jnp.dot(mix, w_ref[...], preferred_element_type=F32)


def out_even(x, a_t, b_t, w_out):
    p, d = x.shape
    _, g_, nq, dv, blk = a_t.shape
    blk_spec = pl.BlockSpec((1, g_, 1, dv, blk), lambda i: (0, 0, i, 0, 0))
    return pl.pallas_call(
        _out_even_kernel, grid=(nq,),
        in_specs=[pl.BlockSpec((blk, d), lambda i: (i, 0)), blk_spec, blk_spec,
                  pl.BlockSpec(w_out.shape, lambda i: (0, 0))],
        out_specs=pl.BlockSpec((blk, d), lambda i: (i, 0)),
        out_shape=jax.ShapeDtypeStruct((p, d), F32),
        compiler_params=_cparams(1), name="out_even")(x, a_t, b_t, w_out)


def _rope_tables(pos):
    half = B_ROPE // 2
    freqs = jnp.power(jnp.float32(ROPE_THETA), -jnp.arange(half, dtype=F32) / half)
    ang = pos.astype(F32)[:, None] * freqs[None, :]
    return jnp.cos(ang), jnp.sin(ang)


def _rope(x, cos, sin):
    half = x.shape[-1] // 2
    shape = (x.shape[0],) + (1,) * (x.ndim - 2) + (half,)
    c, s = cos.reshape(shape), sin.reshape(shape)
    x1, x2 = x[..., :half], x[..., half:]
    return jnp.concatenate([x1 * c - x2 * s, x1 * s + x2 * c], axis=-1)


def _t5_bucket(rel):
    nb = T5_BUCKETS // 2
    max_exact = nb // 2
    ret = jnp.where(rel > 0, nb, 0)
    n = jnp.abs(rel)
    nf = jnp.maximum(n, 1).astype(F32)
    large = max_exact + (jnp.log(nf / max_exact) / math.log(T5_MAX_DIST / max_exact)
                         * (nb - max_exact)).astype(I32)
    large = jnp.minimum(large, nb - 1)
    return ret + jnp.where(n < max_exact, n, large)


def _chunk_diff(tq, tk, n_tiles):
    r = np.arange(tq)[None, None, :]
    c = np.arange(tk)[None, :, None]
    t = np.arange(n_tiles)[:, None, None]
    return c // CHUNK - r // CHUNK - t * (tq // CHUNK)


def _strip_distances(tq, tk, n_tiles):
    y = np.arange(tq + tk)[None, :]
    t = np.arange(n_tiles)[:, None]
    return np.where(y < tq, -y, tq + tk - y) - t * tq


def _toeplitz(strip, tq, tk):
    n = tq + tk
    lead = strip.shape[:-1]
    rows = jnp.broadcast_to(strip[..., None, :], lead + (tk, n)).reshape(lead + (tk * n,))
    return rows[..., :tk * (n - 1)].reshape(lead + (tk, n - 1))[..., :tq]


def band_tiles(relbias, blk):
    n_tiles = (A_LEFT_CHUNKS * CHUNK) // blk + 1
    dchunk = _chunk_diff(blk, blk, n_tiles)
    adm = (dchunk <= 0) & (-dchunk <= A_LEFT_CHUNKS)
    idx = np.clip(-_strip_distances(blk, blk, n_tiles), -A_MAX_REL, A_MAX_REL) + A_MAX_REL
    bias = _toeplitz(relbias.astype(F32)[:, idx] * LOG2E, blk, blk)
    tiles = jnp.where(adm[None], bias, MASKED)
    far = jnp.full((relbias.shape[0], 1, blk, blk), MASKED, F32)
    return jnp.concatenate([tiles, far], axis=1), n_tiles


def causal_tiles(blk):
    t0 = np.where(_chunk_diff(blk, blk, 1) <= 0, 0.0, MASKED).astype(np.float32)
    return jnp.asarray(np.concatenate([t0, np.zeros_like(t0)], axis=0)[None])


def t5_tiles(tab, tq, tk, n_tiles):
    assert n_tiles * tq - tk + 1 >= T5_MAX_DIST
    tab = tab.astype(F32) * LOG2E
    strip = jnp.moveaxis(tab[_t5_bucket(jnp.asarray(_strip_distances(tq, tk, n_tiles), I32))], -1, 0)
    tiles = jnp.where((_chunk_diff(tq, tk, n_tiles) <= 0)[None], _toeplitz(strip, tq, tk), MASKED)
    far = tab[_t5_bucket(jnp.asarray([-(n_tiles * tq + tk)], I32))[0]]
    far = jnp.broadcast_to(far[:, None, None, None], (tab.shape[1], 1, tk, tq))
    return jnp.concatenate([tiles, far], axis=1)


def _row_blocks(x, blk):
    b, t, g, d = x.shape
    return jnp.transpose(x.reshape(b, t // blk, blk, g, d), (0, 3, 1, 2, 4)).astype(BF16)


def _col_blocks(x, blk):
    b, t, g, d = x.shape
    return jnp.transpose(x.reshape(b, t // blk, blk, g, d), (0, 3, 1, 4, 2)).astype(BF16)


def _from_col_blocks(o):
    b, g, nq, d, blk = o.shape
    return jnp.transpose(o, (0, 2, 4, 1, 3)).reshape(b, nq * blk, g * d)


def _pad_rows(x, n, front=0):
    back = n - x.shape[1] - front
    return jnp.pad(x, ((0, 0), (front, back)) + ((0, 0),) * (x.ndim - 2))


def _split(h, sizes):
    out, o = [], 0
    for s in sizes:
        out.append(h[:, o:o + s])
        o += s
    return out


def _split3(x):
    hi = x.astype(BF16)
    lo = (x - hi.astype(F32)).astype(BF16)
    return hi, lo


def _dsa_keys(dk_all, dv_all, dki_all):
    b, nkb = dk_all.shape[0], dk_all.shape[1] // DSA_TK
    kh, kl = _split3(dki_all)
    ki = jnp.concatenate([kh, kh, kl], axis=-1).reshape(b, nkb, DSA_TK, 3 * IDX_DIM)
    k = dk_all.astype(BF16).reshape(b, nkb, DSA_TK, D_DIM)
    vt = jnp.transpose(dv_all.astype(BF16).reshape(b, nkb, DSA_TK, D_DIM), (0, 1, 3, 2))
    norm = jnp.zeros((b, nkb, NORM_ROWS, DSA_TK), BF16).at[:, :, 0].set(1.0)
    return ki, k, jnp.concatenate([vt, norm], axis=2)


def _dsa_call(dq, dqi, dw, dk_all, dv_all, dki_all, tiles, i_off, n_sel):
    b, tq_all = dq.shape[:2]
    nq = tq_all // DSA_TQ

    def stack_t(x):
        d = x.shape[-1]
        return jnp.transpose(x.reshape(b, nq, DSA_TQ, D_HEADS, d), (0, 1, 4, 3, 2)).reshape(b, nq, d, D_HEADS * DSA_TQ)

    qh, ql = _split3(dqi)
    qit = stack_t(jnp.concatenate([qh, ql, qh], axis=-1))
    w = stack_t(dw[..., None])
    qt = stack_t((dq * (D_DIM ** -0.5 * LOG2E)).astype(BF16))
    ki, k, vt = _dsa_keys(dk_all, dv_all, dki_all)
    o = dsa(qit, w, ki, qt, k, vt, tiles, i_off=i_off, n_sel=n_sel)
    o = jnp.transpose(o.reshape(b, nq, D_DIM, D_HEADS, DSA_TQ), (0, 1, 4, 3, 2))
    return o.reshape(b, tq_all, D_HEADS * D_DIM)


P_BLK = 512
S_BLK = 128


def kernel(x_prompt, x_sample, cache_a_k, cache_a_v, cache_b_ckv, cache_b_krope, cache_c_k, cache_c_v, cache_d_k, cache_d_v, cache_d_kidx, t5_table, norm_attn, norm_ffn, final_norm, even_w_in, even_w_out, a_relbias, b_q_norm, b_kv_norm, b_w_uq, b_w_ukv, odd_w_in, odd_w_out, c_lambda_q1, c_lambda_k1, c_lambda_q2, c_lambda_k2, c_subln, ffn_w_gate, ffn_w_up, ffn_w_down):
    pb, pt, d = x_prompt.shape
    sb, st, _ = x_sample.shape
    assert pb == 1 and st == CHUNK
    past = cache_b_ckv.shape[2]
    n_a = cache_a_k.shape[2]
    depth = norm_attn.shape[0]
    n_p, n_s = pb * pt, sb * st
    s_len = past + st
    s_pad = -(-s_len // DSA_TK) * DSA_TK
    s_i = past // S_BLK
    s_i_dsa = past // DSA_TQ
    assert past % S_BLK == 0 and s_pad % S_BLK == 0 and past >= n_a
    p_blk = min(P_BLK, pt)

    cos_p, sin_p = _rope_tables(jnp.arange(pt))
    cos_s, sin_s = _rope_tables(jnp.tile(past + jnp.arange(st), sb))

    def rows_p(a):
        return a.reshape(pb, pt, *a.shape[1:])

    def rows_s(a):
        return a.reshape(sb, st, *a.shape[1:])

    def pad_cols(w):
        n = -(-w.shape[1] // LANES) * LANES
        return jnp.pad(w, ((0, 0), (0, n - w.shape[1]))).astype(BF16)

    def sample_q(a):
        return _pad_rows(a, S_BLK)

    def sample_keys(cache, new, front=0):
        return _pad_rows(jnp.concatenate([cache, new], axis=1), s_pad, front)

    def attend(q, k, v, tiles, blk, n_tiles, has_far, i_off=0):
        o = flash(_col_blocks(q, blk), _row_blocks(k, blk), _col_blocks(v, blk), tiles,
                  n_tiles=n_tiles, has_far=has_far, i_off=i_off)
        return _from_col_blocks(o)

    t5_c, t5_d = t5_table[:, :C_HEADS], t5_table[:, C_HEADS:]
    ctiles_p = causal_tiles(p_blk)
    ctiles_s = causal_tiles(S_BLK)
    c_tiles_p = t5_tiles(t5_c, p_blk, p_blk, 2)
    c_tiles_s = t5_tiles(t5_c, S_BLK, S_BLK, 2)
    d_tiles = t5_tiles(t5_d, DSA_TQ, DSA_TK, 3)
    d_tiles = jnp.transpose(d_tiles, (1, 2, 0, 3)).reshape(4, DSA_TK, D_HEADS * DSA_TQ)
    n_sel_p = min(TOPK_MAX, pt // 4)
    n_sel_s = min(TOPK_MAX, s_len // 4)

    outs_even = {k: [] for k in ("a_k_p", "a_k_s", "a_v_p", "a_v_s", "ckv_p", "ckv_s", "kr_p", "kr_s")}
    outs_odd = {k: [] for k in ("c_k_p", "c_k_s", "c_v_p", "c_v_s", "d_k_p", "d_k_s", "d_v_p", "d_v_s", "d_ki_p", "d_ki_s")}

    def mla_keys(kn, krope):
        return jnp.concatenate([kn, jnp.broadcast_to(krope[:, :, None, :], kn.shape[:3] + (B_ROPE,))], axis=-1)

    def even_prompt(x, l, i):
        a_tiles, nt = band_tiles(a_relbias[i], p_blk)
        akv, ckv, y1, y2, qta, ka, vta, qtb, kb, vtb = proj_even(
            x, norm_attn[l], even_w_in[i], b_q_norm[i], b_w_uq[i], b_kv_norm[i], b_w_ukv[i], cos_p, sin_p, p_blk)
        a_t = flash(qta, ka, vta, a_tiles, n_tiles=nt, has_far=False, out_dtype=BF16)
        b_t = flash(qtb, kb, vtb, ctiles_p, n_tiles=1, has_far=True, out_dtype=BF16)
        half = B_ROPE // 2
        hd = A_HEADS * A_DIM
        outs_even["a_k_p"].append(akv[pt - n_a:, :hd].reshape(pb, n_a, A_HEADS, A_DIM))
        outs_even["a_v_p"].append(akv[pt - n_a:, hd:].reshape(pb, n_a, A_HEADS, A_DIM))
        outs_even["ckv_p"].append(rows_p(ckv))
        outs_even["kr_p"].append(rows_p(jnp.concatenate([y1[:, :half], y2[:, :half]], axis=-1)))
        return out_even(x, a_t, b_t, even_w_out[i].astype(BF16))

    def even_sample(x, l, i):
        h = dense(x, pad_cols(even_w_in[i]), g=norm_attn[l])
        aq, ak, av, cq, ckv_raw, kr_raw = _split(h, EVEN_SPLITS)
        qb = dense(cq, b_w_uq[i].astype(BF16), g=b_q_norm[i]).reshape(-1, B_HEADS, B_NOPE + B_ROPE)
        kv_new, ckv = dense(ckv_raw, b_w_ukv[i].astype(BF16), g=b_kv_norm[i], emit_h=True)
        kr = _rope(kr_raw, cos_s, sin_s)
        q_mla = jnp.concatenate([qb[..., :B_NOPE], _rope(qb[..., B_NOPE:], cos_s, sin_s)], axis=-1)
        q_mla = q_mla * ((B_NOPE + B_ROPE) ** -0.5 * LOG2E)
        kv_s = rows_s(kv_new.reshape(-1, B_HEADS, B_NOPE + B_V))
        aq = (aq * (A_DIM ** -0.5 * LOG2E)).reshape(-1, A_HEADS, A_DIM)
        ak = ak.reshape(-1, A_HEADS, A_DIM)
        av = av.reshape(-1, A_HEADS, A_DIM)
        a_tiles, nt = band_tiles(a_relbias[i], S_BLK)
        ak_all = jnp.concatenate([cache_a_k[i], rows_s(ak)], axis=1)
        av_all = jnp.concatenate([cache_a_v[i], rows_s(av)], axis=1)
        a_out = attend(sample_q(rows_s(aq)), _pad_rows(ak_all, s_pad, past - n_a),
                       _pad_rows(av_all, s_pad, past - n_a), a_tiles, S_BLK, nt, False, s_i)[:, :st]
        kv_c = dense(cache_b_ckv[i].reshape(sb * past, KV_LORA), b_w_ukv[i].astype(BF16))
        kv_c = kv_c.reshape(sb, past, B_HEADS, B_NOPE + B_V)
        kn_all = sample_keys(kv_c[..., :B_NOPE], kv_s[..., :B_NOPE])
        v_all = sample_keys(kv_c[..., B_NOPE:], kv_s[..., B_NOPE:])
        kr_all = sample_keys(cache_b_krope[i], rows_s(kr))
        b_out = attend(sample_q(rows_s(q_mla)), mla_keys(kn_all, kr_all), v_all, ctiles_s, S_BLK, 1, True, s_i)[:, :st]
        mix = jnp.concatenate([a_out.reshape(n_s, -1), b_out.reshape(n_s, -1)], axis=-1)
        outs_even["a_k_s"].append(ak_all[:, -n_a:])
        outs_even["a_v_s"].append(av_all[:, -n_a:])
        outs_even["ckv_s"].append(rows_s(ckv))
        outs_even["kr_s"].append(rows_s(kr))
        return dense(mix.astype(BF16), even_w_out[i].astype(BF16), res=x)

    def diff_lambda(l, i):
        lam_init = 0.8 - 0.6 * math.exp(-0.3 * l)
        lam = (jnp.exp(jnp.sum(c_lambda_q1[i].astype(F32) * c_lambda_k1[i].astype(F32)))
               - jnp.exp(jnp.sum(c_lambda_q2[i].astype(F32) * c_lambda_k2[i].astype(F32))) + lam_init)
        return lam_init, lam

    def odd_prompt(x, l, i):
        lam_init, lam = diff_lambda(l, i)
        st_, cqt, kc, cvt, qit, wq, qt = proj_odd(x, norm_attn[l], odd_w_in[i], p_blk)
        wc = C_HEADS * 2 * C_DIM
        ck, cv = st_[:, :wc], st_[:, wc:2 * wc]
        dk, dv, dki = (st_[:, 2 * wc + n * LANES:2 * wc + n * LANES + D_DIM] for n in range(3))
        c_t = flash(cqt, kc, cvt, c_tiles_p, n_tiles=2, has_far=True)
        ki, k, vt = _dsa_keys(dk[None], dv[None], dki[None])
        d_t = dsa(qit, wq, ki, qt, k, vt, d_tiles, i_off=0, n_sel=n_sel_p, out_dtype=BF16)
        for name, arr, shp in (("c_k", ck, (C_HEADS, 2 * C_DIM)), ("c_v", cv, (C_HEADS, 2 * C_DIM)),
                               ("d_k", dk, (D_DIM,)), ("d_v", dv, (D_DIM,)), ("d_ki", dki, (IDX_DIM,))):
            outs_odd[name + "_p"].append(arr.reshape(pb, pt, *shp))
        return out_odd(x, c_t, d_t, lam, c_subln[i], 1.0 - lam_init, odd_w_out[i].astype(BF16))

    def odd_sample(x, l, i):
        lam_init, lam = diff_lambda(l, i)
        h = dense(x, pad_cols(odd_w_in[i]), g=norm_attn[l])
        cq, ck, cv, dq, dk, dv, dqi, dki, dw = _split(h, ODD_SPLITS)
        cq = (cq * (C_DIM ** -0.5 * LOG2E)).reshape(-1, 2 * C_HEADS, C_DIM)
        ck3 = ck.reshape(-1, 2 * C_HEADS, C_DIM)
        cv3 = cv.reshape(-1, C_HEADS, 2 * C_DIM)
        dq = dq.reshape(-1, D_HEADS, D_DIM)
        dqi = dqi.reshape(-1, IDX_HEADS, IDX_DIM)
        dw = dw * ((IDX_HEADS ** -0.5) * (IDX_DIM ** -0.5))

        def diff_combine(o):
            b_, t_, _ = o.shape
            o = o.reshape(b_, t_, C_HEADS, 2, 2 * C_DIM)
            c = o[:, :, :, 0] - lam * o[:, :, :, 1]
            return (_rms(c, c_subln[i].astype(F32)) * (1.0 - lam_init)).reshape(b_, t_, -1)

        rows, tag, nb_, nt_ = rows_s, "_s", sb, st
        ck_all = sample_keys(cache_c_k[i].reshape(sb, past, 2 * C_HEADS, C_DIM), rows(ck3))
        cv_all = sample_keys(cache_c_v[i], rows(cv3))
        c_out = attend(sample_q(rows(cq)), ck_all, cv_all, c_tiles_s, S_BLK, 2, True, s_i)[:, :st]
        d_out = _dsa_call(_pad_rows(rows(dq), DSA_TQ), _pad_rows(rows(dqi), DSA_TQ), _pad_rows(rows(dw), DSA_TQ),
                          sample_keys(cache_d_k[i], rows(dk)), sample_keys(cache_d_v[i], rows(dv)),
                          sample_keys(cache_d_kidx[i], rows(dki)), d_tiles, s_i_dsa, n_sel_s)[:, :st]
        mix = jnp.concatenate([diff_combine(c_out).reshape(nb_ * nt_, -1), d_out.reshape(nb_ * nt_, -1)], axis=-1)
        for name, arr, shp in (("c_k", ck, (C_HEADS, 2 * C_DIM)), ("c_v", cv, (C_HEADS, 2 * C_DIM)),
                               ("d_k", dk, (D_DIM,)), ("d_v", dv, (D_DIM,)), ("d_ki", dki, (IDX_DIM,))):
            outs_odd[name + tag].append(rows(arr).reshape(nb_, nt_, *shp))
        return dense(mix.astype(BF16), odd_w_out[i].astype(BF16), res=x)

    xp, xs = x_prompt.reshape(n_p, d), x_sample.reshape(n_s, d)
    for l in range(depth):
        i = l // 2
        if l % 2 == 0:
            xp, xs = even_prompt(xp, l, i), even_sample(xs, l, i)
        else:
            xp, xs = odd_prompt(xp, l, i), odd_sample(xs, l, i)
        wg, wu, wd = ffn_w_gate[l].astype(BF16), ffn_w_up[l].astype(BF16), ffn_w_down[l].astype(BF16)
        fg = final_norm if l == depth - 1 else None
        xp, xs = ffn(xp, norm_ffn[l], wg, wu, wd, final_g=fg), ffn(xs, norm_ffn[l], wg, wu, wd, final_g=fg)

    se = {k: jnp.stack(v, axis=0) for k, v in outs_even.items()}
    so = {k: jnp.stack(v, axis=0) for k, v in outs_odd.items()}
    return (xp.reshape(pb, pt, d), xs.reshape(sb, st, d), se["a_k_p"], se["a_k_s"], se["a_v_p"], se["a_v_s"],
            se["ckv_p"], se["ckv_s"], se["kr_p"], se["kr_s"],
            so["c_k_p"], so["c_k_s"], so["c_v_p"], so["c_v_s"],
            so["d_k_p"], so["d_k_s"], so["d_v_p"], so["d_v_s"], so["d_ki_p"], so["d_ki_s"])
```

```python
import functools
import math

import numpy as np
import jax
import jax.numpy as jnp
from jax import lax
from jax.experimental import pallas as pl
from jax.experimental.pallas import tpu as pltpu

F32, BF16, I32 = jnp.float32, jnp.bfloat16, jnp.int32

D_MODEL = 1024
CHUNK = 64
EPS = 1e-6
A_HEADS, A_DIM, A_LEFT_CHUNKS, A_MAX_REL = 8, 64, 8, 128
B_HEADS, B_NOPE, B_ROPE, B_V = 8, 64, 32, 64
Q_LORA, KV_LORA = 256, 128
ROPE_THETA = 10000.0
C_HEADS, C_DIM = 4, 64
D_HEADS, D_DIM = 8, 64
IDX_HEADS, IDX_DIM = 8, 64
TOPK_MAX = 256
T5_BUCKETS, T5_MAX_DIST = 32, 128
EVEN_SPLITS = (512, 512, 512, Q_LORA, KV_LORA, B_ROPE)
ODD_SPLITS = (512, 512, 512, 512, D_DIM, D_DIM, 512, IDX_DIM, IDX_HEADS)

LANES = 128
SUBLANES = 8
VMEM_LIMIT = 56 * 1024 * 1024
MASKED = -1e30
UNSELECTED = -2e30
M_INIT = -1e30
SUM_MAX, SUM_MIN = 1e18, 1e-18
LOG2E = math.log2(math.e)
INT_MIN = -(2 ** 31)
INT_MAX = 2 ** 31 - 1

DSA_TQ, DSA_TK = 128, 256
COUNT_ROWS = 8 * SUBLANES
FAR_UNROLL = 4
DSA_FAR_UNROLL = 4
DSA_SCORE_UNROLL = 4


def _cparams(n_axes):
    return pltpu.CompilerParams(dimension_semantics=("arbitrary",) * n_axes, vmem_limit_bytes=VMEM_LIMIT)


def _rms(x, g):
    return x * lax.rsqrt(jnp.mean(x * x, axis=-1, keepdims=True) + EPS) * g


def _dense_kernel(*refs, norm, res, emit_h):
    it = iter(refs)
    x_ref = next(it)
    g_ref = next(it) if norm else None
    w_ref = next(it)
    r_ref = next(it) if res else None
    o_ref = next(it)
    h_ref = next(it) if emit_h else None
    x = x_ref[...]
    if norm:
        x = _rms(x.astype(F32), g_ref[...])
        if emit_h:
            h_ref[...] = x
    acc = jnp.dot(x.astype(BF16), w_ref[...], preferred_element_type=F32)
    if res:
        acc = acc + r_ref[...]
    o_ref[...] = acc


def dense(x, w, g=None, res=None, emit_h=False, tm=512):
    m, k = x.shape
    n = w.shape[1]
    assert m % tm == 0 and n % LANES == 0
    norm = g is not None
    args, specs = [x], [pl.BlockSpec((tm, k), lambda i: (i, 0))]
    if norm:
        args.append(g.reshape(1, k).astype(F32))
        specs.append(pl.BlockSpec((1, k), lambda i: (0, 0)))
    args.append(w)
    specs.append(pl.BlockSpec((k, n), lambda i: (0, 0)))
    if res is not None:
        args.append(res)
        specs.append(pl.BlockSpec((tm, n), lambda i: (i, 0)))
    out_shape = [jax.ShapeDtypeStruct((m, n), F32)]
    out_specs = [pl.BlockSpec((tm, n), lambda i: (i, 0))]
    if emit_h:
        out_shape.append(jax.ShapeDtypeStruct((m, k), F32))
        out_specs.append(pl.BlockSpec((tm, k), lambda i: (i, 0)))
    outs = pl.pallas_call(
        functools.partial(_dense_kernel, norm=norm, res=res is not None, emit_h=emit_h),
        grid=(m // tm,), in_specs=specs, out_specs=out_specs, out_shape=out_shape,
        compiler_params=_cparams(1), name="dense")(*args)
    return outs if emit_h else outs[0]


def _ffn_kernel(x_ref, g_ref, wg_ref, wu_ref, wd_ref, fg_ref, o_ref, h_sc, acc_sc, *, final):
    f = pl.program_id(1)

    @pl.when(f == 0)
    def _():
        x = x_ref[...]
        h_sc[...] = _rms(x, g_ref[...]).astype(BF16)
        acc_sc[...] = x

    h = h_sc[...]
    gate = jnp.dot(h, wg_ref[...], preferred_element_type=F32)
    up = jnp.dot(h, wu_ref[...], preferred_element_type=F32)
    a = (gate * jax.nn.sigmoid(gate) * up).astype(BF16)
    acc_sc[...] += jnp.dot(a, wd_ref[...], preferred_element_type=F32)

    @pl.when(f == pl.num_programs(1) - 1)
    def _():
        y = acc_sc[...]
        if final:
            y = _rms(y, fg_ref[...])
        o_ref[...] = y


def ffn(x, g, wg, wu, wd, final_g=None, tm=512, nf=2):
    m, d = x.shape
    hid = wg.shape[1]
    tf = hid // nf
    assert m % tm == 0 and hid % nf == 0 and tf % LANES == 0
    final = final_g is not None
    fg = (final_g if final else g).reshape(1, d).astype(F32)
    return pl.pallas_call(
        functools.partial(_ffn_kernel, final=final),
        grid=(m // tm, nf),
        in_specs=[pl.BlockSpec((tm, d), lambda i, f: (i, 0)),
                  pl.BlockSpec((1, d), lambda i, f: (0, 0)),
                  pl.BlockSpec((d, tf), lambda i, f: (0, f)),
                  pl.BlockSpec((d, tf), lambda i, f: (0, f)),
                  pl.BlockSpec((tf, d), lambda i, f: (f, 0)),
                  pl.BlockSpec((1, d), lambda i, f: (0, 0))],
        out_specs=pl.BlockSpec((tm, d), lambda i, f: (i, 0)),
        out_shape=jax.ShapeDtypeStruct((m, d), F32),
        scratch_shapes=[pltpu.VMEM((tm, d), BF16), pltpu.VMEM((tm, d), F32)],
        compiler_params=_cparams(2), name="ffn")(x, g.reshape(1, d).astype(F32), wg, wu, wd, fg)


def _flash_kernel(qt_ref, k_ref, vt_ref, tile_ref, o_ref, m_sc, l_sc, acc_sc, *, n_tiles, has_far, i_off):
    i = pl.program_id(2) + i_off
    qt = qt_ref[0, 0, 0]
    j_lo = 0 if has_far else jnp.maximum(i - (n_tiles - 1), 0)

    def raw(j):
        return jnp.dot(k_ref[0, 0, j], qt, preferred_element_type=F32)

    def scores(j):
        return raw(j) + tile_ref[0, jnp.minimum(i - j, n_tiles)]

    def reset():
        l_sc[...] = jnp.zeros(l_sc.shape, F32)
        acc_sc[...] = jnp.zeros(acc_sc.shape, F32)

    def accumulate(j, p):
        l_sc[...] += jnp.sum(p, axis=0, keepdims=True)
        acc_sc[...] += jnp.dot(vt_ref[0, 0, j], p.astype(BF16), preferred_element_type=F32)

    reset()
    s_diag = raw(i) + tile_ref[0, 0]
    m = jnp.max(s_diag, axis=0, keepdims=True)
    accumulate(i, jnp.exp2(s_diag - m))
    if has_far:
        m_far = m - tile_ref[0, n_tiles, 0:1, :]

        n_far = jnp.maximum(i - (n_tiles - 1), 0)

        def far_group(jj, carry):
            ps = [jnp.exp2(raw(FAR_UNROLL * jj + u) - m_far) for u in range(FAR_UNROLL)]
            for u in range(FAR_UNROLL):
                accumulate(FAR_UNROLL * jj + u, ps[u])
            return carry

        def far_single(j, carry):
            accumulate(j, jnp.exp2(raw(j) - m_far))
            return carry

        n_grouped = (n_far // FAR_UNROLL) * FAR_UNROLL
        lax.fori_loop(0, n_far // FAR_UNROLL, far_group, 0)
        lax.fori_loop(n_grouped, n_far, far_single, 0)
    for t in range(1, n_tiles):
        @pl.when(i - t >= 0)
        def _(t=t):
            accumulate(i - t, jnp.exp2(raw(i - t) + tile_ref[0, t] - m))

    l = l_sc[...]
    in_range = (jnp.max(l) < SUM_MAX) & (jnp.min(l) > SUM_MIN)

    @pl.when(jnp.logical_not(in_range))
    def _():
        reset()
        m_sc[...] = jnp.full(m_sc.shape, M_INIT, F32)

        def body(j, carry):
            s = scores(j)
            m_prev = m_sc[...]
            m_new = jnp.maximum(m_prev, jnp.max(s, axis=0, keepdims=True))
            alpha = jnp.exp2(m_prev - m_new)
            p = jnp.exp2(s - m_new)
            l_sc[...] = alpha * l_sc[...] + jnp.sum(p, axis=0, keepdims=True)
            acc_sc[...] = alpha * acc_sc[...] + jnp.dot(vt_ref[0, 0, j], p.astype(BF16), preferred_element_type=F32)
            m_sc[...] = m_new
            return carry

        lax.fori_loop(j_lo, i + 1, body, 0)

    o_ref[0, 0, 0] = (acc_sc[...] / l_sc[...]).astype(o_ref.dtype)


def flash(qt, k, vt, tiles, *, n_tiles, has_far, i_off=0, out_dtype=F32):
    b, g, nq, dq, blk = qt.shape
    gk, nb = k.shape[1], k.shape[2]
    gv, dv = vt.shape[1], vt.shape[3]
    gb = tiles.shape[0]
    assert tiles.shape[1] == n_tiles + 1
    kdiv, vdiv, bdiv = g // gk, g // gv, g // gb
    return pl.pallas_call(
        functools.partial(_flash_kernel, n_tiles=n_tiles, has_far=has_far, i_off=i_off),
        grid=(b, g, nq),
        in_specs=[pl.BlockSpec((1, 1, 1, dq, blk), lambda bi, gi, i: (bi, gi, i, 0, 0)),
                  pl.BlockSpec((1, 1, nb, blk, dq), lambda bi, gi, i: (bi, gi // kdiv, 0, 0, 0)),
                  pl.BlockSpec((1, 1, nb, dv, blk), lambda bi, gi, i: (bi, gi // vdiv, 0, 0, 0)),
                  pl.BlockSpec((1, n_tiles + 1, blk, blk), lambda bi, gi, i: (gi // bdiv, 0, 0, 0))],
        out_specs=pl.BlockSpec((1, 1, 1, dv, blk), lambda bi, gi, i: (bi, gi, i, 0, 0)),
        out_shape=jax.ShapeDtypeStruct((b, g, nq, dv, blk), out_dtype),
        scratch_shapes=[pltpu.VMEM((1, blk), F32), pltpu.VMEM((1, blk), F32), pltpu.VMEM((dv, blk), F32)],
        compiler_params=_cparams(3), name="flash")(qt, k, vt, tiles)


def _dsa_kernel(qit_ref, w_ref, ki_ref, qt_ref, k_ref, vt_ref, tile_ref, o_ref,
                key_sc, kb_sc, m_sc, l_sc, acc_sc, *, i_off, n_sel, idx_bits):
    tq, tk, nh = DSA_TQ, DSA_TK, D_HEADS
    i = pl.program_id(1) + i_off
    nkb = (i * tq) // tk + 1
    krow = lax.broadcasted_iota(I32, (tk, tq), 0)
    qcol = lax.broadcasted_iota(I32, (tk, tq), 1)
    q_chunk = jnp.right_shift(i * tq + qcol, 6)

    def head(x, h):
        return x[:, h * tq:(h + 1) * tq]

    qit = qit_ref[0, 0]
    w = w_ref[0, 0]

    def score_block(j, mask_future):
        lg = jnp.maximum(jnp.dot(ki_ref[0, j], qit, preferred_element_type=F32), 0.0) * w
        sc = head(lg, 0)
        for h in range(1, IDX_HEADS):
            sc = sc + head(lg, h)
        sc = jnp.where(sc == 0.0, 0.0, sc)
        bits = pltpu.bitcast(sc, I32)
        key = bits ^ (jnp.right_shift(bits, 31) & INT_MAX)
        high = pltpu.bitcast(bits & -(2 ** 16), F32)
        if mask_future:
            adm = jnp.right_shift(j * tk + krow, 6) <= q_chunk
            key = jnp.where(adm, key, INT_MIN)
            high = jnp.where(adm, high, jnp.nan)
        key_sc[j] = key
        kb_sc[j] = high.astype(BF16)

    def score_group(jj, carry):
        for u in range(DSA_SCORE_UNROLL):
            score_block(DSA_SCORE_UNROLL * jj + u, False)
        return carry

    def score_single(j, carry):
        score_block(j, False)
        return carry

    n_full = nkb - 1
    lax.fori_loop(0, n_full // DSA_SCORE_UNROLL, score_group, 0)
    lax.fori_loop((n_full // DSA_SCORE_UNROLL) * DSA_SCORE_UNROLL, n_full, score_single, 0)
    score_block(n_full, True)

    def count(pred_fn):
        def body(j, acc):
            ind = jnp.where(pred_fn(key_sc[j], j * tk + krow), 1.0, 0.0)
            return acc + jnp.sum(ind.reshape(tk // COUNT_ROWS, COUNT_ROWS, tq), axis=0)
        acc = lax.fori_loop(0, nkb, body, jnp.zeros((COUNT_ROWS, tq), F32))
        return jnp.sum(acc, axis=0, keepdims=True)

    target = float(n_sel)
    qpos = i * tq + lax.broadcasted_iota(I32, (1, tq), 1)
    n_adm = ((jnp.right_shift(qpos, 6) + 1) * CHUNK).astype(F32)
    real = n_adm > target

    def unsettled(c_thr):
        settled = (c_thr == target) | jnp.logical_not(real)
        return jnp.min(jnp.where(settled, 1.0, 0.0)) < 1.0

    one_h, zero_h = jnp.ones((tk, tq), BF16), jnp.zeros((tk, tq), BF16)

    def count_high(cand_h):
        def body(j, acc):
            ind = jnp.where(kb_sc[j] >= cand_h, one_h, zero_h)
            return acc + sum(ind[r * COUNT_ROWS:(r + 1) * COUNT_ROWS] for r in range(tk // COUNT_ROWS))
        acc = lax.fori_loop(0, nkb, body, jnp.zeros((COUNT_ROWS, tq), BF16))
        return jnp.sum(acc.astype(F32), axis=0, keepdims=True)

    def high_bit(state):
        thr_h, c_thr, b = state
        cand = thr_h + jnp.left_shift(jnp.int32(1), 15 - b)
        pattern = cand ^ (jnp.right_shift(cand, 15) & 0x7FFF)
        cand_h = pltpu.bitcast(jnp.left_shift(pattern, 16), F32).astype(BF16)
        cnt = count_high(cand_h)
        ge = cnt >= target
        return jnp.where(ge, cand, thr_h), jnp.where(ge, cnt, c_thr), b + 1

    thr_h, c_thr, _ = lax.while_loop(lambda s: (s[2] < 16) & unsettled(s[1]), high_bit,
                                     (jnp.full((1, tq), -(2 ** 15), I32), n_adm, jnp.int32(0)))

    def low_bit(state):
        thr, c_thr, b = state
        cand = thr + jnp.left_shift(jnp.int32(1), 31 - b)
        cnt = count(lambda k, idx: k >= cand)
        ge = cnt >= target
        return jnp.where(ge, cand, thr), jnp.where(ge, cnt, c_thr), b + 1

    thr, c_thr, _ = lax.while_loop(lambda s: (s[2] < 32) & unsettled(s[1]), low_bit,
                                   (jnp.left_shift(thr_h, 16), c_thr, jnp.int32(16)))

    def tie_search():
        need = target - count(lambda k, idx: k > thr)

        def index_bit(b, y):
            cand = y + jnp.left_shift(jnp.int32(1), idx_bits - 1 - b)
            cnt = count(lambda k, idx: (k == thr) & (idx < cand))
            return jnp.where(cnt < need, cand, y)
        return lax.fori_loop(0, idx_bits, index_bit, jnp.zeros((1, tq), I32))

    has_ties = jnp.max(jnp.where((c_thr > target) & real, 1.0, 0.0)) > 0.0
    last = lax.cond(has_ties, tie_search, lambda: jnp.full((1, tq), INT_MAX, I32))
    last = jnp.where(real, last, -1)

    qt = qt_ref[0, 0]
    n_near = 3

    def raw(j):
        return jnp.dot(k_ref[0, j], qt, preferred_element_type=F32)

    def tile_of(j):
        return jnp.minimum((i * tq - j * tk) // tq, n_near)

    def select(j, s):
        key = key_sc[j]
        sel = (key > thr) | ((key == thr) & (j * tk + krow <= last))
        selb = jnp.where(sel, 0.0, UNSELECTED)
        return jnp.concatenate([head(s, h) + selb for h in range(nh)], axis=1)

    def reset():
        l_sc[...] = jnp.zeros(l_sc.shape, F32)
        acc_sc[...] = jnp.zeros(acc_sc.shape, F32)

    def accumulate(j, p):
        l_sc[...] += jnp.sum(p, axis=0, keepdims=True)
        acc_sc[...] += jnp.dot(vt_ref[0, j], p.astype(BF16), preferred_element_type=F32)

    reset()
    s_last = raw(nkb - 1) + tile_ref[tile_of(nkb - 1)]
    m = jnp.max(s_last, axis=0, keepdims=True)
    accumulate(nkb - 1, jnp.exp2(select(nkb - 1, s_last) - m))
    m_far = m - tile_ref[n_near, 0:1, :]
    n_far = jnp.maximum(i - 1, 0) // 2

    def far_group(jj, carry):
        ps = [jnp.exp2(select(DSA_FAR_UNROLL * jj + u, raw(DSA_FAR_UNROLL * jj + u)) - m_far)
              for u in range(DSA_FAR_UNROLL)]
        for u in range(DSA_FAR_UNROLL):
            accumulate(DSA_FAR_UNROLL * jj + u, ps[u])
        return carry

    def far_single(j, carry):
        accumulate(j, jnp.exp2(select(j, raw(j)) - m_far))
        return carry

    def near(j, carry):
        accumulate(j, jnp.exp2(select(j, raw(j) + tile_ref[tile_of(j)]) - m))
        return carry

    n_grouped = (n_far // DSA_FAR_UNROLL) * DSA_FAR_UNROLL
    lax.fori_loop(0, n_far // DSA_FAR_UNROLL, far_group, 0)
    lax.fori_loop(n_grouped, n_far, far_single, 0)
    lax.fori_loop(n_far, nkb - 1, near, 0)

    l = l_sc[...]
    in_range = (jnp.max(l) < SUM_MAX) & (jnp.min(l) > SUM_MIN)

    @pl.when(jnp.logical_not(in_range))
    def _():
        reset()
        m_sc[...] = jnp.full(m_sc.shape, M_INIT, F32)

        def attend(j, carry):
            s = select(j, raw(j) + tile_ref[tile_of(j)])
            m_prev = m_sc[...]
            m_new = jnp.maximum(m_prev, jnp.max(s, axis=0, keepdims=True))
            alpha = jnp.exp2(m_prev - m_new)
            p = jnp.exp2(s - m_new)
            l_sc[...] = alpha * l_sc[...] + jnp.sum(p, axis=0, keepdims=True)
            acc_sc[...] = alpha * acc_sc[...] + jnp.dot(vt_ref[0, j], p.astype(BF16), preferred_element_type=F32)
            m_sc[...] = m_new
            return carry

        lax.fori_loop(0, nkb, attend, 0)

    o_ref[0, 0] = (acc_sc[...] / l_sc[...]).astype(o_ref.dtype)


def dsa(qit, w, ki, qt, k, vt, tiles, *, i_off, n_sel, out_dtype=F32):
    b, nq, kdim, cols = qit.shape
    nkb = ki.shape[1]
    dk, dv = k.shape[-1], vt.shape[2]
    idx_bits = max(1, (nkb * DSA_TK - 1).bit_length())
    assert (DSA_TK // COUNT_ROWS) * nkb <= 256
    return pl.pallas_call(
        functools.partial(_dsa_kernel, i_off=i_off, n_sel=n_sel, idx_bits=idx_bits),
        grid=(b, nq),
        in_specs=[pl.BlockSpec((1, 1, kdim, cols), lambda bi, i: (bi, i, 0, 0)),
                  pl.BlockSpec((1, 1, 1, cols), lambda bi, i: (bi, i, 0, 0)),
                  pl.BlockSpec((1, nkb, DSA_TK, kdim), lambda bi, i: (bi, 0, 0, 0)),
                  pl.BlockSpec((1, 1, dk, cols), lambda bi, i: (bi, i, 0, 0)),
                  pl.BlockSpec((1, nkb, DSA_TK, dk), lambda bi, i: (bi, 0, 0, 0)),
                  pl.BlockSpec((1, nkb, dv, DSA_TK), lambda bi, i: (bi, 0, 0, 0)),
                  pl.BlockSpec((4, DSA_TK, cols), lambda bi, i: (0, 0, 0))],
        out_specs=pl.BlockSpec((1, 1, dv, cols), lambda bi, i: (bi, i, 0, 0)),
        out_shape=jax.ShapeDtypeStruct((b, nq, dv, cols), out_dtype),
        scratch_shapes=[pltpu.VMEM((nkb, DSA_TK, DSA_TQ), I32), pltpu.VMEM((nkb, DSA_TK, DSA_TQ), BF16),
                        pltpu.VMEM((1, cols), F32), pltpu.VMEM((1, cols), F32), pltpu.VMEM((dv, cols), F32)],
        compiler_params=_cparams(2), name="dsa")(qit, w, ki, qt, k, vt, tiles)


_NT = (((1,), (1,)), ((), ()))


def _heads(x, n, d):
    return x.reshape(n, d, x.shape[-1])


def _proj_even_kernel(x_ref, g_ref, w1_ref, w2t_ref, w3_ref, qn_ref, wuqt_ref, kvn_ref, wkb_ref, e1_ref, e2_ref,
                      wuvt_ref, cosn_ref, sinn_ref, cost_ref, sint_ref,
                      akv_ref, ckv_ref, y1_ref, y2_ref, qta_ref, ka_ref, vta_ref, qtb_ref, kb_ref, vtb_ref):
    nh, da, rope_half = A_HEADS, A_DIM, B_ROPE // 2
    h = _rms(x_ref[...], g_ref[...]).astype(BF16)
    h1 = jnp.dot(h, w1_ref[...], preferred_element_type=F32)
    akv_ref[...] = h1[:, :2 * nh * da]
    t2 = lax.dot_general(w2t_ref[...], h, _NT, preferred_element_type=F32)
    qta_ref[0, :, 0] = _heads(t2[:nh * da], nh, da).astype(BF16)
    vta_ref[0, :, 0] = _heads(t2[nh * da:], nh, da).astype(BF16)
    k3 = jnp.dot(h, w3_ref[...], preferred_element_type=F32)
    for hd in range(nh):
        ka_ref[0, hd, 0] = k3[:, hd * LANES:hd * LANES + da].astype(BF16)

    c0 = 2 * nh * da
    cqn = _rms(h1[:, c0:c0 + Q_LORA], qn_ref[...]).astype(BF16)
    tq = lax.dot_general(wuqt_ref[...], cqn, _NT, preferred_element_type=F32)
    n_nope, n_rope = B_HEADS * B_NOPE, B_HEADS * rope_half
    x1, x2 = tq[n_nope:n_nope + n_rope], tq[n_nope + n_rope:]
    ct, st = cost_ref[0], sint_ref[0]
    y1t, y2t = x1 * ct - x2 * st, x1 * st + x2 * ct
    for hd in range(B_HEADS):
        qtb_ref[0, hd, 0] = jnp.concatenate(
            [tq[hd * B_NOPE:(hd + 1) * B_NOPE], y1t[hd * rope_half:(hd + 1) * rope_half],
             y2t[hd * rope_half:(hd + 1) * rope_half]], axis=0).astype(BF16)

    c1 = c0 + Q_LORA
    ckv = _rms(h1[:, c1:c1 + KV_LORA], kvn_ref[...])
    ckv_ref[...] = ckv
    ckvb = ckv.astype(BF16)
    k1, k2 = h1[:, c1 + KV_LORA:c1 + KV_LORA + LANES], h1[:, c1 + KV_LORA + LANES:]
    cn, sn = cosn_ref[...], sinn_ref[...]
    y1, y2 = k1 * cn - k2 * sn, k1 * sn + k2 * cn
    y1_ref[...] = y1
    y2_ref[...] = y2
    kb = (jnp.dot(ckvb, wkb_ref[...], preferred_element_type=F32)
          + jnp.dot(y1.astype(BF16), e1_ref[...], preferred_element_type=F32)
          + jnp.dot(y2.astype(BF16), e2_ref[...], preferred_element_type=F32))
    for hd in range(B_HEADS):
        kb_ref[0, hd, 0] = kb[:, hd * LANES:hd * LANES + B_NOPE + B_ROPE].astype(BF16)
    tv = lax.dot_general(wuvt_ref[...], ckvb, _NT, preferred_element_type=F32)
    vtb_ref[0, :, 0] = _heads(tv, B_HEADS, B_V).astype(BF16)


def proj_even(x, g, w_in, q_norm, w_uq, kv_norm, w_ukv, cos, sin, blk):
    p, d = x.shape
    nq = p // blk
    nh, da, half = A_HEADS, A_DIM, B_ROPE // 2
    s_a = A_DIM ** -0.5 * LOG2E
    s_b = (B_NOPE + B_ROPE) ** -0.5 * LOG2E
    aq_w, ak_w, av_w, cq_w, ckv_w, kr_w = _split(w_in, EVEN_SPLITS)

    def lane_pad(w, n=LANES):
        return jnp.pad(w, ((0, 0), (0, n - w.shape[1])))

    def head_chunks(w, dh):
        k = w.shape[0]
        return jnp.pad(w.reshape(k, -1, dh), ((0, 0), (0, 0), (0, LANES - dh))).reshape(k, -1)

    w1 = jnp.concatenate([ak_w, av_w, cq_w, ckv_w, lane_pad(kr_w[:, :half]), lane_pad(kr_w[:, half:])], axis=1).astype(BF16)
    w2t = jnp.concatenate([aq_w * s_a, av_w], axis=1).T.astype(BF16)
    w3 = head_chunks(ak_w, da).astype(BF16)
    uq = (w_uq * s_b).reshape(Q_LORA, B_HEADS, B_NOPE + B_ROPE)
    wuqt = jnp.concatenate([uq[:, :, :B_NOPE].reshape(Q_LORA, -1), uq[:, :, B_NOPE:B_NOPE + half].reshape(Q_LORA, -1),
                            uq[:, :, B_NOPE + half:].reshape(Q_LORA, -1)], axis=1).T.astype(BF16)
    ukv = w_ukv.reshape(KV_LORA, B_HEADS, B_NOPE + B_V)
    wkb = head_chunks(ukv[:, :, :B_NOPE].reshape(KV_LORA, -1), B_NOPE).astype(BF16)
    wuvt = ukv[:, :, B_NOPE:].reshape(KV_LORA, -1).T.astype(BF16)
    eye = np.zeros((2, LANES, B_HEADS, LANES), np.float32)
    for r in range(half):
        eye[0, r, :, B_NOPE + r] = 1.0
        eye[1, r, :, B_NOPE + half + r] = 1.0
    e1, e2 = (jnp.asarray(e.reshape(LANES, B_HEADS * LANES), BF16) for e in eye)
    cosn, sinn = lane_pad(cos), lane_pad(sin)
    cost = jnp.tile(cos.reshape(nq, blk, half).transpose(0, 2, 1), (1, B_HEADS, 1))
    sint = jnp.tile(sin.reshape(nq, blk, half).transpose(0, 2, 1), (1, B_HEADS, 1))

    def full(a):
        return pl.BlockSpec(a.shape, lambda i: (0,) * a.ndim)

    def rows(n):
        return pl.BlockSpec((blk, n), lambda i: (i, 0))

    def tile3(a):
        return pl.BlockSpec((1,) + a.shape[1:], lambda i: (i, 0, 0))

    def per_block(g_, a, b_):
        return (jax.ShapeDtypeStruct((1, g_, nq, a, b_), BF16),
                pl.BlockSpec((1, g_, 1, a, b_), lambda i: (0, 0, i, 0, 0)))

    consts = [g.reshape(1, d).astype(F32), w1, w2t, w3, q_norm.reshape(1, -1).astype(F32), wuqt,
              kv_norm.reshape(1, -1).astype(F32), wkb, e1, e2, wuvt]
    blocks = [per_block(nh, da, blk), per_block(nh, blk, da), per_block(nh, da, blk),
              per_block(B_HEADS, B_NOPE + B_ROPE, blk), per_block(B_HEADS, blk, B_NOPE + B_ROPE),
              per_block(B_HEADS, B_V, blk)]
    out_shape = [jax.ShapeDtypeStruct((p, 2 * nh * da), F32), jax.ShapeDtypeStruct((p, KV_LORA), F32),
                 jax.ShapeDtypeStruct((p, LANES), F32), jax.ShapeDtypeStruct((p, LANES), F32)] + [s for s, _ in blocks]
    out_specs = [rows(2 * nh * da), rows(KV_LORA), rows(LANES), rows(LANES)] + [s for _, s in blocks]
    return pl.pallas_call(
        _proj_even_kernel, grid=(nq,),
        in_specs=[rows(d)] + [full(a) for a in consts] + [rows(LANES), rows(LANES), tile3(cost), tile3(sint)],
        out_specs=out_specs, out_shape=out_shape,
        compiler_params=_cparams(1), name="proj_even")(x, *consts, cosn, sinn, cost, sint)


def _proj_odd_kernel(x_ref, g_ref, w1_ref, w2t_ref, w3_ref,
                     st_ref, cqt_ref, kc_ref, cvt_ref, qit_ref, wq_ref, qt_ref):
    nmap, width = 2 * C_HEADS, C_HEADS * 2 * C_DIM
    h = _rms(x_ref[...], g_ref[...]).astype(BF16)
    st_ref[...] = jnp.dot(h, w1_ref[...], preferred_element_type=F32)
    t2 = lax.dot_general(w2t_ref[...], h, _NT, preferred_element_type=F32)
    cqt_ref[0, :, 0] = _heads(t2[:width], nmap, C_DIM).astype(BF16)
    cvt_ref[0, :, 0] = _heads(t2[width:2 * width], C_HEADS, 2 * C_DIM).astype(BF16)
    k3 = jnp.dot(h, w3_ref[...], preferred_element_type=F32)
    for m in range(nmap):
        kc_ref[0, m, 0] = k3[:, m * LANES:m * LANES + C_DIM].astype(BF16)

    dqt = t2[2 * width:2 * width + D_HEADS * D_DIM]
    dqit = t2[2 * width + D_HEADS * D_DIM:2 * width + D_HEADS * (D_DIM + IDX_DIM)]
    dwt = t2[2 * width + D_HEADS * (D_DIM + IDX_DIM):]

    def stack(x, rows_per_head, n_rows, b):
        cols = slice(b * DSA_TQ, (b + 1) * DSA_TQ)
        return jnp.concatenate([x[hd * rows_per_head:hd * rows_per_head + n_rows, cols] for hd in range(D_HEADS)], axis=1)

    for b in range(x_ref.shape[0] // DSA_TQ):
        qt_ref[0, b] = stack(dqt, D_DIM, D_DIM, b).astype(BF16)
        qi = stack(dqit, IDX_DIM, IDX_DIM, b)
        hi = qi.astype(BF16)
        lo = (qi - hi.astype(F32)).astype(BF16)
        qit_ref[0, b] = jnp.concatenate([hi, lo, hi], axis=0)
        wq_ref[0, b] = stack(dwt, SUBLANES, 1, b)


def proj_odd(x, g, w_in, blk):
    p, d = x.shape
    nq, nq_dsa = p // blk, p // DSA_TQ
    s_c = C_DIM ** -0.5 * LOG2E
    s_d = D_DIM ** -0.5 * LOG2E
    s_w = (IDX_HEADS ** -0.5) * (IDX_DIM ** -0.5)
    cq_w, ck_w, cv_w, dq_w, dk_w, dv_w, dqi_w, dki_w, dw_w = _split(w_in, ODD_SPLITS)

    def lane_pad(w):
        return jnp.pad(w, ((0, 0), (0, LANES - w.shape[1])))

    w1 = jnp.concatenate([ck_w, cv_w, lane_pad(dk_w), lane_pad(dv_w), lane_pad(dki_w)], axis=1).astype(BF16)
    dw_rows = jnp.pad((dw_w * s_w)[:, :, None], ((0, 0), (0, 0), (0, SUBLANES - 1))).reshape(d, -1)
    w2t = jnp.concatenate([cq_w * s_c, cv_w, dq_w * s_d, dqi_w, dw_rows], axis=1).T.astype(BF16)
    w3 = jnp.pad(ck_w.reshape(d, -1, C_DIM), ((0, 0), (0, 0), (0, LANES - C_DIM))).reshape(d, -1).astype(BF16)
    consts = [g.reshape(1, d).astype(F32), w1, w2t, w3]
    cols = D_HEADS * DSA_TQ
    per = blk // DSA_TQ

    def per_block(g_, a, b_):
        return (jax.ShapeDtypeStruct((1, g_, nq, a, b_), BF16),
                pl.BlockSpec((1, g_, 1, a, b_), lambda i: (0, 0, i, 0, 0)))

    def per_dsa(a, dtype):
        return (jax.ShapeDtypeStruct((1, nq_dsa, a, cols), dtype), pl.BlockSpec((1, per, a, cols), lambda i: (0, i, 0, 0)))

    blocks = [per_block(2 * C_HEADS, C_DIM, blk), per_block(2 * C_HEADS, blk, C_DIM), per_block(C_HEADS, 2 * C_DIM, blk),
              per_dsa(3 * IDX_DIM, BF16), per_dsa(1, F32), per_dsa(D_DIM, BF16)]
    return pl.pallas_call(
        _proj_odd_kernel, grid=(nq,),
        in_specs=[pl.BlockSpec((blk, d), lambda i: (i, 0))] + [pl.BlockSpec(a.shape, lambda i: (0, 0)) for a in consts],
        out_specs=[pl.BlockSpec((blk, w1.shape[1]), lambda i: (i, 0))] + [s for _, s in blocks],
        out_shape=[jax.ShapeDtypeStruct((p, w1.shape[1]), F32)] + [s for s, _ in blocks],
        compiler_params=_cparams(1), name="proj_odd")(x, *consts)


def _out_odd_kernel(x_ref, c_ref, d_ref, lam_ref, sub_ref, w_ref, o_ref, *, post_scale):
    blk = x_ref.shape[0]
    lam, sub = lam_ref[...], sub_ref[...]
    parts = []
    for hd in range(C_HEADS):
        c = c_ref[0, 2 * hd, 0] - lam * c_ref[0, 2 * hd + 1, 0]
        parts.append(c * lax.rsqrt(jnp.mean(c * c, axis=0, keepdims=True) + EPS) * sub * post_scale)
    for hd in range(D_HEADS):
        parts.append(jnp.concatenate([d_ref[0, b][:, hd * DSA_TQ:(hd + 1) * DSA_TQ]
                                      for b in range(blk // DSA_TQ)], axis=1).astype(F32))
    mix = jnp.concatenate(parts, axis=0).T.astype(BF16)
    o_ref[...] = x_ref[...] + jnp.dot(mix, w_ref[...], preferred_element_type=F32)


def out_odd(x, c_t, d_t, lam, subln, post_scale, w_out):
    p, d = x.shape
    _, nmap, nq, dc, blk = c_t.shape
    per = blk // DSA_TQ
    return pl.pallas_call(
        functools.partial(_out_odd_kernel, post_scale=post_scale), grid=(nq,),
        in_specs=[pl.BlockSpec((blk, d), lambda i: (i, 0)),
                  pl.BlockSpec((1, nmap, 1, dc, blk), lambda i: (0, 0, i, 0, 0)),
                  pl.BlockSpec((1, per) + d_t.shape[2:], lambda i: (0, i, 0, 0)),
                  pl.BlockSpec((1, blk), lambda i: (0, 0)),
                  pl.BlockSpec((dc, 1), lambda i: (0, 0)),
                  pl.BlockSpec(w_out.shape, lambda i: (0, 0))],
        out_specs=pl.BlockSpec((blk, d), lambda i: (i, 0)),
        out_shape=jax.ShapeDtypeStruct((p, d), F32),
        compiler_params=_cparams(1), name="out_odd")(
            x, c_t, d_t, jnp.full((1, blk), lam, F32), subln.reshape(dc, 1).astype(F32), w_out)


def _out_even_kernel(x_ref, a_ref, b_ref, w_ref, o_ref):
    blk = x_ref.shape[0]
    mix_t = jnp.concatenate([a_ref[0, :, 0].reshape(-1, blk), b_ref[0, :, 0].reshape(-1, blk)], axis=0)
    mix = mix_t.astype(F32).T.astype(BF16)
    o_ref[...] = x_ref[...] + jnp.dot(mix, w_ref[...], preferred_element_type=F32)


def out_even(x, a_t, b_t, w_out):
    p, d = x.shape
    _, g_, nq, dv, blk = a_t.shape
    blk_spec = pl.BlockSpec((1, g_, 1, dv, blk), lambda i: (0, 0, i, 0, 0))
    return pl.pallas_call(
        _out_even_kernel, grid=(nq,),
        in_specs=[pl.BlockSpec((blk, d), lambda i: (i, 0)), blk_spec, blk_spec,
                  pl.BlockSpec(w_out.shape, lambda i: (0, 0))],
        out_specs=pl.BlockSpec((blk, d), lambda i: (i, 0)),
        out_shape=jax.ShapeDtypeStruct((p, d), F32),
        compiler_params=_cparams(1), name="out_even")(x, a_t, b_t, w_out)


def _rope_tables(pos):
    half = B_ROPE // 2
    freqs = jnp.power(jnp.float32(ROPE_THETA), -jnp.arange(half, dtype=F32) / half)
    ang = pos.astype(F32)[:, None] * freqs[None, :]
    return jnp.cos(ang), jnp.sin(ang)


def _rope(x, cos, sin):
    half = x.shape[-1] // 2
    shape = (x.shape[0],) + (1,) * (x.ndim - 2) + (half,)
    c, s = cos.reshape(shape), sin.reshape(shape)
    x1, x2 = x[..., :half], x[..., half:]
    return jnp.concatenate([x1 * c - x2 * s, x1 * s + x2 * c], axis=-1)


def _t5_bucket(rel):
    nb = T5_BUCKETS // 2
    max_exact = nb // 2
    ret = jnp.where(rel > 0, nb, 0)
    n = jnp.abs(rel)
    nf = jnp.maximum(n, 1).astype(F32)
    large = max_exact + (jnp.log(nf / max_exact) / math.log(T5_MAX_DIST / max_exact)
                         * (nb - max_exact)).astype(I32)
    large = jnp.minimum(large, nb - 1)
    return ret + jnp.where(n < max_exact, n, large)


def _chunk_diff(tq, tk, n_tiles):
    r = np.arange(tq)[None, None, :]
    c = np.arange(tk)[None, :, None]
    t = np.arange(n_tiles)[:, None, None]
    return c // CHUNK - r // CHUNK - t * (tq // CHUNK)


def _strip_distances(tq, tk, n_tiles):
    y = np.arange(tq + tk)[None, :]
    t = np.arange(n_tiles)[:, None]
    return np.where(y < tq, -y, tq + tk - y) - t * tq


def _toeplitz(strip, tq, tk):
    n = tq + tk
    lead = strip.shape[:-1]
    rows = jnp.broadcast_to(strip[..., None, :], lead + (tk, n)).reshape(lead + (tk * n,))
    return rows[..., :tk * (n - 1)].reshape(lead + (tk, n - 1))[..., :tq]


def band_tiles(relbias, blk):
    n_tiles = (A_LEFT_CHUNKS * CHUNK) // blk + 1
    dchunk = _chunk_diff(blk, blk, n_tiles)
    adm = (dchunk <= 0) & (-dchunk <= A_LEFT_CHUNKS)
    idx = np.clip(-_strip_distances(blk, blk, n_tiles), -A_MAX_REL, A_MAX_REL) + A_MAX_REL
    bias = _toeplitz(relbias.astype(F32)[:, idx] * LOG2E, blk, blk)
    tiles = jnp.where(adm[None], bias, MASKED)
    far = jnp.full((relbias.shape[0], 1, blk, blk), MASKED, F32)
    return jnp.concatenate([tiles, far], axis=1), n_tiles


def causal_tiles(blk):
    t0 = np.where(_chunk_diff(blk, blk, 1) <= 0, 0.0, MASKED).astype(np.float32)
    return jnp.asarray(np.concatenate([t0, np.zeros_like(t0)], axis=0)[None])


def t5_tiles(tab, tq, tk, n_tiles):
    assert n_tiles * tq - tk + 1 >= T5_MAX_DIST
    tab = tab.astype(F32) * LOG2E
    strip = jnp.moveaxis(tab[_t5_bucket(jnp.asarray(_strip_distances(tq, tk, n_tiles), I32))], -1, 0)
    tiles = jnp.where((_chunk_diff(tq, tk, n_tiles) <= 0)[None], _toeplitz(strip, tq, tk), MASKED)
    far = tab[_t5_bucket(jnp.asarray([-(n_tiles * tq + tk)], I32))[0]]
    far = jnp.broadcast_to(far[:, None, None, None], (tab.shape[1], 1, tk, tq))
    return jnp.concatenate([tiles, far], axis=1)


def _row_blocks(x, blk):
    b, t, g, d = x.shape
    return jnp.transpose(x.reshape(b, t // blk, blk, g, d), (0, 3, 1, 2, 4)).astype(BF16)


def _col_blocks(x, blk):
    b, t, g, d = x.shape
    return jnp.transpose(x.reshape(b, t // blk, blk, g, d), (0, 3, 1, 4, 2)).astype(BF16)


def _from_col_blocks(o):
    b, g, nq, d, blk = o.shape
    return jnp.transpose(o, (0, 2, 4, 1, 3)).reshape(b, nq * blk, g * d)


def _pad_rows(x, n, front=0):
    back = n - x.shape[1] - front
    return jnp.pad(x, ((0, 0), (front, back)) + ((0, 0),) * (x.ndim - 2))


def _split(h, sizes):
    out, o = [], 0
    for s in sizes:
        out.append(h[:, o:o + s])
        o += s
    return out


def _split3(x):
    hi = x.astype(BF16)
    lo = (x - hi.astype(F32)).astype(BF16)
    return hi, lo


def _dsa_keys(dk_all, dv_all, dki_all):
    b, nkb = dk_all.shape[0], dk_all.shape[1] // DSA_TK
    kh, kl = _split3(dki_all)
    ki = jnp.concatenate([kh, kh, kl], axis=-1).reshape(b, nkb, DSA_TK, 3 * IDX_DIM)
    k = dk_all.astype(BF16).reshape(b, nkb, DSA_TK, D_DIM)
    vt = jnp.transpose(dv_all.astype(BF16).reshape(b, nkb, DSA_TK, D_DIM), (0, 1, 3, 2))
    return ki, k, vt


def _dsa_call(dq, dqi, dw, dk_all, dv_all, dki_all, tiles, i_off, n_sel):
    b, tq_all = dq.shape[:2]
    nq = tq_all // DSA_TQ

    def stack_t(x):
        d = x.shape[-1]
        return jnp.transpose(x.reshape(b, nq, DSA_TQ, D_HEADS, d), (0, 1, 4, 3, 2)).reshape(b, nq, d, D_HEADS * DSA_TQ)

    qh, ql = _split3(dqi)
    qit = stack_t(jnp.concatenate([qh, ql, qh], axis=-1))
    w = stack_t(dw[..., None])
    qt = stack_t((dq * (D_DIM ** -0.5 * LOG2E)).astype(BF16))
    ki, k, vt = _dsa_keys(dk_all, dv_all, dki_all)
    o = dsa(qit, w, ki, qt, k, vt, tiles, i_off=i_off, n_sel=n_sel)
    o = jnp.transpose(o.reshape(b, nq, D_DIM, D_HEADS, DSA_TQ), (0, 1, 4, 3, 2))
    return o.reshape(b, tq_all, D_HEADS * D_DIM)


P_BLK = 512
S_BLK = 128


def kernel(x_prompt, x_sample, cache_a_k, cache_a_v, cache_b_ckv, cache_b_krope, cache_c_k, cache_c_v, cache_d_k, cache_d_v, cache_d_kidx, t5_table, norm_attn, norm_ffn, final_norm, even_w_in, even_w_out, a_relbias, b_q_norm, b_kv_norm, b_w_uq, b_w_ukv, odd_w_in, odd_w_out, c_lambda_q1, c_lambda_k1, c_lambda_q2, c_lambda_k2, c_subln, ffn_w_gate, ffn_w_up, ffn_w_down):
    pb, pt, d = x_prompt.shape
    sb, st, _ = x_sample.shape
    assert pb == 1 and st == CHUNK
    past = cache_b_ckv.shape[2]
    n_a = cache_a_k.shape[2]
    depth = norm_attn.shape[0]
    n_p, n_s = pb * pt, sb * st
    s_len = past + st
    s_pad = -(-s_len // DSA_TK) * DSA_TK
    s_i = past // S_BLK
    s_i_dsa = past // DSA_TQ
    assert past % S_BLK == 0 and s_pad % S_BLK == 0 and past >= n_a
    p_blk = min(P_BLK, pt)

    cos_p, sin_p = _rope_tables(jnp.arange(pt))
    cos_s, sin_s = _rope_tables(jnp.tile(past + jnp.arange(st), sb))

    def rows_p(a):
        return a.reshape(pb, pt, *a.shape[1:])

    def rows_s(a):
        return a.reshape(sb, st, *a.shape[1:])

    def pad_cols(w):
        n = -(-w.shape[1] // LANES) * LANES
        return jnp.pad(w, ((0, 0), (0, n - w.shape[1]))).astype(BF16)

    def sample_q(a):
        return _pad_rows(a, S_BLK)

    def sample_keys(cache, new, front=0):
        return _pad_rows(jnp.concatenate([cache, new], axis=1), s_pad, front)

    def attend(q, k, v, tiles, blk, n_tiles, has_far, i_off=0):
        o = flash(_col_blocks(q, blk), _row_blocks(k, blk), _col_blocks(v, blk), tiles,
                  n_tiles=n_tiles, has_far=has_far, i_off=i_off)
        return _from_col_blocks(o)

    t5_c, t5_d = t5_table[:, :C_HEADS], t5_table[:, C_HEADS:]
    ctiles_p = causal_tiles(p_blk)
    ctiles_s = causal_tiles(S_BLK)
    c_tiles_p = t5_tiles(t5_c, p_blk, p_blk, 2)
    c_tiles_s = t5_tiles(t5_c, S_BLK, S_BLK, 2)
    d_tiles = t5_tiles(t5_d, DSA_TQ, DSA_TK, 3)
    d_tiles = jnp.transpose(d_tiles, (1, 2, 0, 3)).reshape(4, DSA_TK, D_HEADS * DSA_TQ)
    n_sel_p = min(TOPK_MAX, pt // 4)
    n_sel_s = min(TOPK_MAX, s_len // 4)

    outs_even = {k: [] for k in ("a_k_p", "a_k_s", "a_v_p", "a_v_s", "ckv_p", "ckv_s", "kr_p", "kr_s")}
    outs_odd = {k: [] for k in ("c_k_p", "c_k_s", "c_v_p", "c_v_s", "d_k_p", "d_k_s", "d_v_p", "d_v_s", "d_ki_p", "d_ki_s")}

    def mla_keys(kn, krope):
        return jnp.concatenate([kn, jnp.broadcast_to(krope[:, :, None, :], kn.shape[:3] + (B_ROPE,))], axis=-1)

    def even_prompt(x, l, i):
        a_tiles, nt = band_tiles(a_relbias[i], p_blk)
        akv, ckv, y1, y2, qta, ka, vta, qtb, kb, vtb = proj_even(
            x, norm_attn[l], even_w_in[i], b_q_norm[i], b_w_uq[i], b_kv_norm[i], b_w_ukv[i], cos_p, sin_p, p_blk)
        a_t = flash(qta, ka, vta, a_tiles, n_tiles=nt, has_far=False, out_dtype=BF16)
        b_t = flash(qtb, kb, vtb, ctiles_p, n_tiles=1, has_far=True, out_dtype=BF16)
        half = B_ROPE // 2
        hd = A_HEADS * A_DIM
        outs_even["a_k_p"].append(akv[pt - n_a:, :hd].reshape(pb, n_a, A_HEADS, A_DIM))
        outs_even["a_v_p"].append(akv[pt - n_a:, hd:].reshape(pb, n_a, A_HEADS, A_DIM))
        outs_even["ckv_p"].append(rows_p(ckv))
        outs_even["kr_p"].append(rows_p(jnp.concatenate([y1[:, :half], y2[:, :half]], axis=-1)))
        return out_even(x, a_t, b_t, even_w_out[i].astype(BF16))

    def even_sample(x, l, i):
        h = dense(x, pad_cols(even_w_in[i]), g=norm_attn[l])
        aq, ak, av, cq, ckv_raw, kr_raw = _split(h, EVEN_SPLITS)
        qb = dense(cq, b_w_uq[i].astype(BF16), g=b_q_norm[i]).reshape(-1, B_HEADS, B_NOPE + B_ROPE)
        kv_new, ckv = dense(ckv_raw, b_w_ukv[i].astype(BF16), g=b_kv_norm[i], emit_h=True)
        kr = _rope(kr_raw, cos_s, sin_s)
        q_mla = jnp.concatenate([qb[..., :B_NOPE], _rope(qb[..., B_NOPE:], cos_s, sin_s)], axis=-1)
        q_mla = q_mla * ((B_NOPE + B_ROPE) ** -0.5 * LOG2E)
        kv_s = rows_s(kv_new.reshape(-1, B_HEADS, B_NOPE + B_V))
        aq = (aq * (A_DIM ** -0.5 * LOG2E)).reshape(-1, A_HEADS, A_DIM)
        ak = ak.reshape(-1, A_HEADS, A_DIM)
        av = av.reshape(-1, A_HEADS, A_DIM)
        a_tiles, nt = band_tiles(a_relbias[i], S_BLK)
        ak_all = jnp.concatenate([cache_a_k[i], rows_s(ak)], axis=1)
        av_all = jnp.concatenate([cache_a_v[i], rows_s(av)], axis=1)
        a_out = attend(sample_q(rows_s(aq)), _pad_rows(ak_all, s_pad, past - n_a),
                       _pad_rows(av_all, s_pad, past - n_a), a_tiles, S_BLK, nt, False, s_i)[:, :st]
        kv_c = dense(cache_b_ckv[i].reshape(sb * past, KV_LORA), b_w_ukv[i].astype(BF16))
        kv_c = kv_c.reshape(sb, past, B_HEADS, B_NOPE + B_V)
        kn_all = sample_keys(kv_c[..., :B_NOPE], kv_s[..., :B_NOPE])
        v_all = sample_keys(kv_c[..., B_NOPE:], kv_s[..., B_NOPE:])
        kr_all = sample_keys(cache_b_krope[i], rows_s(kr))
        b_out = attend(sample_q(rows_s(q_mla)), mla_keys(kn_all, kr_all), v_all, ctiles_s, S_BLK, 1, True, s_i)[:, :st]
        mix = jnp.concatenate([a_out.reshape(n_s, -1), b_out.reshape(n_s, -1)], axis=-1)
        outs_even["a_k_s"].append(ak_all[:, -n_a:])
        outs_even["a_v_s"].append(av_all[:, -n_a:])
        outs_even["ckv_s"].append(rows_s(ckv))
        outs_even["kr_s"].append(rows_s(kr))
        return dense(mix.astype(BF16), even_w_out[i].astype(BF16), res=x)

    def diff_lambda(l, i):
        lam_init = 0.8 - 0.6 * math.exp(-0.3 * l)
        lam = (jnp.exp(jnp.sum(c_lambda_q1[i].astype(F32) * c_lambda_k1[i].astype(F32)))
               - jnp.exp(jnp.sum(c_lambda_q2[i].astype(F32) * c_lambda_k2[i].astype(F32))) + lam_init)
        return lam_init, lam

    def odd_prompt(x, l, i):
        lam_init, lam = diff_lambda(l, i)
        st_, cqt, kc, cvt, qit, wq, qt = proj_odd(x, norm_attn[l], odd_w_in[i], p_blk)
        wc = C_HEADS * 2 * C_DIM
        ck, cv = st_[:, :wc], st_[:, wc:2 * wc]
        dk, dv, dki = (st_[:, 2 * wc + n * LANES:2 * wc + n * LANES + D_DIM] for n in range(3))
        c_t = flash(cqt, kc, cvt, c_tiles_p, n_tiles=2, has_far=True)
        ki, k, vt = _dsa_keys(dk[None], dv[None], dki[None])
        d_t = dsa(qit, wq, ki, qt, k, vt, d_tiles, i_off=0, n_sel=n_sel_p, out_dtype=BF16)
        for name, arr, shp in (("c_k", ck, (C_HEADS, 2 * C_DIM)), ("c_v", cv, (C_HEADS, 2 * C_DIM)),
                               ("d_k", dk, (D_DIM,)), ("d_v", dv, (D_DIM,)), ("d_ki", dki, (IDX_DIM,))):
            outs_odd[name + "_p"].append(arr.reshape(pb, pt, *shp))
        return out_odd(x, c_t, d_t, lam, c_subln[i], 1.0 - lam_init, odd_w_out[i].astype(BF16))

    def odd_sample(x, l, i):
        lam_init, lam = diff_lambda(l, i)
        h = dense(x, pad_cols(odd_w_in[i]), g=norm_attn[l])
        cq, ck, cv, dq, dk, dv, dqi, dki, dw = _split(h, ODD_SPLITS)
        cq = (cq * (C_DIM ** -0.5 * LOG2E)).reshape(-1, 2 * C_HEADS, C_DIM)
        ck3 = ck.reshape(-1, 2 * C_HEADS, C_DIM)
        cv3 = cv.reshape(-1, C_HEADS, 2 * C_DIM)
        dq = dq.reshape(-1, D_HEADS, D_DIM)
        dqi = dqi.reshape(-1, IDX_HEADS, IDX_DIM)
        dw = dw * ((IDX_HEADS ** -0.5) * (IDX_DIM ** -0.5))

        def diff_combine(o):
            b_, t_, _ = o.shape
            o = o.reshape(b_, t_, C_HEADS, 2, 2 * C_DIM)
            c = o[:, :, :, 0] - lam * o[:, :, :, 1]
            return (_rms(c, c_subln[i].astype(F32)) * (1.0 - lam_init)).reshape(b_, t_, -1)

        rows, tag, nb_, nt_ = rows_s, "_s", sb, st
        ck_all = sample_keys(cache_c_k[i].reshape(sb, past, 2 * C_HEADS, C_DIM), rows(ck3))
        cv_all = sample_keys(cache_c_v[i], rows(cv3))
        c_out = attend(sample_q(rows(cq)), ck_all, cv_all, c_tiles_s, S_BLK, 2, True, s_i)[:, :st]
        d_out = _dsa_call(_pad_rows(rows(dq), DSA_TQ), _pad_rows(rows(dqi), DSA_TQ), _pad_rows(rows(dw), DSA_TQ),
                          sample_keys(cache_d_k[i], rows(dk)), sample_keys(cache_d_v[i], rows(dv)),
                          sample_keys(cache_d_kidx[i], rows(dki)), d_tiles, s_i_dsa, n_sel_s)[:, :st]
        mix = jnp.concatenate([diff_combine(c_out).reshape(nb_ * nt_, -1), d_out.reshape(nb_ * nt_, -1)], axis=-1)
        for name, arr, shp in (("c_k", ck, (C_HEADS, 2 * C_DIM)), ("c_v", cv, (C_HEADS, 2 * C_DIM)),
                               ("d_k", dk, (D_DIM,)), ("d_v", dv, (D_DIM,)), ("d_ki", dki, (IDX_DIM,))):
            outs_odd[name + tag].append(rows(arr).reshape(nb_, nt_, *shp))
        return dense(mix.astype(BF16), odd_w_out[i].astype(BF16), res=x)

    xp, xs = x_prompt.reshape(n_p, d), x_sample.reshape(n_s, d)
    for l in range(depth):
        i = l // 2
        if l % 2 == 0:
            xp, xs = even_prompt(xp, l, i), even_sample(xs, l, i)
        else:
            xp, xs = odd_prompt(xp, l, i), odd_sample(xs, l, i)
        wg, wu, wd = ffn_w_gate[l].astype(BF16), ffn_w_up[l].astype(BF16), ffn_w_down[l].astype(BF16)
        fg = final_norm if l == depth - 1 else None
        xp, xs = ffn(xp, norm_ffn[l], wg, wu, wd, final_g=fg), ffn(xs, norm_ffn[l], wg, wu, wd, final_g=fg)

    se = {k: jnp.stack(v, axis=0) for k, v in outs_even.items()}
    so = {k: jnp.stack(v, axis=0) for k, v in outs_odd.items()}
    return (xp.reshape(pb, pt, d), xs.reshape(sb, st, d), se["a_k_p"], se["a_k_s"], se["a_v_p"], se["a_v_s"],
            se["ckv_p"], se["ckv_s"], se["kr_p"], se["kr_s"],
            so["c_k_p"], so["c_k_s"], so["c_v_p"], so["c_v_s"],
            so["d_k_p"], so["d_k_s"], so["d_v_p"], so["d_v_s"], so["d_ki_p"], so["d_ki_s"])
```

```python
import functools
import math

import numpy as np
import jax
import jax.numpy as jnp
from jax import lax
from jax.experimental import pallas as pl
from jax.experimental.pallas import tpu as pltpu

F32, BF16, I32 = jnp.float32, jnp.bfloat16, jnp.int32

D_MODEL = 1024
CHUNK = 64
EPS = 1e-6
A_HEADS, A_DIM, A_LEFT_CHUNKS, A_MAX_REL = 8, 64, 8, 128
B_HEADS, B_NOPE, B_ROPE, B_V = 8, 64, 32, 64
Q_LORA, KV_LORA = 256, 128
ROPE_THETA = 10000.0
C_HEADS, C_DIM = 4, 64
D_HEADS, D_DIM = 8, 64
IDX_HEADS, IDX_DIM = 8, 64
TOPK_MAX = 256
T5_BUCKETS, T5_MAX_DIST = 32, 128
EVEN_SPLITS = (512, 512, 512, Q_LORA, KV_LORA, B_ROPE)
ODD_SPLITS = (512, 512, 512, 512, D_DIM, D_DIM, 512, IDX_DIM, IDX_HEADS)

LANES = 128
SUBLANES = 8
VMEM_LIMIT = 56 * 1024 * 1024
MASKED = -1e30
UNSELECTED = -2e30
M_INIT = -1e30
SUM_MAX, SUM_MIN = 1e18, 1e-18
LOG2E = math.log2(math.e)
INT_MIN = -(2 ** 31)
INT_MAX = 2 ** 31 - 1

DSA_TQ, DSA_TK = 128, 256
COUNT_ROWS = 8 * SUBLANES
FAR_UNROLL = 4
DSA_FAR_UNROLL = 4
NORM_ROWS = 16
DSA_SCORE_UNROLL = 4


def _cparams(n_axes):
    return pltpu.CompilerParams(dimension_semantics=("arbitrary",) * n_axes, vmem_limit_bytes=VMEM_LIMIT)


def _rms(x, g):
    return x * lax.rsqrt(jnp.mean(x * x, axis=-1, keepdims=True) + EPS) * g


def _dense_kernel(*refs, norm, res, emit_h):
    it = iter(refs)
    x_ref = next(it)
    g_ref = next(it) if norm else None
    w_ref = next(it)
    r_ref = next(it) if res else None
    o_ref = next(it)
    h_ref = next(it) if emit_h else None
    x = x_ref[...]
    if norm:
        x = _rms(x.astype(F32), g_ref[...])
        if emit_h:
            h_ref[...] = x
    acc = jnp.dot(x.astype(BF16), w_ref[...], preferred_element_type=F32)
    if res:
        acc = acc + r_ref[...]
    o_ref[...] = acc


def dense(x, w, g=None, res=None, emit_h=False, tm=512):
    m, k = x.shape
    n = w.shape[1]
    assert m % tm == 0 and n % LANES == 0
    norm = g is not None
    args, specs = [x], [pl.BlockSpec((tm, k), lambda i: (i, 0))]
    if norm:
        args.append(g.reshape(1, k).astype(F32))
        specs.append(pl.BlockSpec((1, k), lambda i: (0, 0)))
    args.append(w)
    specs.append(pl.BlockSpec((k, n), lambda i: (0, 0)))
    if res is not None:
        args.append(res)
        specs.append(pl.BlockSpec((tm, n), lambda i: (i, 0)))
    out_shape = [jax.ShapeDtypeStruct((m, n), F32)]
    out_specs = [pl.BlockSpec((tm, n), lambda i: (i, 0))]
    if emit_h:
        out_shape.append(jax.ShapeDtypeStruct((m, k), F32))
        out_specs.append(pl.BlockSpec((tm, k), lambda i: (i, 0)))
    outs = pl.pallas_call(
        functools.partial(_dense_kernel, norm=norm, res=res is not None, emit_h=emit_h),
        grid=(m // tm,), in_specs=specs, out_specs=out_specs, out_shape=out_shape,
        compiler_params=_cparams(1), name="dense")(*args)
    return outs if emit_h else outs[0]


def _ffn_kernel(x_ref, g_ref, wg_ref, wu_ref, wd_ref, fg_ref, o_ref, h_sc, acc_sc, *, final):
    f = pl.program_id(1)

    @pl.when(f == 0)
    def _():
        x = x_ref[...]
        h_sc[...] = _rms(x, g_ref[...]).astype(BF16)
        acc_sc[...] = x

    h = h_sc[...]
    gate = jnp.dot(h, wg_ref[...], preferred_element_type=F32)
    up = jnp.dot(h, wu_ref[...], preferred_element_type=F32)
    a = (gate * jax.nn.sigmoid(gate) * up).astype(BF16)
    acc_sc[...] += jnp.dot(a, wd_ref[...], preferred_element_type=F32)

    @pl.when(f == pl.num_programs(1) - 1)
    def _():
        y = acc_sc[...]
        if final:
            y = _rms(y, fg_ref[...])
        o_ref[...] = y


def ffn(x, g, wg, wu, wd, final_g=None, tm=512, nf=2):
    m, d = x.shape
    hid = wg.shape[1]
    tf = hid // nf
    assert m % tm == 0 and hid % nf == 0 and tf % LANES == 0
    final = final_g is not None
    fg = (final_g if final else g).reshape(1, d).astype(F32)
    return pl.pallas_call(
        functools.partial(_ffn_kernel, final=final),
        grid=(m // tm, nf),
        in_specs=[pl.BlockSpec((tm, d), lambda i, f: (i, 0)),
                  pl.BlockSpec((1, d), lambda i, f: (0, 0)),
                  pl.BlockSpec((d, tf), lambda i, f: (0, f)),
                  pl.BlockSpec((d, tf), lambda i, f: (0, f)),
                  pl.BlockSpec((tf, d), lambda i, f: (f, 0)),
                  pl.BlockSpec((1, d), lambda i, f: (0, 0))],
        out_specs=pl.BlockSpec((tm, d), lambda i, f: (i, 0)),
        out_shape=jax.ShapeDtypeStruct((m, d), F32),
        scratch_shapes=[pltpu.VMEM((tm, d), BF16), pltpu.VMEM((tm, d), F32)],
        compiler_params=_cparams(2), name="ffn")(x, g.reshape(1, d).astype(F32), wg, wu, wd, fg)


def _flash_kernel(qt_ref, k_ref, vt_ref, tile_ref, o_ref, m_sc, l_sc, acc_sc, *, n_tiles, has_far, i_off,
                  heads, kdiv, vdiv, bdiv):
    i = pl.program_id(2) + i_off
    j_lo = 0 if has_far else jnp.maximum(i - (n_tiles - 1), 0)

    def one_head(g):
        gk, gv, gt = g // kdiv, g // vdiv, g // bdiv
        qt = qt_ref[0, g, 0]

        def raw(j):
            return jnp.dot(k_ref[0, gk, j], qt, preferred_element_type=F32)

        def scores(j):
            return raw(j) + tile_ref[gt, jnp.minimum(i - j, n_tiles)]

        def reset():
            l_sc[...] = jnp.zeros(l_sc.shape, F32)
            acc_sc[...] = jnp.zeros(acc_sc.shape, F32)

        def accumulate(j, p):
            l_sc[...] += jnp.sum(p, axis=0, keepdims=True)
            acc_sc[...] += jnp.dot(vt_ref[0, gv, j], p.astype(BF16), preferred_element_type=F32)

        reset()
        s_diag = raw(i) + tile_ref[gt, 0]
        m = jnp.max(s_diag, axis=0, keepdims=True)
        accumulate(i, jnp.exp2(s_diag - m))
        if has_far:
            m_far = m - tile_ref[gt, n_tiles, 0:1, :]
            n_far = jnp.maximum(i - (n_tiles - 1), 0)

            def far_group(jj, carry):
                ps = [jnp.exp2(raw(FAR_UNROLL * jj + u) - m_far) for u in range(FAR_UNROLL)]
                for u in range(FAR_UNROLL):
                    accumulate(FAR_UNROLL * jj + u, ps[u])
                return carry

            def far_single(j, carry):
                accumulate(j, jnp.exp2(raw(j) - m_far))
                return carry

            n_grouped = (n_far // FAR_UNROLL) * FAR_UNROLL
            lax.fori_loop(0, n_far // FAR_UNROLL, far_group, 0)
            lax.fori_loop(n_grouped, n_far, far_single, 0)
        for t in range(1, n_tiles):
            @pl.when(i - t >= 0)
            def _(t=t):
                accumulate(i - t, jnp.exp2(raw(i - t) + tile_ref[gt, t] - m))

        l = l_sc[...]
        in_range = (jnp.max(l) < SUM_MAX) & (jnp.min(l) > SUM_MIN)

        @pl.when(jnp.logical_not(in_range))
        def _():
            reset()
            m_sc[...] = jnp.full(m_sc.shape, M_INIT, F32)

            def body(j, carry):
                s = scores(j)
                m_prev = m_sc[...]
                m_new = jnp.maximum(m_prev, jnp.max(s, axis=0, keepdims=True))
                alpha = jnp.exp2(m_prev - m_new)
                p = jnp.exp2(s - m_new)
                l_sc[...] = alpha * l_sc[...] + jnp.sum(p, axis=0, keepdims=True)
                acc_sc[...] = alpha * acc_sc[...] + jnp.dot(vt_ref[0, gv, j], p.astype(BF16),
                                                            preferred_element_type=F32)
                m_sc[...] = m_new
                return carry

            lax.fori_loop(j_lo, i + 1, body, 0)

        o_ref[0, g, 0] = (acc_sc[...] / l_sc[...]).astype(o_ref.dtype)

    if heads == 1:
        one_head(0)
    else:
        def head_step(g, carry):
            one_head(g)
            return carry
        lax.fori_loop(0, heads, head_step, 0)


def flash(qt, k, vt, tiles, *, n_tiles, has_far, i_off=0, out_dtype=F32, heads=1):
    b, g, nq, dq, blk = qt.shape
    gk, nb = k.shape[1], k.shape[2]
    gv, dv = vt.shape[1], vt.shape[3]
    gb = tiles.shape[0]
    assert tiles.shape[1] == n_tiles + 1
    kdiv, vdiv, bdiv = g // gk, g // gv, g // gb
    assert g % heads == 0 and (heads == 1 or (heads % kdiv == 0 and heads % vdiv == 0 and heads % bdiv == 0))
    hk, hv, hb = max(heads // kdiv, 1), max(heads // vdiv, 1), max(heads // bdiv, 1)
    kd, vd, bd = (kdiv, vdiv, bdiv) if heads == 1 else (1, 1, 1)
    return pl.pallas_call(
        functools.partial(_flash_kernel, n_tiles=n_tiles, has_far=has_far, i_off=i_off,
                          heads=heads, kdiv=kdiv, vdiv=vdiv, bdiv=bdiv),
        grid=(b, g // heads, nq),
        in_specs=[pl.BlockSpec((1, heads, 1, dq, blk), lambda bi, gi, i: (bi, gi, i, 0, 0)),
                  pl.BlockSpec((1, hk, nb, blk, dq), lambda bi, gi, i: (bi, gi // kd, 0, 0, 0)),
                  pl.BlockSpec((1, hv, nb, dv, blk), lambda bi, gi, i: (bi, gi // vd, 0, 0, 0)),
                  pl.BlockSpec((hb, n_tiles + 1, blk, blk), lambda bi, gi, i: (gi // bd, 0, 0, 0))],
        out_specs=pl.BlockSpec((1, heads, 1, dv, blk), lambda bi, gi, i: (bi, gi, i, 0, 0)),
        out_shape=jax.ShapeDtypeStruct((b, g, nq, dv, blk), out_dtype),
        scratch_shapes=[pltpu.VMEM((1, blk), F32), pltpu.VMEM((1, blk), F32), pltpu.VMEM((dv, blk), F32)],
        compiler_params=_cparams(3), name="flash")(qt, k, vt, tiles)


def _dsa_kernel(qit_ref, w_ref, ki_ref, qt_ref, k_ref, vt_ref, tile_ref, o_ref,
                key_sc, m_sc, acc_sc, *, i_off, n_sel, idx_bits):
    tq, tk, nh = DSA_TQ, DSA_TK, D_HEADS
    i = pl.program_id(1) + i_off
    nkb = (i * tq) // tk + 1
    krow = lax.broadcasted_iota(I32, (tk, tq), 0)
    qcol = lax.broadcasted_iota(I32, (tk, tq), 1)
    q_chunk = jnp.right_shift(i * tq + qcol, 6)

    def head(x, h):
        return x[:, h * tq:(h + 1) * tq]

    qit = qit_ref[0, 0]
    w = w_ref[0, 0]

    def score_block(j, mask_future):
        lg = jnp.maximum(jnp.dot(ki_ref[0, j], qit, preferred_element_type=F32), 0.0) * w
        sc = head(lg, 0)
        for h in range(1, IDX_HEADS):
            sc = sc + head(lg, h)
        sc = jnp.where(sc == 0.0, 0.0, sc)
        bits = pltpu.bitcast(sc, I32)
        key = bits ^ (jnp.right_shift(bits, 31) & INT_MAX)
        if mask_future:
            key = jnp.where(jnp.right_shift(j * tk + krow, 6) <= q_chunk, key, INT_MIN)
        key_sc[j] = key

    def score_group(jj, carry):
        for u in range(DSA_SCORE_UNROLL):
            score_block(DSA_SCORE_UNROLL * jj + u, False)
        return carry

    def score_single(j, carry):
        score_block(j, False)
        return carry

    n_full = nkb - 1
    lax.fori_loop(0, n_full // DSA_SCORE_UNROLL, score_group, 0)
    lax.fori_loop((n_full // DSA_SCORE_UNROLL) * DSA_SCORE_UNROLL, n_full, score_single, 0)
    score_block(n_full, True)

    def count(pred_fn):
        def body(j, acc):
            ind = jnp.where(pred_fn(key_sc[j], j * tk + krow), 1.0, 0.0)
            return acc + jnp.sum(ind.reshape(tk // COUNT_ROWS, COUNT_ROWS, tq), axis=0)
        acc = lax.fori_loop(0, nkb, body, jnp.zeros((COUNT_ROWS, tq), F32))
        return jnp.sum(acc, axis=0, keepdims=True)

    target = float(n_sel)
    qpos = i * tq + lax.broadcasted_iota(I32, (1, tq), 1)
    n_adm = ((jnp.right_shift(qpos, 6) + 1) * CHUNK).astype(F32)
    real = n_adm > target

    def unsettled(c_thr):
        settled = (c_thr == target) | jnp.logical_not(real)
        return jnp.min(jnp.where(settled, 1.0, 0.0)) < 1.0

    def value_bit(state):
        thr, c_thr, b = state
        cand = thr + jnp.left_shift(jnp.int32(1), 31 - b)
        cnt = count(lambda k, idx: k >= cand)
        ge = cnt >= target
        return jnp.where(ge, cand, thr), jnp.where(ge, cnt, c_thr), b + 1

    thr, c_thr, _ = lax.while_loop(lambda s: (s[2] < 32) & unsettled(s[1]), value_bit,
                                   (jnp.full((1, tq), INT_MIN, I32), n_adm, jnp.int32(0)))

    def tie_search():
        need = target - count(lambda k, idx: k > thr)

        def index_bit(b, y):
            cand = y + jnp.left_shift(jnp.int32(1), idx_bits - 1 - b)
            cnt = count(lambda k, idx: (k == thr) & (idx < cand))
            return jnp.where(cnt < need, cand, y)
        return lax.fori_loop(0, idx_bits, index_bit, jnp.zeros((1, tq), I32))

    has_ties = jnp.max(jnp.where((c_thr > target) & real, 1.0, 0.0)) > 0.0
    last = lax.cond(has_ties, tie_search, lambda: jnp.full((1, tq), INT_MAX, I32))
    last = jnp.where(real, last, -1)

    qt = qt_ref[0, 0]
    n_near = 3

    def raw(j):
        return jnp.dot(k_ref[0, j], qt, preferred_element_type=F32)

    def tile_of(j):
        return jnp.minimum((i * tq - j * tk) // tq, n_near)

    def select(j, s):
        key = key_sc[j]
        sel = (key > thr) | ((key == thr) & (j * tk + krow <= last))
        selb = jnp.where(sel, 0.0, UNSELECTED)
        return jnp.concatenate([head(s, h) + selb for h in range(nh)], axis=1)

    dv = o_ref.shape[2]
    def reset():
        acc_sc[...] = jnp.zeros(acc_sc.shape, F32)

    def accumulate(j, p):
        acc_sc[...] += jnp.dot(vt_ref[0, j], p.astype(BF16), preferred_element_type=F32)

    reset()
    s_last = raw(nkb - 1) + tile_ref[tile_of(nkb - 1)]
    m = jnp.max(s_last, axis=0, keepdims=True)
    accumulate(nkb - 1, jnp.exp2(select(nkb - 1, s_last) - m))
    m_far = m - tile_ref[n_near, 0:1, :]
    n_far = jnp.maximum(i - 1, 0) // 2

    def far_group(jj, carry):
        ps = [jnp.exp2(select(DSA_FAR_UNROLL * jj + u, raw(DSA_FAR_UNROLL * jj + u)) - m_far)
              for u in range(DSA_FAR_UNROLL)]
        for u in range(DSA_FAR_UNROLL):
            accumulate(DSA_FAR_UNROLL * jj + u, ps[u])
        return carry

    def far_single(j, carry):
        accumulate(j, jnp.exp2(select(j, raw(j)) - m_far))
        return carry

    def near(j, carry):
        accumulate(j, jnp.exp2(select(j, raw(j) + tile_ref[tile_of(j)]) - m))
        return carry

    n_grouped = (n_far // DSA_FAR_UNROLL) * DSA_FAR_UNROLL
    lax.fori_loop(0, n_far // DSA_FAR_UNROLL, far_group, 0)
    lax.fori_loop(n_grouped, n_far, far_single, 0)
    lax.fori_loop(n_far, nkb - 1, near, 0)

    l = acc_sc[dv:dv + 1, :]
    in_range = (jnp.max(l) < SUM_MAX) & (jnp.min(l) > SUM_MIN)

    @pl.when(jnp.logical_not(in_range))
    def _():
        reset()
        m_sc[...] = jnp.full(m_sc.shape, M_INIT, F32)

        def attend(j, carry):
            s = select(j, raw(j) + tile_ref[tile_of(j)])
            m_prev = m_sc[...]
            m_new = jnp.maximum(m_prev, jnp.max(s, axis=0, keepdims=True))
            alpha = jnp.exp2(m_prev - m_new)
            p = jnp.exp2(s - m_new)
            acc_sc[...] = alpha * acc_sc[...] + jnp.dot(vt_ref[0, j], p.astype(BF16), preferred_element_type=F32)
            m_sc[...] = m_new
            return carry

        lax.fori_loop(0, nkb, attend, 0)

    o_ref[0, 0] = (acc_sc[0:dv, :] / acc_sc[dv:dv + 1, :]).astype(o_ref.dtype)


def dsa(qit, w, ki, qt, k, vt, tiles, *, i_off, n_sel, out_dtype=F32):
    b, nq, kdim, cols = qit.shape
    nkb = ki.shape[1]
    dk, dva = k.shape[-1], vt.shape[2]
    dv = dva - NORM_ROWS
    idx_bits = max(1, (nkb * DSA_TK - 1).bit_length())
    return pl.pallas_call(
        functools.partial(_dsa_kernel, i_off=i_off, n_sel=n_sel, idx_bits=idx_bits),
        grid=(b, nq),
        in_specs=[pl.BlockSpec((1, 1, kdim, cols), lambda bi, i: (bi, i, 0, 0)),
                  pl.BlockSpec((1, 1, 1, cols), lambda bi, i: (bi, i, 0, 0)),
                  pl.BlockSpec((1, nkb, DSA_TK, kdim), lambda bi, i: (bi, 0, 0, 0)),
                  pl.BlockSpec((1, 1, dk, cols), lambda bi, i: (bi, i, 0, 0)),
                  pl.BlockSpec((1, nkb, DSA_TK, dk), lambda bi, i: (bi, 0, 0, 0)),
                  pl.BlockSpec((1, nkb, dva, DSA_TK), lambda bi, i: (bi, 0, 0, 0)),
                  pl.BlockSpec((4, DSA_TK, cols), lambda bi, i: (0, 0, 0))],
        out_specs=pl.BlockSpec((1, 1, dv, cols), lambda bi, i: (bi, i, 0, 0)),
        out_shape=jax.ShapeDtypeStruct((b, nq, dv, cols), out_dtype),
        scratch_shapes=[pltpu.VMEM((nkb, DSA_TK, DSA_TQ), I32), pltpu.VMEM((1, cols), F32),
                        pltpu.VMEM((dva, cols), F32)],
        compiler_params=_cparams(2), name="dsa")(qit, w, ki, qt, k, vt, tiles)


_NT = (((1,), (1,)), ((), ()))


def _heads(x, n, d):
    return x.reshape(n, d, x.shape[-1])


def _proj_even_kernel(x_ref, g_ref, w1_ref, w2t_ref, w3_ref, qn_ref, wuqt_ref, kvn_ref, wkb_ref, e1_ref, e2_ref,
                      wuvt_ref, cosn_ref, sinn_ref, cost_ref, sint_ref,
                      akv_ref, ckv_ref, y1_ref, y2_ref, qta_ref, ka_ref, vta_ref, qtb_ref, kb_ref, vtb_ref):
    nh, da, rope_half = A_HEADS, A_DIM, B_ROPE // 2
    h = _rms(x_ref[...], g_ref[...]).astype(BF16)
    h1 = jnp.dot(h, w1_ref[...], preferred_element_type=F32)
    akv_ref[...] = h1[:, :2 * nh * da]
    t2 = lax.dot_general(w2t_ref[...], h, _NT, preferred_element_type=F32)
    qta_ref[0, :, 0] = _heads(t2[:nh * da], nh, da).astype(BF16)
    vta_ref[0, :, 0] = _heads(t2[nh * da:], nh, da).astype(BF16)
    k3 = jnp.dot(h, w3_ref[...], preferred_element_type=F32)
    for hd in range(nh):
        ka_ref[0, hd, 0] = k3[:, hd * LANES:hd * LANES + da].astype(BF16)

    c0 = 2 * nh * da
    cqn = _rms(h1[:, c0:c0 + Q_LORA], qn_ref[...]).astype(BF16)
    tq = lax.dot_general(wuqt_ref[...], cqn, _NT, preferred_element_type=F32)
    n_nope, n_rope = B_HEADS * B_NOPE, B_HEADS * rope_half
    x1, x2 = tq[n_nope:n_nope + n_rope], tq[n_nope + n_rope:]
    ct, st = cost_ref[0], sint_ref[0]
    y1t, y2t = x1 * ct - x2 * st, x1 * st + x2 * ct
    for hd in range(B_HEADS):
        qtb_ref[0, hd, 0] = jnp.concatenate(
            [tq[hd * B_NOPE:(hd + 1) * B_NOPE], y1t[hd * rope_half:(hd + 1) * rope_half],
             y2t[hd * rope_half:(hd + 1) * rope_half]], axis=0).astype(BF16)

    c1 = c0 + Q_LORA
    ckv = _rms(h1[:, c1:c1 + KV_LORA], kvn_ref[...])
    ckv_ref[...] = ckv
    ckvb = ckv.astype(BF16)
    k1, k2 = h1[:, c1 + KV_LORA:c1 + KV_LORA + LANES], h1[:, c1 + KV_LORA + LANES:]
    cn, sn = cosn_ref[...], sinn_ref[...]
    y1, y2 = k1 * cn - k2 * sn, k1 * sn + k2 * cn
    y1_ref[...] = y1
    y2_ref[...] = y2
    kb = (jnp.dot(ckvb, wkb_ref[...], preferred_element_type=F32)
          + jnp.dot(y1.astype(BF16), e1_ref[...], preferred_element_type=F32)
          + jnp.dot(y2.astype(BF16), e2_ref[...], preferred_element_type=F32))
    for hd in range(B_HEADS):
        kb_ref[0, hd, 0] = kb[:, hd * LANES:hd * LANES + B_NOPE + B_ROPE].astype(BF16)
    tv = lax.dot_general(wuvt_ref[...], ckvb, _NT, preferred_element_type=F32)
    vtb_ref[0, :, 0] = _heads(tv, B_HEADS, B_V).astype(BF16)


def proj_even(x, g, w_in, q_norm, w_uq, kv_norm, w_ukv, cos, sin, blk):
    p, d = x.shape
    nq = p // blk
    nh, da, half = A_HEADS, A_DIM, B_ROPE // 2
    s_a = A_DIM ** -0.5 * LOG2E
    s_b = (B_NOPE + B_ROPE) ** -0.5 * LOG2E
    aq_w, ak_w, av_w, cq_w, ckv_w, kr_w = _split(w_in, EVEN_SPLITS)

    def lane_pad(w, n=LANES):
        return jnp.pad(w, ((0, 0), (0, n - w.shape[1])))

    def head_chunks(w, dh):
        k = w.shape[0]
        return jnp.pad(w.reshape(k, -1, dh), ((0, 0), (0, 0), (0, LANES - dh))).reshape(k, -1)

    w1 = jnp.concatenate([ak_w, av_w, cq_w, ckv_w, lane_pad(kr_w[:, :half]), lane_pad(kr_w[:, half:])], axis=1).astype(BF16)
    w2t = jnp.concatenate([aq_w * s_a, av_w], axis=1).T.astype(BF16)
    w3 = head_chunks(ak_w, da).astype(BF16)
    uq = (w_uq * s_b).reshape(Q_LORA, B_HEADS, B_NOPE + B_ROPE)
    wuqt = jnp.concatenate([uq[:, :, :B_NOPE].reshape(Q_LORA, -1), uq[:, :, B_NOPE:B_NOPE + half].reshape(Q_LORA, -1),
                            uq[:, :, B_NOPE + half:].reshape(Q_LORA, -1)], axis=1).T.astype(BF16)
    ukv = w_ukv.reshape(KV_LORA, B_HEADS, B_NOPE + B_V)
    wkb = head_chunks(ukv[:, :, :B_NOPE].reshape(KV_LORA, -1), B_NOPE).astype(BF16)
    wuvt = ukv[:, :, B_NOPE:].reshape(KV_LORA, -1).T.astype(BF16)
    eye = np.zeros((2, LANES, B_HEADS, LANES), np.float32)
    for r in range(half):
        eye[0, r, :, B_NOPE + r] = 1.0
        eye[1, r, :, B_NOPE + half + r] = 1.0
    e1, e2 = (jnp.asarray(e.reshape(LANES, B_HEADS * LANES), BF16) for e in eye)
    cosn, sinn = lane_pad(cos), lane_pad(sin)
    cost = jnp.tile(cos.reshape(nq, blk, half).transpose(0, 2, 1), (1, B_HEADS, 1))
    sint = jnp.tile(sin.reshape(nq, blk, half).transpose(0, 2, 1), (1, B_HEADS, 1))

    def full(a):
        return pl.BlockSpec(a.shape, lambda i: (0,) * a.ndim)

    def rows(n):
        return pl.BlockSpec((blk, n), lambda i: (i, 0))

    def tile3(a):
        return pl.BlockSpec((1,) + a.shape[1:], lambda i: (i, 0, 0))

    def per_block(g_, a, b_):
        return (jax.ShapeDtypeStruct((1, g_, nq, a, b_), BF16),
                pl.BlockSpec((1, g_, 1, a, b_), lambda i: (0, 0, i, 0, 0)))

    consts = [g.reshape(1, d).astype(F32), w1, w2t, w3, q_norm.reshape(1, -1).astype(F32), wuqt,
              kv_norm.reshape(1, -1).astype(F32), wkb, e1, e2, wuvt]
    blocks = [per_block(nh, da, blk), per_block(nh, blk, da), per_block(nh, da, blk),
              per_block(B_HEADS, B_NOPE + B_ROPE, blk), per_block(B_HEADS, blk, B_NOPE + B_ROPE),
              per_block(B_HEADS, B_V, blk)]
    out_shape = [jax.ShapeDtypeStruct((p, 2 * nh * da), F32), jax.ShapeDtypeStruct((p, KV_LORA), F32),
                 jax.ShapeDtypeStruct((p, LANES), F32), jax.ShapeDtypeStruct((p, LANES), F32)] + [s for s, _ in blocks]
    out_specs = [rows(2 * nh * da), rows(KV_LORA), rows(LANES), rows(LANES)] + [s for _, s in blocks]
    return pl.pallas_call(
        _proj_even_kernel, grid=(nq,),
        in_specs=[rows(d)] + [full(a) for a in consts] + [rows(LANES), rows(LANES), tile3(cost), tile3(sint)],
        out_specs=out_specs, out_shape=out_shape,
        compiler_params=_cparams(1), name="proj_even")(x, *consts, cosn, sinn, cost, sint)


def _proj_odd_kernel(x_ref, g_ref, w1_ref, w2t_ref, w3_ref,
                     st_ref, cqt_ref, kc_ref, cvt_ref, qit_ref, wq_ref, qt_ref):
    nmap, width = 2 * C_HEADS, C_HEADS * 2 * C_DIM
    h = _rms(x_ref[...], g_ref[...]).astype(BF16)
    st_ref[...] = jnp.dot(h, w1_ref[...], preferred_element_type=F32)
    t2 = lax.dot_general(w2t_ref[...], h, _NT, preferred_element_type=F32)
    cqt_ref[0, :, 0] = _heads(t2[:width], nmap, C_DIM).astype(BF16)
    cvt_ref[0, :, 0] = _heads(t2[width:2 * width], C_HEADS, 2 * C_DIM).astype(BF16)
    k3 = jnp.dot(h, w3_ref[...], preferred_element_type=F32)
    for m in range(nmap):
        kc_ref[0, m, 0] = k3[:, m * LANES:m * LANES + C_DIM].astype(BF16)

    dqt = t2[2 * width:2 * width + D_HEADS * D_DIM]
    dqit = t2[2 * width + D_HEADS * D_DIM:2 * width + D_HEADS * (D_DIM + IDX_DIM)]
    dwt = t2[2 * width + D_HEADS * (D_DIM + IDX_DIM):]

    def stack(x, rows_per_head, n_rows, b):
        cols = slice(b * DSA_TQ, (b + 1) * DSA_TQ)
        return jnp.concatenate([x[hd * rows_per_head:hd * rows_per_head + n_rows, cols] for hd in range(D_HEADS)], axis=1)

    for b in range(x_ref.shape[0] // DSA_TQ):
        qt_ref[0, b] = stack(dqt, D_DIM, D_DIM, b).astype(BF16)
        qi = stack(dqit, IDX_DIM, IDX_DIM, b)
        hi = qi.astype(BF16)
        lo = (qi - hi.astype(F32)).astype(BF16)
        qit_ref[0, b] = jnp.concatenate([hi, lo, hi], axis=0)
        wq_ref[0, b] = stack(dwt, SUBLANES, 1, b)


def proj_odd(x, g, w_in, blk):
    p, d = x.shape
    nq, nq_dsa = p // blk, p // DSA_TQ
    s_c = C_DIM ** -0.5 * LOG2E
    s_d = D_DIM ** -0.5 * LOG2E
    s_w = (IDX_HEADS ** -0.5) * (IDX_DIM ** -0.5)
    cq_w, ck_w, cv_w, dq_w, dk_w, dv_w, dqi_w, dki_w, dw_w = _split(w_in, ODD_SPLITS)

    def lane_pad(w):
        return jnp.pad(w, ((0, 0), (0, LANES - w.shape[1])))

    w1 = jnp.concatenate([ck_w, cv_w, lane_pad(dk_w), lane_pad(dv_w), lane_pad(dki_w)], axis=1).astype(BF16)
    dw_rows = jnp.pad((dw_w * s_w)[:, :, None], ((0, 0), (0, 0), (0, SUBLANES - 1))).reshape(d, -1)
    w2t = jnp.concatenate([cq_w * s_c, cv_w, dq_w * s_d, dqi_w, dw_rows], axis=1).T.astype(BF16)
    w3 = jnp.pad(ck_w.reshape(d, -1, C_DIM), ((0, 0), (0, 0), (0, LANES - C_DIM))).reshape(d, -1).astype(BF16)
    consts = [g.reshape(1, d).astype(F32), w1, w2t, w3]
    cols = D_HEADS * DSA_TQ
    per = blk // DSA_TQ

    def per_block(g_, a, b_):
        return (jax.ShapeDtypeStruct((1, g_, nq, a, b_), BF16),
                pl.BlockSpec((1, g_, 1, a, b_), lambda i: (0, 0, i, 0, 0)))

    def per_dsa(a, dtype):
        return (jax.ShapeDtypeStruct((1, nq_dsa, a, cols), dtype), pl.BlockSpec((1, per, a, cols), lambda i: (0, i, 0, 0)))

    blocks = [per_block(2 * C_HEADS, C_DIM, blk), per_block(2 * C_HEADS, blk, C_DIM), per_block(C_HEADS, 2 * C_DIM, blk),
              per_dsa(3 * IDX_DIM, BF16), per_dsa(1, F32), per_dsa(D_DIM, BF16)]
    return pl.pallas_call(
        _proj_odd_kernel, grid=(nq,),
        in_specs=[pl.BlockSpec((blk, d), lambda i: (i, 0))] + [pl.BlockSpec(a.shape, lambda i: (0, 0)) for a in consts],
        out_specs=[pl.BlockSpec((blk, w1.shape[1]), lambda i: (i, 0))] + [s for _, s in blocks],
        out_shape=[jax.ShapeDtypeStruct((p, w1.shape[1]), F32)] + [s for s, _ in blocks],
        compiler_params=_cparams(1), name="proj_odd")(x, *consts)


def _out_odd_kernel(x_ref, c_ref, d_ref, lam_ref, sub_ref, w_ref, o_ref, *, post_scale):
    blk = x_ref.shape[0]
    lam, sub = lam_ref[...], sub_ref[...]
    parts = []
    for hd in range(C_HEADS):
        c = c_ref[0, 2 * hd, 0] - lam * c_ref[0, 2 * hd + 1, 0]
        parts.append(c * lax.rsqrt(jnp.mean(c * c, axis=0, keepdims=True) + EPS) * sub * post_scale)
    for hd in range(D_HEADS):
        parts.append(jnp.concatenate([d_ref[0, b][:, hd * DSA_TQ:(hd + 1) * DSA_TQ]
                                      for b in range(blk // DSA_TQ)], axis=1).astype(F32))
    mix = jnp.concatenate(parts, axis=0).T.astype(BF16)
    o_ref[...] = x_ref[...] + jnp.dot(mix, w_ref[...], preferred_element_type=F32)


def out_odd(x, c_t, d_t, lam, subln, post_scale, w_out):
    p, d = x.shape
    _, nmap, nq, dc, blk = c_t.shape
    per = blk // DSA_TQ
    return pl.pallas_call(
        functools.partial(_out_odd_kernel, post_scale=post_scale), grid=(nq,),
        in_specs=[pl.BlockSpec((blk, d), lambda i: (i, 0)),
                  pl.BlockSpec((1, nmap, 1, dc, blk), lambda i: (0, 0, i, 0, 0)),
                  pl.BlockSpec((1, per) + d_t.shape[2:], lambda i: (0, i, 0, 0)),
                  pl.BlockSpec((1, blk), lambda i: (0, 0)),
                  pl.BlockSpec((dc, 1), lambda i: (0, 0)),
                  pl.BlockSpec(w_out.shape, lambda i: (0, 0))],
        out_specs=pl.BlockSpec((blk, d), lambda i: (i, 0)),
        out_shape=jax.ShapeDtypeStruct((p, d), F32),
        compiler_params=_cparams(1), name="out_odd")(
            x, c_t, d_t, jnp.full((1, blk), lam, F32), subln.reshape(dc, 1).astype(F32), w_out)


def _out_even_kernel(x_ref, a_ref, b_ref, w_ref, o_ref):
    blk = x_ref.shape[0]
    mix_t = jnp.concatenate([a_ref[0, :, 0].reshape(-1, blk), b_ref[0, :, 0].reshape(-1, blk)], axis=0)
    mix = mix_t.astype(F32).T.astype(BF16)
    o_ref[...] = x_ref[...] + jnp.dot(mix, w_ref[...], preferred_element_type=F32)


def out_even(x, a_t, b_t, w_out):
    p, d = x.shape
    _, g_, nq, dv, blk = a_t.shape
    blk_spec = pl.BlockSpec((1, g_, 1, dv, blk), lambda i: (0, 0, i, 0, 0))
    return pl.pallas_call(
        _out_even_kernel, grid=(nq,),
        in_specs=[pl.BlockSpec((blk, d), lambda i: (i, 0)), blk_spec, blk_spec,
                  pl.BlockSpec(w_out.shape, lambda i: (0, 0))],
        out_specs=pl.BlockSpec((blk, d), lambda i: (i, 0)),
        out_shape=jax.ShapeDtypeStruct((p, d), F32),
        compiler_params=_cparams(1), name="out_even")(x, a_t, b_t, w_out)


def _rope_tables(pos):
    half = B_ROPE // 2
    freqs = jnp.power(jnp.float32(ROPE_THETA), -jnp.arange(half, dtype=F32) / half)
    ang = pos.astype(F32)[:, None] * freqs[None, :]
    return jnp.cos(ang), jnp.sin(ang)


def _rope(x, cos, sin):
    half = x.shape[-1] // 2
    shape = (x.shape[0],) + (1,) * (x.ndim - 2) + (half,)
    c, s = cos.reshape(shape), sin.reshape(shape)
    x1, x2 = x[..., :half], x[..., half:]
    return jnp.concatenate([x1 * c - x2 * s, x1 * s + x2 * c], axis=-1)


def _t5_bucket(rel):
    nb = T5_BUCKETS // 2
    max_exact = nb // 2
    ret = jnp.where(rel > 0, nb, 0)
    n = jnp.abs(rel)
    nf = jnp.maximum(n, 1).astype(F32)
    large = max_exact + (jnp.log(nf / max_exact) / math.log(T5_MAX_DIST / max_exact)
                         * (nb - max_exact)).astype(I32)
    large = jnp.minimum(large, nb - 1)
    return ret + jnp.where(n < max_exact, n, large)


def _chunk_diff(tq, tk, n_tiles):
    r = np.arange(tq)[None, None, :]
    c = np.arange(tk)[None, :, None]
    t = np.arange(n_tiles)[:, None, None]
    return c // CHUNK - r // CHUNK - t * (tq // CHUNK)


def _strip_distances(tq, tk, n_tiles):
    y = np.arange(tq + tk)[None, :]
    t = np.arange(n_tiles)[:, None]
    return np.where(y < tq, -y, tq + tk - y) - t * tq


def _toeplitz(strip, tq, tk):
    n = tq + tk
    lead = strip.shape[:-1]
    rows = jnp.broadcast_to(strip[..., None, :], lead + (tk, n)).reshape(lead + (tk * n,))
    return rows[..., :tk * (n - 1)].reshape(lead + (tk, n - 1))[..., :tq]


def band_tiles(relbias, blk):
    n_tiles = (A_LEFT_CHUNKS * CHUNK) // blk + 1
    dchunk = _chunk_diff(blk, blk, n_tiles)
    adm = (dchunk <= 0) & (-dchunk <= A_LEFT_CHUNKS)
    idx = np.clip(-_strip_distances(blk, blk, n_tiles), -A_MAX_REL, A_MAX_REL) + A_MAX_REL
    bias = _toeplitz(relbias.astype(F32)[:, idx] * LOG2E, blk, blk)
    tiles = jnp.where(adm[None], bias, MASKED)
    far = jnp.full((relbias.shape[0], 1, blk, blk), MASKED, F32)
    return jnp.concatenate([tiles, far], axis=1), n_tiles


def causal_tiles(blk):
    t0 = np.where(_chunk_diff(blk, blk, 1) <= 0, 0.0, MASKED).astype(np.float32)
    return jnp.asarray(np.concatenate([t0, np.zeros_like(t0)], axis=0)[None])


def t5_tiles(tab, tq, tk, n_tiles):
    assert n_tiles * tq - tk + 1 >= T5_MAX_DIST
    tab = tab.astype(F32) * LOG2E
    strip = jnp.moveaxis(tab[_t5_bucket(jnp.asarray(_strip_distances(tq, tk, n_tiles), I32))], -1, 0)
    tiles = jnp.where((_chunk_diff(tq, tk, n_tiles) <= 0)[None], _toeplitz(strip, tq, tk), MASKED)
    far = tab[_t5_bucket(jnp.asarray([-(n_tiles * tq + tk)], I32))[0]]
    far = jnp.broadcast_to(far[:, None, None, None], (tab.shape[1], 1, tk, tq))
    return jnp.concatenate([tiles, far], axis=1)


def _row_blocks(x, blk):
    b, t, g, d = x.shape
    return jnp.transpose(x.reshape(b, t // blk, blk, g, d), (0, 3, 1, 2, 4)).astype(BF16)


def _col_blocks(x, blk):
    b, t, g, d = x.shape
    return jnp.transpose(x.reshape(b, t // blk, blk, g, d), (0, 3, 1, 4, 2)).astype(BF16)


def _from_col_blocks(o):
    b, g, nq, d, blk = o.shape
    return jnp.transpose(o, (0, 2, 4, 1, 3)).reshape(b, nq * blk, g * d)


def _pad_rows(x, n, front=0):
    back = n - x.shape[1] - front
    return jnp.pad(x, ((0, 0), (front, back)) + ((0, 0),) * (x.ndim - 2))


def _split(h, sizes):
    out, o = [], 0
    for s in sizes:
        out.append(h[:, o:o + s])
        o += s
    return out


def _split3(x):
    hi = x.astype(BF16)
    lo = (x - hi.astype(F32)).astype(BF16)
    return hi, lo


def _dsa_keys(dk_all, dv_all, dki_all):
    b, nkb = dk_all.shape[0], dk_all.shape[1] // DSA_TK
    kh, kl = _split3(dki_all)
    ki = jnp.concatenate([kh, kh, kl], axis=-1).reshape(b, nkb, DSA_TK, 3 * IDX_DIM)
    k = dk_all.astype(BF16).reshape(b, nkb, DSA_TK, D_DIM)
    vt = jnp.transpose(dv_all.astype(BF16).reshape(b, nkb, DSA_TK, D_DIM), (0, 1, 3, 2))
    norm = jnp.zeros((b, nkb, NORM_ROWS, DSA_TK), BF16).at[:, :, 0].set(1.0)
    return ki, k, jnp.concatenate([vt, norm], axis=2)


def _dsa_call(dq, dqi, dw, dk_all, dv_all, dki_all, tiles, i_off, n_sel):
    b, tq_all = dq.shape[:2]
    nq = tq_all // DSA_TQ

    def stack_t(x):
        d = x.shape[-1]
        return jnp.transpose(x.reshape(b, nq, DSA_TQ, D_HEADS, d), (0, 1, 4, 3, 2)).reshape(b, nq, d, D_HEADS * DSA_TQ)

    qh, ql = _split3(dqi)
    qit = stack_t(jnp.concatenate([qh, ql, qh], axis=-1))
    w = stack_t(dw[..., None])
    qt = stack_t((dq * (D_DIM ** -0.5 * LOG2E)).astype(BF16))
    ki, k, vt = _dsa_keys(dk_all, dv_all, dki_all)
    o = dsa(qit, w, ki, qt, k, vt, tiles, i_off=i_off, n_sel=n_sel)
    o = jnp.transpose(o.reshape(b, nq, D_DIM, D_HEADS, DSA_TQ), (0, 1, 4, 3, 2))
    return o.reshape(b, tq_all, D_HEADS * D_DIM)


P_BLK = 512
S_BLK = 128


def kernel(x_prompt, x_sample, cache_a_k, cache_a_v, cache_b_ckv, cache_b_krope, cache_c_k, cache_c_v, cache_d_k, cache_d_v, cache_d_kidx, t5_table, norm_attn, norm_ffn, final_norm, even_w_in, even_w_out, a_relbias, b_q_norm, b_kv_norm, b_w_uq, b_w_ukv, odd_w_in, odd_w_out, c_lambda_q1, c_lambda_k1, c_lambda_q2, c_lambda_k2, c_subln, ffn_w_gate, ffn_w_up, ffn_w_down):
    pb, pt, d = x_prompt.shape
    sb, st, _ = x_sample.shape
    assert pb == 1 and st == CHUNK
    past = cache_b_ckv.shape[2]
    n_a = cache_a_k.shape[2]
    depth = norm_attn.shape[0]
    n_p, n_s = pb * pt, sb * st
    s_len = past + st
    s_pad = -(-s_len // DSA_TK) * DSA_TK
    s_i = past // S_BLK
    s_i_dsa = past // DSA_TQ
    assert past % S_BLK == 0 and s_pad % S_BLK == 0 and past >= n_a
    p_blk = min(P_BLK, pt)

    cos_p, sin_p = _rope_tables(jnp.arange(pt))
    cos_s, sin_s = _rope_tables(jnp.tile(past + jnp.arange(st), sb))

    def rows_p(a):
        return a.reshape(pb, pt, *a.shape[1:])

    def rows_s(a):
        return a.reshape(sb, st, *a.shape[1:])

    def pad_cols(w):
        n = -(-w.shape[1] // LANES) * LANES
        return jnp.pad(w, ((0, 0), (0, n - w.shape[1]))).astype(BF16)

    def sample_q(a):
        return _pad_rows(a, S_BLK)

    def sample_keys(cache, new, front=0):
        return _pad_rows(jnp.concatenate([cache, new], axis=1), s_pad, front)

    def attend(q, k, v, tiles, blk, n_tiles, has_far, i_off=0):
        o = flash(_col_blocks(q, blk), _row_blocks(k, blk), _col_blocks(v, blk), tiles,
                  n_tiles=n_tiles, has_far=has_far, i_off=i_off, heads=q.shape[2])
        return _from_col_blocks(o)

    t5_c, t5_d = t5_table[:, :C_HEADS], t5_table[:, C_HEADS:]
    ctiles_p = causal_tiles(p_blk)
    ctiles_s = causal_tiles(S_BLK)
    c_tiles_p = t5_tiles(t5_c, p_blk, p_blk, 2)
    c_tiles_s = t5_tiles(t5_c, S_BLK, S_BLK, 2)
    d_tiles = t5_tiles(t5_d, DSA_TQ, DSA_TK, 3)
    d_tiles = jnp.transpose(d_tiles, (1, 2, 0, 3)).reshape(4, DSA_TK, D_HEADS * DSA_TQ)
    n_sel_p = min(TOPK_MAX, pt // 4)
    n_sel_s = min(TOPK_MAX, s_len // 4)

    outs_even = {k: [] for k in ("a_k_p", "a_k_s", "a_v_p", "a_v_s", "ckv_p", "ckv_s", "kr_p", "kr_s")}
    outs_odd = {k: [] for k in ("c_k_p", "c_k_s", "c_v_p", "c_v_s", "d_k_p", "d_k_s", "d_v_p", "d_v_s", "d_ki_p", "d_ki_s")}

    def mla_keys(kn, krope):
        return jnp.concatenate([kn, jnp.broadcast_to(krope[:, :, None, :], kn.shape[:3] + (B_ROPE,))], axis=-1)

    def even_prompt(x, l, i):
        a_tiles, nt = band_tiles(a_relbias[i], p_blk)
        akv, ckv, y1, y2, qta, ka, vta, qtb, kb, vtb = proj_even(
            x, norm_attn[l], even_w_in[i], b_q_norm[i], b_w_uq[i], b_kv_norm[i], b_w_ukv[i], cos_p, sin_p, p_blk)
        a_t = flash(qta, ka, vta, a_tiles, n_tiles=nt, has_far=False, out_dtype=BF16)
        b_t = flash(qtb, kb, vtb, ctiles_p, n_tiles=1, has_far=True, out_dtype=BF16)
        half = B_ROPE // 2
        hd = A_HEADS * A_DIM
        outs_even["a_k_p"].append(akv[pt - n_a:, :hd].reshape(pb, n_a, A_HEADS, A_DIM))
        outs_even["a_v_p"].append(akv[pt - n_a:, hd:].reshape(pb, n_a, A_HEADS, A_DIM))
        outs_even["ckv_p"].append(rows_p(ckv))
        outs_even["kr_p"].append(rows_p(jnp.concatenate([y1[:, :half], y2[:, :half]], axis=-1)))
        return out_even(x, a_t, b_t, even_w_out[i].astype(BF16))

    def even_sample(x, l, i):
        h = dense(x, pad_cols(even_w_in[i]), g=norm_attn[l])
        aq, ak, av, cq, ckv_raw, kr_raw = _split(h, EVEN_SPLITS)
        qb = dense(cq, b_w_uq[i].astype(BF16), g=b_q_norm[i]).reshape(-1, B_HEADS, B_NOPE + B_ROPE)
        kv_new, ckv = dense(ckv_raw, b_w_ukv[i].astype(BF16), g=b_kv_norm[i], emit_h=True)
        kr = _rope(kr_raw, cos_s, sin_s)
        q_mla = jnp.concatenate([qb[..., :B_NOPE], _rope(qb[..., B_NOPE:], cos_s, sin_s)], axis=-1)
        q_mla = q_mla * ((B_NOPE + B_ROPE) ** -0.5 * LOG2E)
        kv_s = rows_s(kv_new.reshape(-1, B_HEADS, B_NOPE + B_V))
        aq = (aq * (A_DIM ** -0.5 * LOG2E)).reshape(-1, A_HEADS, A_DIM)
        ak = ak.reshape(-1, A_HEADS, A_DIM)
        av = av.reshape(-1, A_HEADS, A_DIM)
        a_tiles, nt = band_tiles(a_relbias[i], S_BLK)
        ak_all = jnp.concatenate([cache_a_k[i], rows_s(ak)], axis=1)
        av_all = jnp.concatenate([cache_a_v[i], rows_s(av)], axis=1)
        a_out = attend(sample_q(rows_s(aq)), _pad_rows(ak_all, s_pad, past - n_a),
                       _pad_rows(av_all, s_pad, past - n_a), a_tiles, S_BLK, nt, False, s_i)[:, :st]
        kv_c = dense(cache_b_ckv[i].reshape(sb * past, KV_LORA), b_w_ukv[i].astype(BF16))
        kv_c = kv_c.reshape(sb, past, B_HEADS, B_NOPE + B_V)
        kn_all = sample_keys(kv_c[..., :B_NOPE], kv_s[..., :B_NOPE])
        v_all = sample_keys(kv_c[..., B_NOPE:], kv_s[..., B_NOPE:])
        kr_all = sample_keys(cache_b_krope[i], rows_s(kr))
        b_out = attend(sample_q(rows_s(q_mla)), mla_keys(kn_all, kr_all), v_all, ctiles_s, S_BLK, 1, True, s_i)[:, :st]
        mix = jnp.concatenate([a_out.reshape(n_s, -1), b_out.reshape(n_s, -1)], axis=-1)
        outs_even["a_k_s"].append(ak_all[:, -n_a:])
        outs_even["a_v_s"].append(av_all[:, -n_a:])
        outs_even["ckv_s"].append(rows_s(ckv))
        outs_even["kr_s"].append(rows_s(kr))
        return dense(mix.astype(BF16), even_w_out[i].astype(BF16), res=x)

    def diff_lambda(l, i):
        lam_init = 0.8 - 0.6 * math.exp(-0.3 * l)
        lam = (jnp.exp(jnp.sum(c_lambda_q1[i].astype(F32) * c_lambda_k1[i].astype(F32)))
               - jnp.exp(jnp.sum(c_lambda_q2[i].astype(F32) * c_lambda_k2[i].astype(F32))) + lam_init)
        return lam_init, lam

    def odd_prompt(x, l, i):
        lam_init, lam = diff_lambda(l, i)
        st_, cqt, kc, cvt, qit, wq, qt = proj_odd(x, norm_attn[l], odd_w_in[i], p_blk)
        wc = C_HEADS * 2 * C_DIM
        ck, cv = st_[:, :wc], st_[:, wc:2 * wc]
        dk, dv, dki = (st_[:, 2 * wc + n * LANES:2 * wc + n * LANES + D_DIM] for n in range(3))
        c_t = flash(cqt, kc, cvt, c_tiles_p, n_tiles=2, has_far=True)
        ki, k, vt = _dsa_keys(dk[None], dv[None], dki[None])
        d_t = dsa(qit, wq, ki, qt, k, vt, d_tiles, i_off=0, n_sel=n_sel_p, out_dtype=BF16)
        for name, arr, shp in (("c_k", ck, (C_HEADS, 2 * C_DIM)), ("c_v", cv, (C_HEADS, 2 * C_DIM)),
                               ("d_k", dk, (D_DIM,)), ("d_v", dv, (D_DIM,)), ("d_ki", dki, (IDX_DIM,))):
            outs_odd[name + "_p"].append(arr.reshape(pb, pt, *shp))
        return out_odd(x, c_t, d_t, lam, c_subln[i], 1.0 - lam_init, odd_w_out[i].astype(BF16))

    def odd_sample(x, l, i):
        lam_init, lam = diff_lambda(l, i)
        h = dense(x, pad_cols(odd_w_in[i]), g=norm_attn[l])
        cq, ck, cv, dq, dk, dv, dqi, dki, dw = _split(h, ODD_SPLITS)
        cq = (cq * (C_DIM ** -0.5 * LOG2E)).reshape(-1, 2 * C_HEADS, C_DIM)
        ck3 = ck.reshape(-1, 2 * C_HEADS, C_DIM)
        cv3 = cv.reshape(-1, C_HEADS, 2 * C_DIM)
        dq = dq.reshape(-1, D_HEADS, D_DIM)
        dqi = dqi.reshape(-1, IDX_HEADS, IDX_DIM)
        dw = dw * ((IDX_HEADS ** -0.5) * (IDX_DIM ** -0.5))

        def diff_combine(o):
            b_, t_, _ = o.shape
            o = o.reshape(b_, t_, C_HEADS, 2, 2 * C_DIM)
            c = o[:, :, :, 0] - lam * o[:, :, :, 1]
            return (_rms(c, c_subln[i].astype(F32)) * (1.0 - lam_init)).reshape(b_, t_, -1)

        rows, tag, nb_, nt_ = rows_s, "_s", sb, st
        ck_all = sample_keys(cache_c_k[i].reshape(sb, past, 2 * C_HEADS, C_DIM), rows(ck3))
        cv_all = sample_keys(cache_c_v[i], rows(cv3))
        c_out = attend(sample_q(rows(cq)), ck_all, cv_all, c_tiles_s, S_BLK, 2, True, s_i)[:, :st]
        d_out = _dsa_call(_pad_rows(rows(dq), DSA_TQ), _pad_rows(rows(dqi), DSA_TQ), _pad_rows(rows(dw), DSA_TQ),
                          sample_keys(cache_d_k[i], rows(dk)), sample_keys(cache_d_v[i], rows(dv)),
                          sample_keys(cache_d_kidx[i], rows(dki)), d_tiles, s_i_dsa, n_sel_s)[:, :st]
        mix = jnp.concatenate([diff_combine(c_out).reshape(nb_ * nt_, -1), d_out.reshape(nb_ * nt_, -1)], axis=-1)
        for name, arr, shp in (("c_k", ck, (C_HEADS, 2 * C_DIM)), ("c_v", cv, (C_HEADS, 2 * C_DIM)),
                               ("d_k", dk, (D_DIM,)), ("d_v", dv, (D_DIM,)), ("d_ki", dki, (IDX_DIM,))):
            outs_odd[name + tag].append(rows(arr).reshape(nb_, nt_, *shp))
        return dense(mix.astype(BF16), odd_w_out[i].astype(BF16), res=x)

    xp, xs = x_prompt.reshape(n_p, d), x_sample.reshape(n_s, d)
    for l in range(depth):
        i = l // 2
        if l % 2 == 0:
            xp, xs = even_prompt(xp, l, i), even_sample(xs, l, i)
        else:
            xp, xs = odd_prompt(xp, l, i), odd_sample(xs, l, i)
        wg, wu, wd = ffn_w_gate[l].astype(BF16), ffn_w_up[l].astype(BF16), ffn_w_down[l].astype(BF16)
        fg = final_norm if l == depth - 1 else None
        xp, xs = ffn(xp, norm_ffn[l], wg, wu, wd, final_g=fg), ffn(xs, norm_ffn[l], wg, wu, wd, final_g=fg)

    se = {k: jnp.stack(v, axis=0) for k, v in outs_even.items()}
    so = {k: jnp.stack(v, axis=0) for k, v in outs_odd.items()}
    return (xp.reshape(pb, pt, d), xs.reshape(sb, st, d), se["a_k_p"], se["a_k_s"], se["a_v_p"], se["a_v_s"],
            se["ckv_p"], se["ckv_s"], se["kr_p"], se["kr_s"],
            so["c_k_p"], so["c_k_s"], so["c_v_p"], so["c_v_s"],
            so["d_k_p"], so["d_k_s"], so["d_v_p"], so["d_v_s"], so["d_ki_p"], so["d_ki_s"])
```

```python
import functools
import math

import numpy as np
import jax
import jax.numpy as jnp
from jax import lax
from jax.experimental import pallas as pl
from jax.experimental.pallas import tpu as pltpu

F32, BF16, I32 = jnp.float32, jnp.bfloat16, jnp.int32

D_MODEL = 1024
CHUNK = 64
EPS = 1e-6
A_HEADS, A_DIM, A_LEFT_CHUNKS, A_MAX_REL = 8, 64, 8, 128
B_HEADS, B_NOPE, B_ROPE, B_V = 8, 64, 32, 64
Q_LORA, KV_LORA = 256, 128
ROPE_THETA = 10000.0
C_HEADS, C_DIM = 4, 64
D_HEADS, D_DIM = 8, 64
IDX_HEADS, IDX_DIM = 8, 64
TOPK_MAX = 256
T5_BUCKETS, T5_MAX_DIST = 32, 128
EVEN_SPLITS = (512, 512, 512, Q_LORA, KV_LORA, B_ROPE)
ODD_SPLITS = (512, 512, 512, 512, D_DIM, D_DIM, 512, IDX_DIM, IDX_HEADS)

LANES = 128
SUBLANES = 8
VMEM_LIMIT = 56 * 1024 * 1024
MASKED = -1e30
UNSELECTED = -2e30
M_INIT = -1e30
SUM_MAX, SUM_MIN = 1e18, 1e-18
LOG2E = math.log2(math.e)
INT_MIN = -(2 ** 31)
INT_MAX = 2 ** 31 - 1

DSA_TQ, DSA_TK = 128, 256
COUNT_ROWS = 8 * SUBLANES
FAR_UNROLL = 4
DSA_FAR_UNROLL = 4
NORM_ROWS = 16
DSA_SCORE_UNROLL = 4


def _cparams(n_axes):
    return pltpu.CompilerParams(dimension_semantics=("arbitrary",) * n_axes, vmem_limit_bytes=VMEM_LIMIT)


def _rms(x, g):
    return x * lax.rsqrt(jnp.mean(x * x, axis=-1, keepdims=True) + EPS) * g


def _dense_kernel(*refs, norm, res, emit_h):
    it = iter(refs)
    x_ref = next(it)
    g_ref = next(it) if norm else None
    w_ref = next(it)
    r_ref = next(it) if res else None
    o_ref = next(it)
    h_ref = next(it) if emit_h else None
    x = x_ref[...]
    if norm:
        x = _rms(x.astype(F32), g_ref[...])
        if emit_h:
            h_ref[...] = x
    acc = jnp.dot(x.astype(BF16), w_ref[...], preferred_element_type=F32)
    if res:
        acc = acc + r_ref[...]
    o_ref[...] = acc


def dense(x, w, g=None, res=None, emit_h=False, tm=512):
    m, k = x.shape
    n = w.shape[1]
    assert m % tm == 0 and n % LANES == 0
    norm = g is not None
    args, specs = [x], [pl.BlockSpec((tm, k), lambda i: (i, 0))]
    if norm:
        args.append(g.reshape(1, k).astype(F32))
        specs.append(pl.BlockSpec((1, k), lambda i: (0, 0)))
    args.append(w)
    specs.append(pl.BlockSpec((k, n), lambda i: (0, 0)))
    if res is not None:
        args.append(res)
        specs.append(pl.BlockSpec((tm, n), lambda i: (i, 0)))
    out_shape = [jax.ShapeDtypeStruct((m, n), F32)]
    out_specs = [pl.BlockSpec((tm, n), lambda i: (i, 0))]
    if emit_h:
        out_shape.append(jax.ShapeDtypeStruct((m, k), F32))
        out_specs.append(pl.BlockSpec((tm, k), lambda i: (i, 0)))
    outs = pl.pallas_call(
        functools.partial(_dense_kernel, norm=norm, res=res is not None, emit_h=emit_h),
        grid=(m // tm,), in_specs=specs, out_specs=out_specs, out_shape=out_shape,
        compiler_params=_cparams(1), name="dense")(*args)
    return outs if emit_h else outs[0]


def _ffn_kernel(x_ref, g_ref, wg_ref, wu_ref, wd_ref, fg_ref, o_ref, h_sc, acc_sc, *, final):
    f = pl.program_id(1)

    @pl.when(f == 0)
    def _():
        x = x_ref[...]
        h_sc[...] = _rms(x, g_ref[...]).astype(BF16)
        acc_sc[...] = x

    h = h_sc[...]
    gate = jnp.dot(h, wg_ref[...], preferred_element_type=F32)
    up = jnp.dot(h, wu_ref[...], preferred_element_type=F32)
    a = (gate * jax.nn.sigmoid(gate) * up).astype(BF16)
    acc_sc[...] += jnp.dot(a, wd_ref[...], preferred_element_type=F32)

    @pl.when(f == pl.num_programs(1) - 1)
    def _():
        y = acc_sc[...]
        if final:
            y = _rms(y, fg_ref[...])
        o_ref[...] = y


def ffn(x, g, wg, wu, wd, final_g=None, tm=512, nf=2):
    m, d = x.shape
    hid = wg.shape[1]
    tf = hid // nf
    assert m % tm == 0 and hid % nf == 0 and tf % LANES == 0
    final = final_g is not None
    fg = (final_g if final else g).reshape(1, d).astype(F32)
    return pl.pallas_call(
        functools.partial(_ffn_kernel, final=final),
        grid=(m // tm, nf),
        in_specs=[pl.BlockSpec((tm, d), lambda i, f: (i, 0)),
                  pl.BlockSpec((1, d), lambda i, f: (0, 0)),
                  pl.BlockSpec((d, tf), lambda i, f: (0, f)),
                  pl.BlockSpec((d, tf), lambda i, f: (0, f)),
                  pl.BlockSpec((tf, d), lambda i, f: (f, 0)),
                  pl.BlockSpec((1, d), lambda i, f: (0, 0))],
        out_specs=pl.BlockSpec((tm, d), lambda i, f: (i, 0)),
        out_shape=jax.ShapeDtypeStruct((m, d), F32),
        scratch_shapes=[pltpu.VMEM((tm, d), BF16), pltpu.VMEM((tm, d), F32)],
        compiler_params=_cparams(2), name="ffn")(x, g.reshape(1, d).astype(F32), wg, wu, wd, fg)


def _flash_kernel(qt_ref, k_ref, vt_ref, tile_ref, o_ref, m_sc, l_sc, acc_sc, *, n_tiles, has_far, i_off,
                  heads, kdiv, vdiv, bdiv):
    i = pl.program_id(2) + i_off
    j_lo = 0 if has_far else jnp.maximum(i - (n_tiles - 1), 0)

    def one_head(g):
        gk, gv, gt = g // kdiv, g // vdiv, g // bdiv
        qt = qt_ref[0, g, 0]

        def raw(j):
            return jnp.dot(k_ref[0, gk, j], qt, preferred_element_type=F32)

        def scores(j):
            return raw(j) + tile_ref[gt, jnp.minimum(i - j, n_tiles)]

        def reset():
            l_sc[...] = jnp.zeros(l_sc.shape, F32)
            acc_sc[...] = jnp.zeros(acc_sc.shape, F32)

        def accumulate(j, p):
            l_sc[...] += jnp.sum(p, axis=0, keepdims=True)
            acc_sc[...] += jnp.dot(vt_ref[0, gv, j], p.astype(BF16), preferred_element_type=F32)

        reset()
        s_diag = raw(i) + tile_ref[gt, 0]
        m = jnp.max(s_diag, axis=0, keepdims=True)
        accumulate(i, jnp.exp2(s_diag - m))
        if has_far:
            m_far = m - tile_ref[gt, n_tiles, 0:1, :]
            n_far = jnp.maximum(i - (n_tiles - 1), 0)

            def far_group(jj, carry):
                ps = [jnp.exp2(raw(FAR_UNROLL * jj + u) - m_far) for u in range(FAR_UNROLL)]
                for u in range(FAR_UNROLL):
                    accumulate(FAR_UNROLL * jj + u, ps[u])
                return carry

            def far_single(j, carry):
                accumulate(j, jnp.exp2(raw(j) - m_far))
                return carry

            n_grouped = (n_far // FAR_UNROLL) * FAR_UNROLL
            lax.fori_loop(0, n_far // FAR_UNROLL, far_group, 0)
            lax.fori_loop(n_grouped, n_far, far_single, 0)
        for t in range(1, n_tiles):
            @pl.when(i - t >= 0)
            def _(t=t):
                accumulate(i - t, jnp.exp2(raw(i - t) + tile_ref[gt, t] - m))

        l = l_sc[...]
        in_range = (jnp.max(l) < SUM_MAX) & (jnp.min(l) > SUM_MIN)

        @pl.when(jnp.logical_not(in_range))
        def _():
            reset()
            m_sc[...] = jnp.full(m_sc.shape, M_INIT, F32)

            def body(j, carry):
                s = scores(j)
                m_prev = m_sc[...]
                m_new = jnp.maximum(m_prev, jnp.max(s, axis=0, keepdims=True))
                alpha = jnp.exp2(m_prev - m_new)
                p = jnp.exp2(s - m_new)
                l_sc[...] = alpha * l_sc[...] + jnp.sum(p, axis=0, keepdims=True)
                acc_sc[...] = alpha * acc_sc[...] + jnp.dot(vt_ref[0, gv, j], p.astype(BF16),
                                                            preferred_element_type=F32)
                m_sc[...] = m_new
                return carry

            lax.fori_loop(j_lo, i + 1, body, 0)

        o_ref[0, g, 0] = (acc_sc[...] / l_sc[...]).astype(o_ref.dtype)

    if heads == 1:
        one_head(0)
    else:
        def head_step(g, carry):
            one_head(g)
            return carry
        lax.fori_loop(0, heads, head_step, 0)


def flash(qt, k, vt, tiles, *, n_tiles, has_far, i_off=0, out_dtype=F32, heads=1):
    b, g, nq, dq, blk = qt.shape
    gk, nb = k.shape[1], k.shape[2]
    gv, dv = vt.shape[1], vt.shape[3]
    gb = tiles.shape[0]
    assert tiles.shape[1] == n_tiles + 1
    kdiv, vdiv, bdiv = g // gk, g // gv, g // gb
    assert g % heads == 0

    def shared(div):
        if heads % div == 0:
            return heads // div, 1
        assert div % heads == 0
        return 1, div // heads

    (hk, kd), (hv, vd), (hb, bd) = shared(kdiv), shared(vdiv), shared(bdiv)
    return pl.pallas_call(
        functools.partial(_flash_kernel, n_tiles=n_tiles, has_far=has_far, i_off=i_off,
                          heads=heads, kdiv=kdiv, vdiv=vdiv, bdiv=bdiv),
        grid=(b, g // heads, nq),
        in_specs=[pl.BlockSpec((1, heads, 1, dq, blk), lambda bi, gi, i: (bi, gi, i, 0, 0)),
                  pl.BlockSpec((1, hk, nb, blk, dq), lambda bi, gi, i: (bi, gi // kd, 0, 0, 0)),
                  pl.BlockSpec((1, hv, nb, dv, blk), lambda bi, gi, i: (bi, gi // vd, 0, 0, 0)),
                  pl.BlockSpec((hb, n_tiles + 1, blk, blk), lambda bi, gi, i: (gi // bd, 0, 0, 0))],
        out_specs=pl.BlockSpec((1, heads, 1, dv, blk), lambda bi, gi, i: (bi, gi, i, 0, 0)),
        out_shape=jax.ShapeDtypeStruct((b, g, nq, dv, blk), out_dtype),
        scratch_shapes=[pltpu.VMEM((1, blk), F32), pltpu.VMEM((1, blk), F32), pltpu.VMEM((dv, blk), F32)],
        compiler_params=_cparams(3), name="flash")(qt, k, vt, tiles)


def _dsa_kernel(qit_ref, w_ref, ki_ref, qt_ref, k_ref, vt_ref, tile_ref, o_ref,
                key_sc, m_sc, acc_sc, *, i_off, n_sel, idx_bits):
    tq, tk, nh = DSA_TQ, DSA_TK, D_HEADS
    i = pl.program_id(1) + i_off
    nkb = (i * tq) // tk + 1
    krow = lax.broadcasted_iota(I32, (tk, tq), 0)
    qcol = lax.broadcasted_iota(I32, (tk, tq), 1)
    q_chunk = jnp.right_shift(i * tq + qcol, 6)

    def head(x, h):
        return x[:, h * tq:(h + 1) * tq]

    qit = qit_ref[0, 0]
    w = w_ref[0, 0]

    def score_block(j, mask_future):
        lg = jnp.maximum(jnp.dot(ki_ref[0, j], qit, preferred_element_type=F32), 0.0) * w
        sc = head(lg, 0)
        for h in range(1, IDX_HEADS):
            sc = sc + head(lg, h)
        sc = jnp.where(sc == 0.0, 0.0, sc)
        bits = pltpu.bitcast(sc, I32)
        key = bits ^ (jnp.right_shift(bits, 31) & INT_MAX)
        if mask_future:
            key = jnp.where(jnp.right_shift(j * tk + krow, 6) <= q_chunk, key, INT_MIN)
        key_sc[j] = key

    def score_group(jj, carry):
        for u in range(DSA_SCORE_UNROLL):
            score_block(DSA_SCORE_UNROLL * jj + u, False)
        return carry

    def score_single(j, carry):
        score_block(j, False)
        return carry

    n_full = nkb - 1
    lax.fori_loop(0, n_full // DSA_SCORE_UNROLL, score_group, 0)
    lax.fori_loop((n_full // DSA_SCORE_UNROLL) * DSA_SCORE_UNROLL, n_full, score_single, 0)
    score_block(n_full, True)

    def count(pred_fn):
        def body(j, acc):
            ind = jnp.where(pred_fn(key_sc[j], j * tk + krow), 1.0, 0.0)
            return acc + jnp.sum(ind.reshape(tk // COUNT_ROWS, COUNT_ROWS, tq), axis=0)
        acc = lax.fori_loop(0, nkb, body, jnp.zeros((COUNT_ROWS, tq), F32))
        return jnp.sum(acc, axis=0, keepdims=True)

    target = float(n_sel)
    qpos = i * tq + lax.broadcasted_iota(I32, (1, tq), 1)
    n_adm = ((jnp.right_shift(qpos, 6) + 1) * CHUNK).astype(F32)
    real = n_adm > target

    def unsettled(c_thr):
        settled = (c_thr == target) | jnp.logical_not(real)
        return jnp.min(jnp.where(settled, 1.0, 0.0)) < 1.0

    def value_bit(state):
        thr, c_thr, b = state
        cand = thr + jnp.left_shift(jnp.int32(1), 31 - b)
        cnt = count(lambda k, idx: k >= cand)
        ge = cnt >= target
        return jnp.where(ge, cand, thr), jnp.where(ge, cnt, c_thr), b + 1

    thr, c_thr, _ = lax.while_loop(lambda s: (s[2] < 32) & unsettled(s[1]), value_bit,
                                   (jnp.full((1, tq), INT_MIN, I32), n_adm, jnp.int32(0)))

    def tie_search():
        need = target - count(lambda k, idx: k > thr)

        def index_bit(b, y):
            cand = y + jnp.left_shift(jnp.int32(1), idx_bits - 1 - b)
            cnt = count(lambda k, idx: (k == thr) & (idx < cand))
            return jnp.where(cnt < need, cand, y)
        return lax.fori_loop(0, idx_bits, index_bit, jnp.zeros((1, tq), I32))

    has_ties = jnp.max(jnp.where((c_thr > target) & real, 1.0, 0.0)) > 0.0
    last = lax.cond(has_ties, tie_search, lambda: jnp.full((1, tq), INT_MAX, I32))
    last = jnp.where(real, last, -1)

    qt = qt_ref[0, 0]
    n_near = 3

    def raw(j):
        return jnp.dot(k_ref[0, j], qt, preferred_element_type=F32)

    def tile_of(j):
        return jnp.minimum((i * tq - j * tk) // tq, n_near)

    def select(j, s):
        key = key_sc[j]
        sel = (key > thr) | ((key == thr) & (j * tk + krow <= last))
        selb = jnp.where(sel, 0.0, UNSELECTED)
        return jnp.concatenate([head(s, h) + selb for h in range(nh)], axis=1)

    dv = o_ref.shape[2]
    def reset():
        acc_sc[...] = jnp.zeros(acc_sc.shape, F32)

    def accumulate(j, p):
        acc_sc[...] += jnp.dot(vt_ref[0, j], p.astype(BF16), preferred_element_type=F32)

    reset()
    s_last = raw(nkb - 1) + tile_ref[tile_of(nkb - 1)]
    m = jnp.max(s_last, axis=0, keepdims=True)
    accumulate(nkb - 1, jnp.exp2(select(nkb - 1, s_last) - m))
    m_far = m - tile_ref[n_near, 0:1, :]
    n_far = jnp.maximum(i - 1, 0) // 2

    def far_group(jj, carry):
        ps = [jnp.exp2(select(DSA_FAR_UNROLL * jj + u, raw(DSA_FAR_UNROLL * jj + u)) - m_far)
              for u in range(DSA_FAR_UNROLL)]
        for u in range(DSA_FAR_UNROLL):
            accumulate(DSA_FAR_UNROLL * jj + u, ps[u])
        return carry

    def far_single(j, carry):
        accumulate(j, jnp.exp2(select(j, raw(j)) - m_far))
        return carry

    def near(j, carry):
        accumulate(j, jnp.exp2(select(j, raw(j) + tile_ref[tile_of(j)]) - m))
        return carry

    n_grouped = (n_far // DSA_FAR_UNROLL) * DSA_FAR_UNROLL
    lax.fori_loop(0, n_far // DSA_FAR_UNROLL, far_group, 0)
    lax.fori_loop(n_grouped, n_far, far_single, 0)
    lax.fori_loop(n_far, nkb - 1, near, 0)

    l = acc_sc[dv:dv + 1, :]
    in_range = (jnp.max(l) < SUM_MAX) & (jnp.min(l) > SUM_MIN)

    @pl.when(jnp.logical_not(in_range))
    def _():
        reset()
        m_sc[...] = jnp.full(m_sc.shape, M_INIT, F32)

        def attend(j, carry):
            s = select(j, raw(j) + tile_ref[tile_of(j)])
            m_prev = m_sc[...]
            m_new = jnp.maximum(m_prev, jnp.max(s, axis=0, keepdims=True))
            alpha = jnp.exp2(m_prev - m_new)
            p = jnp.exp2(s - m_new)
            acc_sc[...] = alpha * acc_sc[...] + jnp.dot(vt_ref[0, j], p.astype(BF16), preferred_element_type=F32)
            m_sc[...] = m_new
            return carry

        lax.fori_loop(0, nkb, attend, 0)

    o_ref[0, 0] = (acc_sc[0:dv, :] / acc_sc[dv:dv + 1, :]).astype(o_ref.dtype)


def dsa(qit, w, ki, qt, k, vt, tiles, *, i_off, n_sel, out_dtype=F32):
    b, nq, kdim, cols = qit.shape
    nkb = ki.shape[1]
    dk, dva = k.shape[-1], vt.shape[2]
    dv = dva - NORM_ROWS
    idx_bits = max(1, (nkb * DSA_TK - 1).bit_length())
    return pl.pallas_call(
        functools.partial(_dsa_kernel, i_off=i_off, n_sel=n_sel, idx_bits=idx_bits),
        grid=(b, nq),
        in_specs=[pl.BlockSpec((1, 1, kdim, cols), lambda bi, i: (bi, i, 0, 0)),
                  pl.BlockSpec((1, 1, 1, cols), lambda bi, i: (bi, i, 0, 0)),
                  pl.BlockSpec((1, nkb, DSA_TK, kdim), lambda bi, i: (bi, 0, 0, 0)),
                  pl.BlockSpec((1, 1, dk, cols), lambda bi, i: (bi, i, 0, 0)),
                  pl.BlockSpec((1, nkb, DSA_TK, dk), lambda bi, i: (bi, 0, 0, 0)),
                  pl.BlockSpec((1, nkb, dva, DSA_TK), lambda bi, i: (bi, 0, 0, 0)),
                  pl.BlockSpec((4, DSA_TK, cols), lambda bi, i: (0, 0, 0))],
        out_specs=pl.BlockSpec((1, 1, dv, cols), lambda bi, i: (bi, i, 0, 0)),
        out_shape=jax.ShapeDtypeStruct((b, nq, dv, cols), out_dtype),
        scratch_shapes=[pltpu.VMEM((nkb, DSA_TK, DSA_TQ), I32), pltpu.VMEM((1, cols), F32),
                        pltpu.VMEM((dva, cols), F32)],
        compiler_params=_cparams(2), name="dsa")(qit, w, ki, qt, k, vt, tiles)


_NT = (((1,), (1,)), ((), ()))


def _heads(x, n, d):
    return x.reshape(n, d, x.shape[-1])


def _proj_even_kernel(x_ref, g_ref, w1_ref, w2t_ref, w3_ref, qn_ref, wuqt_ref, kvn_ref, wkb_ref, e1_ref, e2_ref,
                      wuvt_ref, cosn_ref, sinn_ref, cost_ref, sint_ref,
                      akv_ref, ckv_ref, y1_ref, y2_ref, qta_ref, ka_ref, vta_ref, qtb_ref, kb_ref, vtb_ref):
    nh, da, rope_half = A_HEADS, A_DIM, B_ROPE // 2
    h = _rms(x_ref[...], g_ref[...]).astype(BF16)
    h1 = jnp.dot(h, w1_ref[...], preferred_element_type=F32)
    akv_ref[...] = h1[:, :2 * nh * da]
    t2 = lax.dot_general(w2t_ref[...], h, _NT, preferred_element_type=F32)
    qta_ref[0, :, 0] = _heads(t2[:nh * da], nh, da).astype(BF16)
    vta_ref[0, :, 0] = _heads(t2[nh * da:], nh, da).astype(BF16)
    k3 = jnp.dot(h, w3_ref[...], preferred_element_type=F32)
    for hd in range(nh):
        ka_ref[0, hd, 0] = k3[:, hd * LANES:hd * LANES + da].astype(BF16)

    c0 = 2 * nh * da
    cqn = _rms(h1[:, c0:c0 + Q_LORA], qn_ref[...]).astype(BF16)
    tq = lax.dot_general(wuqt_ref[...], cqn, _NT, preferred_element_type=F32)
    n_nope, n_rope = B_HEADS * B_NOPE, B_HEADS * rope_half
    x1, x2 = tq[n_nope:n_nope + n_rope], tq[n_nope + n_rope:]
    ct, st = cost_ref[0], sint_ref[0]
    y1t, y2t = x1 * ct - x2 * st, x1 * st + x2 * ct
    for hd in range(B_HEADS):
        qtb_ref[0, hd, 0] = jnp.concatenate(
            [tq[hd * B_NOPE:(hd + 1) * B_NOPE], y1t[hd * rope_half:(hd + 1) * rope_half],
             y2t[hd * rope_half:(hd + 1) * rope_half]], axis=0).astype(BF16)

    c1 = c0 + Q_LORA
    ckv = _rms(h1[:, c1:c1 + KV_LORA], kvn_ref[...])
    ckv_ref[...] = ckv
    ckvb = ckv.astype(BF16)
    k1, k2 = h1[:, c1 + KV_LORA:c1 + KV_LORA + LANES], h1[:, c1 + KV_LORA + LANES:]
    cn, sn = cosn_ref[...], sinn_ref[...]
    y1, y2 = k1 * cn - k2 * sn, k1 * sn + k2 * cn
    y1_ref[...] = y1
    y2_ref[...] = y2
    kb = (jnp.dot(ckvb, wkb_ref[...], preferred_element_type=F32)
          + jnp.dot(y1.astype(BF16), e1_ref[...], preferred_element_type=F32)
          + jnp.dot(y2.astype(BF16), e2_ref[...], preferred_element_type=F32))
    for hd in range(B_HEADS):
        kb_ref[0, hd, 0] = kb[:, hd * LANES:hd * LANES + B_NOPE + B_ROPE].astype(BF16)
    tv = lax.dot_general(wuvt_ref[...], ckvb, _NT, preferred_element_type=F32)
    vtb_ref[0, :, 0] = _heads(tv, B_HEADS, B_V).astype(BF16)


def proj_even(x, g, w_in, q_norm, w_uq, kv_norm, w_ukv, cos, sin, blk):
    p, d = x.shape
    nq = p // blk
    nh, da, half = A_HEADS, A_DIM, B_ROPE // 2
    s_a = A_DIM ** -0.5 * LOG2E
    s_b = (B_NOPE + B_ROPE) ** -0.5 * LOG2E
    aq_w, ak_w, av_w, cq_w, ckv_w, kr_w = _split(w_in, EVEN_SPLITS)

    def lane_pad(w, n=LANES):
        return jnp.pad(w, ((0, 0), (0, n - w.shape[1])))

    def head_chunks(w, dh):
        k = w.shape[0]
        return jnp.pad(w.reshape(k, -1, dh), ((0, 0), (0, 0), (0, LANES - dh))).reshape(k, -1)

    w1 = jnp.concatenate([ak_w, av_w, cq_w, ckv_w, lane_pad(kr_w[:, :half]), lane_pad(kr_w[:, half:])], axis=1).astype(BF16)
    w2t = jnp.concatenate([aq_w * s_a, av_w], axis=1).T.astype(BF16)
    w3 = head_chunks(ak_w, da).astype(BF16)
    uq = (w_uq * s_b).reshape(Q_LORA, B_HEADS, B_NOPE + B_ROPE)
    wuqt = jnp.concatenate([uq[:, :, :B_NOPE].reshape(Q_LORA, -1), uq[:, :, B_NOPE:B_NOPE + half].reshape(Q_LORA, -1),
                            uq[:, :, B_NOPE + half:].reshape(Q_LORA, -1)], axis=1).T.astype(BF16)
    ukv = w_ukv.reshape(KV_LORA, B_HEADS, B_NOPE + B_V)
    wkb = head_chunks(ukv[:, :, :B_NOPE].reshape(KV_LORA, -1), B_NOPE).astype(BF16)
    wuvt = ukv[:, :, B_NOPE:].reshape(KV_LORA, -1).T.astype(BF16)
    eye = np.zeros((2, LANES, B_HEADS, LANES), np.float32)
    for r in range(half):
        eye[0, r, :, B_NOPE + r] = 1.0
        eye[1, r, :, B_NOPE + half + r] = 1.0
    e1, e2 = (jnp.asarray(e.reshape(LANES, B_HEADS * LANES), BF16) for e in eye)
    cosn, sinn = lane_pad(cos), lane_pad(sin)
    cost = jnp.tile(cos.reshape(nq, blk, half).transpose(0, 2, 1), (1, B_HEADS, 1))
    sint = jnp.tile(sin.reshape(nq, blk, half).transpose(0, 2, 1), (1, B_HEADS, 1))

    def full(a):
        return pl.BlockSpec(a.shape, lambda i: (0,) * a.ndim)

    def rows(n):
        return pl.BlockSpec((blk, n), lambda i: (i, 0))

    def tile3(a):
        return pl.BlockSpec((1,) + a.shape[1:], lambda i: (i, 0, 0))

    def per_block(g_, a, b_):
        return (jax.ShapeDtypeStruct((1, g_, nq, a, b_), BF16),
                pl.BlockSpec((1, g_, 1, a, b_), lambda i: (0, 0, i, 0, 0)))

    consts = [g.reshape(1, d).astype(F32), w1, w2t, w3, q_norm.reshape(1, -1).astype(F32), wuqt,
              kv_norm.reshape(1, -1).astype(F32), wkb, e1, e2, wuvt]
    blocks = [per_block(nh, da, blk), per_block(nh, blk, da), per_block(nh, da, blk),
              per_block(B_HEADS, B_NOPE + B_ROPE, blk), per_block(B_HEADS, blk, B_NOPE + B_ROPE),
              per_block(B_HEADS, B_V, blk)]
    out_shape = [jax.ShapeDtypeStruct((p, 2 * nh * da), F32), jax.ShapeDtypeStruct((p, KV_LORA), F32),
                 jax.ShapeDtypeStruct((p, LANES), F32), jax.ShapeDtypeStruct((p, LANES), F32)] + [s for s, _ in blocks]
    out_specs = [rows(2 * nh * da), rows(KV_LORA), rows(LANES), rows(LANES)] + [s for _, s in blocks]
    return pl.pallas_call(
        _proj_even_kernel, grid=(nq,),
        in_specs=[rows(d)] + [full(a) for a in consts] + [rows(LANES), rows(LANES), tile3(cost), tile3(sint)],
        out_specs=out_specs, out_shape=out_shape,
        compiler_params=_cparams(1), name="proj_even")(x, *consts, cosn, sinn, cost, sint)


def _proj_odd_kernel(x_ref, g_ref, w1_ref, w2t_ref, w3_ref,
                     st_ref, cqt_ref, kc_ref, cvt_ref, qit_ref, wq_ref, qt_ref):
    nmap, width = 2 * C_HEADS, C_HEADS * 2 * C_DIM
    h = _rms(x_ref[...], g_ref[...]).astype(BF16)
    st_ref[...] = jnp.dot(h, w1_ref[...], preferred_element_type=F32)
    t2 = lax.dot_general(w2t_ref[...], h, _NT, preferred_element_type=F32)
    cqt_ref[0, :, 0] = _heads(t2[:width], nmap, C_DIM).astype(BF16)
    cvt_ref[0, :, 0] = _heads(t2[width:2 * width], C_HEADS, 2 * C_DIM).astype(BF16)
    k3 = jnp.dot(h, w3_ref[...], preferred_element_type=F32)
    for m in range(nmap):
        kc_ref[0, m, 0] = k3[:, m * LANES:m * LANES + C_DIM].astype(BF16)

    dqt = t2[2 * width:2 * width + D_HEADS * D_DIM]
    dqit = t2[2 * width + D_HEADS * D_DIM:2 * width + D_HEADS * (D_DIM + IDX_DIM)]
    dwt = t2[2 * width + D_HEADS * (D_DIM + IDX_DIM):]

    def stack(x, rows_per_head, n_rows, b):
        cols = slice(b * DSA_TQ, (b + 1) * DSA_TQ)
        return jnp.concatenate([x[hd * rows_per_head:hd * rows_per_head + n_rows, cols] for hd in range(D_HEADS)], axis=1)

    for b in range(x_ref.shape[0] // DSA_TQ):
        qt_ref[0, b] = stack(dqt, D_DIM, D_DIM, b).astype(BF16)
        qi = stack(dqit, IDX_DIM, IDX_DIM, b)
        hi = qi.astype(BF16)
        lo = (qi - hi.astype(F32)).astype(BF16)
        qit_ref[0, b] = jnp.concatenate([hi, lo, hi], axis=0)
        wq_ref[0, b] = stack(dwt, SUBLANES, 1, b)


def proj_odd(x, g, w_in, blk):
    p, d = x.shape
    nq, nq_dsa = p // blk, p // DSA_TQ
    s_c = C_DIM ** -0.5 * LOG2E
    s_d = D_DIM ** -0.5 * LOG2E
    s_w = (IDX_HEADS ** -0.5) * (IDX_DIM ** -0.5)
    cq_w, ck_w, cv_w, dq_w, dk_w, dv_w, dqi_w, dki_w, dw_w = _split(w_in, ODD_SPLITS)

    def lane_pad(w):
        return jnp.pad(w, ((0, 0), (0, LANES - w.shape[1])))

    w1 = jnp.concatenate([ck_w, cv_w, lane_pad(dk_w), lane_pad(dv_w), lane_pad(dki_w)], axis=1).astype(BF16)
    dw_rows = jnp.pad((dw_w * s_w)[:, :, None], ((0, 0), (0, 0), (0, SUBLANES - 1))).reshape(d, -1)
    w2t = jnp.concatenate([cq_w * s_c, cv_w, dq_w * s_d, dqi_w, dw_rows], axis=1).T.astype(BF16)
    w3 = jnp.pad(ck_w.reshape(d, -1, C_DIM), ((0, 0), (0, 0), (0, LANES - C_DIM))).reshape(d, -1).astype(BF16)
    consts = [g.reshape(1, d).astype(F32), w1, w2t, w3]
    cols = D_HEADS * DSA_TQ
    per = blk // DSA_TQ

    def per_block(g_, a, b_):
        return (jax.ShapeDtypeStruct((1, g_, nq, a, b_), BF16),
                pl.BlockSpec((1, g_, 1, a, b_), lambda i: (0, 0, i, 0, 0)))

    def per_dsa(a, dtype):
        return (jax.ShapeDtypeStruct((1, nq_dsa, a, cols), dtype), pl.BlockSpec((1, per, a, cols), lambda i: (0, i, 0, 0)))

    blocks = [per_block(2 * C_HEADS, C_DIM, blk), per_block(2 * C_HEADS, blk, C_DIM), per_block(C_HEADS, 2 * C_DIM, blk),
              per_dsa(3 * IDX_DIM, BF16), per_dsa(1, F32), per_dsa(D_DIM, BF16)]
    return pl.pallas_call(
        _proj_odd_kernel, grid=(nq,),
        in_specs=[pl.BlockSpec((blk, d), lambda i: (i, 0))] + [pl.BlockSpec(a.shape, lambda i: (0, 0)) for a in consts],
        out_specs=[pl.BlockSpec((blk, w1.shape[1]), lambda i: (i, 0))] + [s for _, s in blocks],
        out_shape=[jax.ShapeDtypeStruct((p, w1.shape[1]), F32)] + [s for s, _ in blocks],
        compiler_params=_cparams(1), name="proj_odd")(x, *consts)


def _out_odd_kernel(x_ref, c_ref, d_ref, lam_ref, sub_ref, w_ref, o_ref, *, post_scale):
    blk = x_ref.shape[0]
    lam, sub = lam_ref[...], sub_ref[...]
    parts = []
    for hd in range(C_HEADS):
        c = c_ref[0, 2 * hd, 0] - lam * c_ref[0, 2 * hd + 1, 0]
        parts.append(c * lax.rsqrt(jnp.mean(c * c, axis=0, keepdims=True) + EPS) * sub * post_scale)
    for hd in range(D_HEADS):
        parts.append(jnp.concatenate([d_ref[0, b][:, hd * DSA_TQ:(hd + 1) * DSA_TQ]
                                      for b in range(blk // DSA_TQ)], axis=1).astype(F32))
    mix = jnp.concatenate(parts, axis=0).T.astype(BF16)
    o_ref[...] = x_ref[...] + jnp.dot(mix, w_ref[...], preferred_element_type=F32)


def out_odd(x, c_t, d_t, lam, subln, post_scale, w_out):
    p, d = x.shape
    _, nmap, nq, dc, blk = c_t.shape
    per = blk // DSA_TQ
    return pl.pallas_call(
        functools.partial(_out_odd_kernel, post_scale=post_scale), grid=(nq,),
        in_specs=[pl.BlockSpec((blk, d), lambda i: (i, 0)),
                  pl.BlockSpec((1, nmap, 1, dc, blk), lambda i: (0, 0, i, 0, 0)),
                  pl.BlockSpec((1, per) + d_t.shape[2:], lambda i: (0, i, 0, 0)),
                  pl.BlockSpec((1, blk), lambda i: (0, 0)),
                  pl.BlockSpec((dc, 1), lambda i: (0, 0)),
                  pl.BlockSpec(w_out.shape, lambda i: (0, 0))],
        out_specs=pl.BlockSpec((blk, d), lambda i: (i, 0)),
        out_shape=jax.ShapeDtypeStruct((p, d), F32),
        compiler_params=_cparams(1), name="out_odd")(
            x, c_t, d_t, jnp.full((1, blk), lam, F32), subln.reshape(dc, 1).astype(F32), w_out)


def _out_even_kernel(x_ref, a_ref, b_ref, w_ref, o_ref):
    blk = x_ref.shape[0]
    mix_t = jnp.concatenate([a_ref[0, :, 0].reshape(-1, blk), b_ref[0, :, 0].reshape(-1, blk)], axis=0)
    mix = mix_t.astype(F32).T.astype(BF16)
    o_ref[...] = x_ref[...] + jnp.dot(mix, w_ref[...], preferred_element_type=F32)


def out_even(x, a_t, b_t, w_out):
    p, d = x.shape
    _, g_, nq, dv, blk = a_t.shape
    blk_spec = pl.BlockSpec((1, g_, 1, dv, blk), lambda i: (0, 0, i, 0, 0))
    return pl.pallas_call(
        _out_even_kernel, grid=(nq,),
        in_specs=[pl.BlockSpec((blk, d), lambda i: (i, 0)), blk_spec, blk_spec,
                  pl.BlockSpec(w_out.shape, lambda i: (0, 0))],
        out_specs=pl.BlockSpec((blk, d), lambda i: (i, 0)),
        out_shape=jax.ShapeDtypeStruct((p, d), F32),
        compiler_params=_cparams(1), name="out_even")(x, a_t, b_t, w_out)


def _rope_tables(pos):
    half = B_ROPE // 2
    freqs = jnp.power(jnp.float32(ROPE_THETA), -jnp.arange(half, dtype=F32) / half)
    ang = pos.astype(F32)[:, None] * freqs[None, :]
    return jnp.cos(ang), jnp.sin(ang)


def _rope(x, cos, sin):
    half = x.shape[-1] // 2
    shape = (x.shape[0],) + (1,) * (x.ndim - 2) + (half,)
    c, s = cos.reshape(shape), sin.reshape(shape)
    x1, x2 = x[..., :half], x[..., half:]
    return jnp.concatenate([x1 * c - x2 * s, x1 * s + x2 * c], axis=-1)


def _t5_bucket(rel):
    nb = T5_BUCKETS // 2
    max_exact = nb // 2
    ret = jnp.where(rel > 0, nb, 0)
    n = jnp.abs(rel)
    nf = jnp.maximum(n, 1).astype(F32)
    large = max_exact + (jnp.log(nf / max_exact) / math.log(T5_MAX_DIST / max_exact)
                         * (nb - max_exact)).astype(I32)
    large = jnp.minimum(large, nb - 1)
    return ret + jnp.where(n < max_exact, n, large)


def _chunk_diff(tq, tk, n_tiles):
    r = np.arange(tq)[None, None, :]
    c = np.arange(tk)[None, :, None]
    t = np.arange(n_tiles)[:, None, None]
    return c // CHUNK - r // CHUNK - t * (tq // CHUNK)


def _strip_distances(tq, tk, n_tiles):
    y = np.arange(tq + tk)[None, :]
    t = np.arange(n_tiles)[:, None]
    return np.where(y < tq, -y, tq + tk - y) - t * tq


def _toeplitz(strip, tq, tk):
    n = tq + tk
    lead = strip.shape[:-1]
    rows = jnp.broadcast_to(strip[..., None, :], lead + (tk, n)).reshape(lead + (tk * n,))
    return rows[..., :tk * (n - 1)].reshape(lead + (tk, n - 1))[..., :tq]


def band_tiles(relbias, blk):
    n_tiles = (A_LEFT_CHUNKS * CHUNK) // blk + 1
    dchunk = _chunk_diff(blk, blk, n_tiles)
    adm = (dchunk <= 0) & (-dchunk <= A_LEFT_CHUNKS)
    idx = np.clip(-_strip_distances(blk, blk, n_tiles), -A_MAX_REL, A_MAX_REL) + A_MAX_REL
    bias = _toeplitz(relbias.astype(F32)[:, idx] * LOG2E, blk, blk)
    tiles = jnp.where(adm[None], bias, MASKED)
    far = jnp.full((relbias.shape[0], 1, blk, blk), MASKED, F32)
    return jnp.concatenate([tiles, far], axis=1), n_tiles


def causal_tiles(blk):
    t0 = np.where(_chunk_diff(blk, blk, 1) <= 0, 0.0, MASKED).astype(np.float32)
    return jnp.asarray(np.concatenate([t0, np.zeros_like(t0)], axis=0)[None])


def t5_tiles(tab, tq, tk, n_tiles):
    assert n_tiles * tq - tk + 1 >= T5_MAX_DIST
    tab = tab.astype(F32) * LOG2E
    strip = jnp.moveaxis(tab[_t5_bucket(jnp.asarray(_strip_distances(tq, tk, n_tiles), I32))], -1, 0)
    tiles = jnp.where((_chunk_diff(tq, tk, n_tiles) <= 0)[None], _toeplitz(strip, tq, tk), MASKED)
    far = tab[_t5_bucket(jnp.asarray([-(n_tiles * tq + tk)], I32))[0]]
    far = jnp.broadcast_to(far[:, None, None, None], (tab.shape[1], 1, tk, tq))
    return jnp.concatenate([tiles, far], axis=1)


def _row_blocks(x, blk):
    b, t, g, d = x.shape
    return jnp.transpose(x.reshape(b, t // blk, blk, g, d), (0, 3, 1, 2, 4)).astype(BF16)


def _col_blocks(x, blk):
    b, t, g, d = x.shape
    return jnp.transpose(x.reshape(b, t // blk, blk, g, d), (0, 3, 1, 4, 2)).astype(BF16)


def _from_col_blocks(o):
    b, g, nq, d, blk = o.shape
    return jnp.transpose(o, (0, 2, 4, 1, 3)).reshape(b, nq * blk, g * d)


def _pad_rows(x, n, front=0):
    back = n - x.shape[1] - front
    return jnp.pad(x, ((0, 0), (front, back)) + ((0, 0),) * (x.ndim - 2))


def _split(h, sizes):
    out, o = [], 0
    for s in sizes:
        out.append(h[:, o:o + s])
        o += s
    return out


def _split3(x):
    hi = x.astype(BF16)
    lo = (x - hi.astype(F32)).astype(BF16)
    return hi, lo


def _dsa_keys(dk_all, dv_all, dki_all):
    b, nkb = dk_all.shape[0], dk_all.shape[1] // DSA_TK
    kh, kl = _split3(dki_all)
    ki = jnp.concatenate([kh, kh, kl], axis=-1).reshape(b, nkb, DSA_TK, 3 * IDX_DIM)
    k = dk_all.astype(BF16).reshape(b, nkb, DSA_TK, D_DIM)
    vt = jnp.transpose(dv_all.astype(BF16).reshape(b, nkb, DSA_TK, D_DIM), (0, 1, 3, 2))
    norm = jnp.zeros((b, nkb, NORM_ROWS, DSA_TK), BF16).at[:, :, 0].set(1.0)
    return ki, k, jnp.concatenate([vt, norm], axis=2)


def _dsa_call(dq, dqi, dw, dk_all, dv_all, dki_all, tiles, i_off, n_sel):
    b, tq_all = dq.shape[:2]
    nq = tq_all // DSA_TQ

    def stack_t(x):
        d = x.shape[-1]
        return jnp.transpose(x.reshape(b, nq, DSA_TQ, D_HEADS, d), (0, 1, 4, 3, 2)).reshape(b, nq, d, D_HEADS * DSA_TQ)

    qh, ql = _split3(dqi)
    qit = stack_t(jnp.concatenate([qh, ql, qh], axis=-1))
    w = stack_t(dw[..., None])
    qt = stack_t((dq * (D_DIM ** -0.5 * LOG2E)).astype(BF16))
    ki, k, vt = _dsa_keys(dk_all, dv_all, dki_all)
    o = dsa(qit, w, ki, qt, k, vt, tiles, i_off=i_off, n_sel=n_sel)
    o = jnp.transpose(o.reshape(b, nq, D_DIM, D_HEADS, DSA_TQ), (0, 1, 4, 3, 2))
    return o.reshape(b, tq_all, D_HEADS * D_DIM)


P_BLK = 512
P_HEADS = 2
S_BLK = 128


def kernel(x_prompt, x_sample, cache_a_k, cache_a_v, cache_b_ckv, cache_b_krope, cache_c_k, cache_c_v, cache_d_k, cache_d_v, cache_d_kidx, t5_table, norm_attn, norm_ffn, final_norm, even_w_in, even_w_out, a_relbias, b_q_norm, b_kv_norm, b_w_uq, b_w_ukv, odd_w_in, odd_w_out, c_lambda_q1, c_lambda_k1, c_lambda_q2, c_lambda_k2, c_subln, ffn_w_gate, ffn_w_up, ffn_w_down):
    pb, pt, d = x_prompt.shape
    sb, st, _ = x_sample.shape
    assert pb == 1 and st == CHUNK
    past = cache_b_ckv.shape[2]
    n_a = cache_a_k.shape[2]
    depth = norm_attn.shape[0]
    n_p, n_s = pb * pt, sb * st
    s_len = past + st
    s_pad = -(-s_len // DSA_TK) * DSA_TK
    s_i = past // S_BLK
    s_i_dsa = past // DSA_TQ
    assert past % S_BLK == 0 and s_pad % S_BLK == 0 and past >= n_a
    p_blk = min(P_BLK, pt)

    cos_p, sin_p = _rope_tables(jnp.arange(pt))
    cos_s, sin_s = _rope_tables(jnp.tile(past + jnp.arange(st), sb))

    def rows_p(a):
        return a.reshape(pb, pt, *a.shape[1:])

    def rows_s(a):
        return a.reshape(sb, st, *a.shape[1:])

    def pad_cols(w):
        n = -(-w.shape[1] // LANES) * LANES
        return jnp.pad(w, ((0, 0), (0, n - w.shape[1]))).astype(BF16)

    def sample_q(a):
        return _pad_rows(a, S_BLK)

    def sample_keys(cache, new, front=0):
        return _pad_rows(jnp.concatenate([cache, new], axis=1), s_pad, front)

    def attend(q, k, v, tiles, blk, n_tiles, has_far, i_off=0):
        o = flash(_col_blocks(q, blk), _row_blocks(k, blk), _col_blocks(v, blk), tiles,
                  n_tiles=n_tiles, has_far=has_far, i_off=i_off, heads=q.shape[2])
        return _from_col_blocks(o)

    t5_c, t5_d = t5_table[:, :C_HEADS], t5_table[:, C_HEADS:]
    ctiles_p = causal_tiles(p_blk)
    ctiles_s = causal_tiles(S_BLK)
    c_tiles_p = t5_tiles(t5_c, p_blk, p_blk, 2)
    c_tiles_s = t5_tiles(t5_c, S_BLK, S_BLK, 2)
    d_tiles = t5_tiles(t5_d, DSA_TQ, DSA_TK, 3)
    d_tiles = jnp.transpose(d_tiles, (1, 2, 0, 3)).reshape(4, DSA_TK, D_HEADS * DSA_TQ)
    n_sel_p = min(TOPK_MAX, pt // 4)
    n_sel_s = min(TOPK_MAX, s_len // 4)

    outs_even = {k: [] for k in ("a_k_p", "a_k_s", "a_v_p", "a_v_s", "ckv_p", "ckv_s", "kr_p", "kr_s")}
    outs_odd = {k: [] for k in ("c_k_p", "c_k_s", "c_v_p", "c_v_s", "d_k_p", "d_k_s", "d_v_p", "d_v_s", "d_ki_p", "d_ki_s")}

    def mla_keys(kn, krope):
        return jnp.concatenate([kn, jnp.broadcast_to(krope[:, :, None, :], kn.shape[:3] + (B_ROPE,))], axis=-1)

    def even_prompt(x, l, i):
        a_tiles, nt = band_tiles(a_relbias[i], p_blk)
        akv, ckv, y1, y2, qta, ka, vta, qtb, kb, vtb = proj_even(
            x, norm_attn[l], even_w_in[i], b_q_norm[i], b_w_uq[i], b_kv_norm[i], b_w_ukv[i], cos_p, sin_p, p_blk)
        a_t = flash(qta, ka, vta, a_tiles, n_tiles=nt, has_far=False, out_dtype=BF16, heads=P_HEADS)
        b_t = flash(qtb, kb, vtb, ctiles_p, n_tiles=1, has_far=True, out_dtype=BF16, heads=P_HEADS)
        half = B_ROPE // 2
        hd = A_HEADS * A_DIM
        outs_even["a_k_p"].append(akv[pt - n_a:, :hd].reshape(pb, n_a, A_HEADS, A_DIM))
        outs_even["a_v_p"].append(akv[pt - n_a:, hd:].reshape(pb, n_a, A_HEADS, A_DIM))
        outs_even["ckv_p"].append(rows_p(ckv))
        outs_even["kr_p"].append(rows_p(jnp.concatenate([y1[:, :half], y2[:, :half]], axis=-1)))
        return out_even(x, a_t, b_t, even_w_out[i].astype(BF16))

    def even_sample(x, l, i):
        h = dense(x, pad_cols(even_w_in[i]), g=norm_attn[l])
        aq, ak, av, cq, ckv_raw, kr_raw = _split(h, EVEN_SPLITS)
        qb = dense(cq, b_w_uq[i].astype(BF16), g=b_q_norm[i]).reshape(-1, B_HEADS, B_NOPE + B_ROPE)
        kv_new, ckv = dense(ckv_raw, b_w_ukv[i].astype(BF16), g=b_kv_norm[i], emit_h=True)
        kr = _rope(kr_raw, cos_s, sin_s)
        q_mla = jnp.concatenate([qb[..., :B_NOPE], _rope(qb[..., B_NOPE:], cos_s, sin_s)], axis=-1)
        q_mla = q_mla * ((B_NOPE + B_ROPE) ** -0.5 * LOG2E)
        kv_s = rows_s(kv_new.reshape(-1, B_HEADS, B_NOPE + B_V))
        aq = (aq * (A_DIM ** -0.5 * LOG2E)).reshape(-1, A_HEADS, A_DIM)
        ak = ak.reshape(-1, A_HEADS, A_DIM)
        av = av.reshape(-1, A_HEADS, A_DIM)
        a_tiles, nt = band_tiles(a_relbias[i], S_BLK)
        ak_all = jnp.concatenate([cache_a_k[i], rows_s(ak)], axis=1)
        av_all = jnp.concatenate([cache_a_v[i], rows_s(av)], axis=1)
        a_out = attend(sample_q(rows_s(aq)), _pad_rows(ak_all, s_pad, past - n_a),
                       _pad_rows(av_all, s_pad, past - n_a), a_tiles, S_BLK, nt, False, s_i)[:, :st]
        kv_c = dense(cache_b_ckv[i].reshape(sb * past, KV_LORA), b_w_ukv[i].astype(BF16))
        kv_c = kv_c.reshape(sb, past, B_HEADS, B_NOPE + B_V)
        kn_all = sample_keys(kv_c[..., :B_NOPE], kv_s[..., :B_NOPE])
        v_all = sample_keys(kv_c[..., B_NOPE:], kv_s[..., B_NOPE:])
        kr_all = sample_keys(cache_b_krope[i], rows_s(kr))
        b_out = attend(sample_q(rows_s(q_mla)), mla_keys(kn_all, kr_all), v_all, ctiles_s, S_BLK, 1, True, s_i)[:, :st]
        mix = jnp.concatenate([a_out.reshape(n_s, -1), b_out.reshape(n_s, -1)], axis=-1)
        outs_even["a_k_s"].append(ak_all[:, -n_a:])
        outs_even["a_v_s"].append(av_all[:, -n_a:])
        outs_even["ckv_s"].append(rows_s(ckv))
        outs_even["kr_s"].append(rows_s(kr))
        return dense(mix.astype(BF16), even_w_out[i].astype(BF16), res=x)

    def diff_lambda(l, i):
        lam_init = 0.8 - 0.6 * math.exp(-0.3 * l)
        lam = (jnp.exp(jnp.sum(c_lambda_q1[i].astype(F32) * c_lambda_k1[i].astype(F32)))
               - jnp.exp(jnp.sum(c_lambda_q2[i].astype(F32) * c_lambda_k2[i].astype(F32))) + lam_init)
        return lam_init, lam

    def odd_prompt(x, l, i):
        lam_init, lam = diff_lambda(l, i)
        st_, cqt, kc, cvt, qit, wq, qt = proj_odd(x, norm_attn[l], odd_w_in[i], p_blk)
        wc = C_HEADS * 2 * C_DIM
        ck, cv = st_[:, :wc], st_[:, wc:2 * wc]
        dk, dv, dki = (st_[:, 2 * wc + n * LANES:2 * wc + n * LANES + D_DIM] for n in range(3))
        c_t = flash(cqt, kc, cvt, c_tiles_p, n_tiles=2, has_far=True, heads=P_HEADS)
        ki, k, vt = _dsa_keys(dk[None], dv[None], dki[None])
        d_t = dsa(qit, wq, ki, qt, k, vt, d_tiles, i_off=0, n_sel=n_sel_p, out_dtype=BF16)
        for name, arr, shp in (("c_k", ck, (C_HEADS, 2 * C_DIM)), ("c_v", cv, (C_HEADS, 2 * C_DIM)),
                               ("d_k", dk, (D_DIM,)), ("d_v", dv, (D_DIM,)), ("d_ki", dki, (IDX_DIM,))):
            outs_odd[name + "_p"].append(arr.reshape(pb, pt, *shp))
        return out_odd(x, c_t, d_t, lam, c_subln[i], 1.0 - lam_init, odd_w_out[i].astype(BF16))

    def odd_sample(x, l, i):
        lam_init, lam = diff_lambda(l, i)
        h = dense(x, pad_cols(odd_w_in[i]), g=norm_attn[l])
        cq, ck, cv, dq, dk, dv, dqi, dki, dw = _split(h, ODD_SPLITS)
        cq = (cq * (C_DIM ** -0.5 * LOG2E)).reshape(-1, 2 * C_HEADS, C_DIM)
        ck3 = ck.reshape(-1, 2 * C_HEADS, C_DIM)
        cv3 = cv.reshape(-1, C_HEADS, 2 * C_DIM)
        dq = dq.reshape(-1, D_HEADS, D_DIM)
        dqi = dqi.reshape(-1, IDX_HEADS, IDX_DIM)
        dw = dw * ((IDX_HEADS ** -0.5) * (IDX_DIM ** -0.5))

        def diff_combine(o):
            b_, t_, _ = o.shape
            o = o.reshape(b_, t_, C_HEADS, 2, 2 * C_DIM)
            c = o[:, :, :, 0] - lam * o[:, :, :, 1]
            return (_rms(c, c_subln[i].astype(F32)) * (1.0 - lam_init)).reshape(b_, t_, -1)

        rows, tag, nb_, nt_ = rows_s, "_s", sb, st
        ck_all = sample_keys(cache_c_k[i].reshape(sb, past, 2 * C_HEADS, C_DIM), rows(ck3))
        cv_all = sample_keys(cache_c_v[i], rows(cv3))
        c_out = attend(sample_q(rows(cq)), ck_all, cv_all, c_tiles_s, S_BLK, 2, True, s_i)[:, :st]
        d_out = _dsa_call(_pad_rows(rows(dq), DSA_TQ), _pad_rows(rows(dqi), DSA_TQ), _pad_rows(rows(dw), DSA_TQ),
                          sample_keys(cache_d_k[i], rows(dk)), sample_keys(cache_d_v[i], rows(dv)),
                          sample_keys(cache_d_kidx[i], rows(dki)), d_tiles, s_i_dsa, n_sel_s)[:, :st]
        mix = jnp.concatenate([diff_combine(c_out).reshape(nb_ * nt_, -1), d_out.reshape(nb_ * nt_, -1)], axis=-1)
        for name, arr, shp in (("c_k", ck, (C_HEADS, 2 * C_DIM)), ("c_v", cv, (C_HEADS, 2 * C_DIM)),
                               ("d_k", dk, (D_DIM,)), ("d_v", dv, (D_DIM,)), ("d_ki", dki, (IDX_DIM,))):
            outs_odd[name + tag].append(rows(arr).reshape(nb_, nt_, *shp))
        return dense(mix.astype(BF16), odd_w_out[i].astype(BF16), res=x)

    xp, xs = x_prompt.reshape(n_p, d), x_sample.reshape(n_s, d)
    for l in range(depth):
        i = l // 2
        if l % 2 == 0:
            xp, xs = even_prompt(xp, l, i), even_sample(xs, l, i)
        else:
            xp, xs = odd_prompt(xp, l, i), odd_sample(xs, l, i)
        wg, wu, wd = ffn_w_gate[l].astype(BF16), ffn_w_up[l].astype(BF16), ffn_w_down[l].astype(BF16)
        fg = final_norm if l == depth - 1 else None
        xp, xs = ffn(xp, norm_ffn[l], wg, wu, wd, final_g=fg), ffn(xs, norm_ffn[l], wg, wu, wd, final_g=fg)

    se = {k: jnp.stack(v, axis=0) for k, v in outs_even.items()}
    so = {k: jnp.stack(v, axis=0) for k, v in outs_odd.items()}
    return (xp.reshape(pb, pt, d), xs.reshape(sb, st, d), se["a_k_p"], se["a_k_s"], se["a_v_p"], se["a_v_s"],
            se["ckv_p"], se["ckv_s"], se["kr_p"], se["kr_s"],
            so["c_k_p"], so["c_k_s"], so["c_v_p"], so["c_v_s"],
            so["d_k_p"], so["d_k_s"], so["d_v_p"], so["d_v_s"], so["d_ki_p"], so["d_ki_s"])
```

```python
import functools
import math

import numpy as np
import jax
import jax.numpy as jnp
from jax import lax
from jax.experimental import pallas as pl
from jax.experimental.pallas import tpu as pltpu

F32, BF16, I32 = jnp.float32, jnp.bfloat16, jnp.int32

D_MODEL = 1024
CHUNK = 64
EPS = 1e-6
A_HEADS, A_DIM, A_LEFT_CHUNKS, A_MAX_REL = 8, 64, 8, 128
B_HEADS, B_NOPE, B_ROPE, B_V = 8, 64, 32, 64
Q_LORA, KV_LORA = 256, 128
ROPE_THETA = 10000.0
C_HEADS, C_DIM = 4, 64
D_HEADS, D_DIM = 8, 64
IDX_HEADS, IDX_DIM = 8, 64
TOPK_MAX = 256
T5_BUCKETS, T5_MAX_DIST = 32, 128
EVEN_SPLITS = (512, 512, 512, Q_LORA, KV_LORA, B_ROPE)
ODD_SPLITS = (512, 512, 512, 512, D_DIM, D_DIM, 512, IDX_DIM, IDX_HEADS)

LANES = 128
SUBLANES = 8
VMEM_LIMIT = 56 * 1024 * 1024
MASKED = -1e30
UNSELECTED = -2e30
M_INIT = -1e30
SUM_MAX, SUM_MIN = 1e18, 1e-18
LOG2E = math.log2(math.e)
INT_MIN = -(2 ** 31)
INT_MAX = 2 ** 31 - 1

DSA_TQ, DSA_TK = 128, 256
COUNT_ROWS = 8 * SUBLANES
FAR_UNROLL = 4
DSA_FAR_UNROLL = 2
NORM_ROWS = 16
DSA_SCORE_UNROLL = 4


def _cparams(n_axes):
    return pltpu.CompilerParams(dimension_semantics=("arbitrary",) * n_axes, vmem_limit_bytes=VMEM_LIMIT)


def _rms(x, g):
    return x * lax.rsqrt(jnp.mean(x * x, axis=-1, keepdims=True) + EPS) * g


def _dense_kernel(*refs, norm, res, emit_h):
    it = iter(refs)
    x_ref = next(it)
    g_ref = next(it) if norm else None
    w_ref = next(it)
    r_ref = next(it) if res else None
    o_ref = next(it)
    h_ref = next(it) if emit_h else None
    x = x_ref[...]
    if norm:
        x = _rms(x.astype(F32), g_ref[...])
        if emit_h:
            h_ref[...] = x
    acc = jnp.dot(x.astype(BF16), w_ref[...], preferred_element_type=F32)
    if res:
        acc = acc + r_ref[...]
    o_ref[...] = acc


def dense(x, w, g=None, res=None, emit_h=False, tm=512):
    m, k = x.shape
    n = w.shape[1]
    assert m % tm == 0 and n % LANES == 0
    norm = g is not None
    args, specs = [x], [pl.BlockSpec((tm, k), lambda i: (i, 0))]
    if norm:
        args.append(g.reshape(1, k).astype(F32))
        specs.append(pl.BlockSpec((1, k), lambda i: (0, 0)))
    args.append(w)
    specs.append(pl.BlockSpec((k, n), lambda i: (0, 0)))
    if res is not None:
        args.append(res)
        specs.append(pl.BlockSpec((tm, n), lambda i: (i, 0)))
    out_shape = [jax.ShapeDtypeStruct((m, n), F32)]
    out_specs = [pl.BlockSpec((tm, n), lambda i: (i, 0))]
    if emit_h:
        out_shape.append(jax.ShapeDtypeStruct((m, k), F32))
        out_specs.append(pl.BlockSpec((tm, k), lambda i: (i, 0)))
    outs = pl.pallas_call(
        functools.partial(_dense_kernel, norm=norm, res=res is not None, emit_h=emit_h),
        grid=(m // tm,), in_specs=specs, out_specs=out_specs, out_shape=out_shape,
        compiler_params=_cparams(1), name="dense")(*args)
    return outs if emit_h else outs[0]


def _ffn_kernel(x_ref, g_ref, wg_ref, wu_ref, wd_ref, fg_ref, o_ref, h_sc, acc_sc, *, final):
    f = pl.program_id(1)

    @pl.when(f == 0)
    def _():
        x = x_ref[...]
        h_sc[...] = _rms(x, g_ref[...]).astype(BF16)
        acc_sc[...] = x

    h = h_sc[...]
    gate = jnp.dot(h, wg_ref[...], preferred_element_type=F32)
    up = jnp.dot(h, wu_ref[...], preferred_element_type=F32)
    a = (gate * jax.nn.sigmoid(gate) * up).astype(BF16)
    acc_sc[...] += jnp.dot(a, wd_ref[...], preferred_element_type=F32)

    @pl.when(f == pl.num_programs(1) - 1)
    def _():
        y = acc_sc[...]
        if final:
            y = _rms(y, fg_ref[...])
        o_ref[...] = y


def ffn(x, g, wg, wu, wd, final_g=None, tm=512, nf=2):
    m, d = x.shape
    hid = wg.shape[1]
    tf = hid // nf
    assert m % tm == 0 and hid % nf == 0 and tf % LANES == 0
    final = final_g is not None
    fg = (final_g if final else g).reshape(1, d).astype(F32)
    return pl.pallas_call(
        functools.partial(_ffn_kernel, final=final),
        grid=(m // tm, nf),
        in_specs=[pl.BlockSpec((tm, d), lambda i, f: (i, 0)),
                  pl.BlockSpec((1, d), lambda i, f: (0, 0)),
                  pl.BlockSpec((d, tf), lambda i, f: (0, f)),
                  pl.BlockSpec((d, tf), lambda i, f: (0, f)),
                  pl.BlockSpec((tf, d), lambda i, f: (f, 0)),
                  pl.BlockSpec((1, d), lambda i, f: (0, 0))],
        out_specs=pl.BlockSpec((tm, d), lambda i, f: (i, 0)),
        out_shape=jax.ShapeDtypeStruct((m, d), F32),
        scratch_shapes=[pltpu.VMEM((tm, d), BF16), pltpu.VMEM((tm, d), F32)],
        compiler_params=_cparams(2), name="ffn")(x, g.reshape(1, d).astype(F32), wg, wu, wd, fg)


def _flash_kernel(qt_ref, k_ref, vt_ref, tile_ref, o_ref, m_sc, l_sc, acc_sc, *, n_tiles, has_far, i_off,
                  heads, kdiv, vdiv, bdiv):
    i = pl.program_id(2) + i_off
    j_lo = 0 if has_far else jnp.maximum(i - (n_tiles - 1), 0)

    def one_head(g):
        gk, gv, gt = g // kdiv, g // vdiv, g // bdiv
        qt = qt_ref[0, g, 0]

        def raw(j):
            return jnp.dot(k_ref[0, gk, j], qt, preferred_element_type=F32)

        def scores(j):
            return raw(j) + tile_ref[gt, jnp.minimum(i - j, n_tiles)]

        def reset():
            l_sc[...] = jnp.zeros(l_sc.shape, F32)
            acc_sc[...] = jnp.zeros(acc_sc.shape, F32)

        def accumulate(j, p):
            l_sc[...] += jnp.sum(p, axis=0, keepdims=True)
            acc_sc[...] += jnp.dot(vt_ref[0, gv, j], p.astype(BF16), preferred_element_type=F32)

        reset()
        s_diag = raw(i) + tile_ref[gt, 0]
        m = jnp.max(s_diag, axis=0, keepdims=True)
        accumulate(i, jnp.exp2(s_diag - m))
        if has_far:
            m_far = m - tile_ref[gt, n_tiles, 0:1, :]
            n_far = jnp.maximum(i - (n_tiles - 1), 0)

            def far_group(jj, carry):
                ps = [jnp.exp2(raw(FAR_UNROLL * jj + u) - m_far) for u in range(FAR_UNROLL)]
                for u in range(FAR_UNROLL):
                    accumulate(FAR_UNROLL * jj + u, ps[u])
                return carry

            def far_single(j, carry):
                accumulate(j, jnp.exp2(raw(j) - m_far))
                return carry

            n_grouped = (n_far // FAR_UNROLL) * FAR_UNROLL
            lax.fori_loop(0, n_far // FAR_UNROLL, far_group, 0)
            lax.fori_loop(n_grouped, n_far, far_single, 0)
        for t in range(1, n_tiles):
            @pl.when(i - t >= 0)
            def _(t=t):
                accumulate(i - t, jnp.exp2(raw(i - t) + tile_ref[gt, t] - m))

        l = l_sc[...]
        in_range = (jnp.max(l) < SUM_MAX) & (jnp.min(l) > SUM_MIN)

        @pl.when(jnp.logical_not(in_range))
        def _():
            reset()
            m_sc[...] = jnp.full(m_sc.shape, M_INIT, F32)

            def body(j, carry):
                s = scores(j)
                m_prev = m_sc[...]
                m_new = jnp.maximum(m_prev, jnp.max(s, axis=0, keepdims=True))
                alpha = jnp.exp2(m_prev - m_new)
                p = jnp.exp2(s - m_new)
                l_sc[...] = alpha * l_sc[...] + jnp.sum(p, axis=0, keepdims=True)
                acc_sc[...] = alpha * acc_sc[...] + jnp.dot(vt_ref[0, gv, j], p.astype(BF16),
                                                            preferred_element_type=F32)
                m_sc[...] = m_new
                return carry

            lax.fori_loop(j_lo, i + 1, body, 0)

        o_ref[0, g, 0] = (acc_sc[...] / l_sc[...]).astype(o_ref.dtype)

    if heads == 1:
        one_head(0)
    else:
        def head_step(g, carry):
            one_head(g)
            return carry
        lax.fori_loop(0, heads, head_step, 0)


def flash(qt, k, vt, tiles, *, n_tiles, has_far, i_off=0, out_dtype=F32, heads=1):
    b, g, nq, dq, blk = qt.shape
    gk, nb = k.shape[1], k.shape[2]
    gv, dv = vt.shape[1], vt.shape[3]
    gb = tiles.shape[0]
    assert tiles.shape[1] == n_tiles + 1
    kdiv, vdiv, bdiv = g // gk, g // gv, g // gb
    assert g % heads == 0

    def shared(div):
        if heads % div == 0:
            return heads // div, 1
        assert div % heads == 0
        return 1, div // heads

    (hk, kd), (hv, vd), (hb, bd) = shared(kdiv), shared(vdiv), shared(bdiv)
    return pl.pallas_call(
        functools.partial(_flash_kernel, n_tiles=n_tiles, has_far=has_far, i_off=i_off,
                          heads=heads, kdiv=kdiv, vdiv=vdiv, bdiv=bdiv),
        grid=(b, g // heads, nq),
        in_specs=[pl.BlockSpec((1, heads, 1, dq, blk), lambda bi, gi, i: (bi, gi, i, 0, 0)),
                  pl.BlockSpec((1, hk, nb, blk, dq), lambda bi, gi, i: (bi, gi // kd, 0, 0, 0)),
                  pl.BlockSpec((1, hv, nb, dv, blk), lambda bi, gi, i: (bi, gi // vd, 0, 0, 0)),
                  pl.BlockSpec((hb, n_tiles + 1, blk, blk), lambda bi, gi, i: (gi // bd, 0, 0, 0))],
        out_specs=pl.BlockSpec((1, heads, 1, dv, blk), lambda bi, gi, i: (bi, gi, i, 0, 0)),
        out_shape=jax.ShapeDtypeStruct((b, g, nq, dv, blk), out_dtype),
        scratch_shapes=[pltpu.VMEM((1, blk), F32), pltpu.VMEM((1, blk), F32), pltpu.VMEM((dv, blk), F32)],
        compiler_params=_cparams(3), name="flash")(qt, k, vt, tiles)


def _dsa_kernel(qit_ref, w_ref, ki_ref, qt_ref, k_ref, vt_ref, tile_ref, o_ref,
                key_sc, m_sc, acc_sc, *, i_off, n_sel, idx_bits):
    tq, tk, nh = DSA_TQ, DSA_TK, D_HEADS
    i = pl.program_id(1) + i_off
    nkb = (i * tq) // tk + 1
    krow = lax.broadcasted_iota(I32, (tk, tq), 0)
    qcol = lax.broadcasted_iota(I32, (tk, tq), 1)
    q_chunk = jnp.right_shift(i * tq + qcol, 6)

    def head(x, h):
        return x[:, h * tq:(h + 1) * tq]

    qit = qit_ref[0, 0]
    w = w_ref[0, 0]

    def score_block(j, mask_future):
        lg = jnp.maximum(jnp.dot(ki_ref[0, j], qit, preferred_element_type=F32), 0.0) * w
        sc = head(lg, 0)
        for h in range(1, IDX_HEADS):
            sc = sc + head(lg, h)
        sc = jnp.where(sc == 0.0, 0.0, sc)
        bits = pltpu.bitcast(sc, I32)
        key = bits ^ (jnp.right_shift(bits, 31) & INT_MAX)
        if mask_future:
            key = jnp.where(jnp.right_shift(j * tk + krow, 6) <= q_chunk, key, INT_MIN)
        key_sc[j] = key

    def score_group(jj, carry):
        for u in range(DSA_SCORE_UNROLL):
            score_block(DSA_SCORE_UNROLL * jj + u, False)
        return carry

    def score_single(j, carry):
        score_block(j, False)
        return carry

    n_full = nkb - 1
    lax.fori_loop(0, n_full // DSA_SCORE_UNROLL, score_group, 0)
    lax.fori_loop((n_full // DSA_SCORE_UNROLL) * DSA_SCORE_UNROLL, n_full, score_single, 0)
    score_block(n_full, True)

    def count(pred_fn):
        def body(j, acc):
            ind = jnp.where(pred_fn(key_sc[j], j * tk + krow), 1.0, 0.0)
            return acc + jnp.sum(ind.reshape(tk // COUNT_ROWS, COUNT_ROWS, tq), axis=0)
        acc = lax.fori_loop(0, nkb, body, jnp.zeros((COUNT_ROWS, tq), F32))
        return jnp.sum(acc, axis=0, keepdims=True)

    target = float(n_sel)
    qpos = i * tq + lax.broadcasted_iota(I32, (1, tq), 1)
    n_adm = ((jnp.right_shift(qpos, 6) + 1) * CHUNK).astype(F32)
    real = n_adm > target

    def unsettled(c_thr):
        settled = (c_thr == target) | jnp.logical_not(real)
        return jnp.min(jnp.where(settled, 1.0, 0.0)) < 1.0

    def value_bit(state):
        thr, c_thr, b = state
        cand = thr + jnp.left_shift(jnp.int32(1), 31 - b)
        cnt = count(lambda k, idx: k >= cand)
        ge = cnt >= target
        return jnp.where(ge, cand, thr), jnp.where(ge, cnt, c_thr), b + 1

    thr, c_thr, _ = lax.while_loop(lambda s: (s[2] < 32) & unsettled(s[1]), value_bit,
                                   (jnp.full((1, tq), INT_MIN, I32), n_adm, jnp.int32(0)))

    def tie_search():
        need = target - count(lambda k, idx: k > thr)

        def index_bit(b, y):
            cand = y + jnp.left_shift(jnp.int32(1), idx_bits - 1 - b)
            cnt = count(lambda k, idx: (k == thr) & (idx < cand))
            return jnp.where(cnt < need, cand, y)
        return lax.fori_loop(0, idx_bits, index_bit, jnp.zeros((1, tq), I32))

    has_ties = jnp.max(jnp.where((c_thr > target) & real, 1.0, 0.0)) > 0.0
    last = lax.cond(has_ties, tie_search, lambda: jnp.full((1, tq), INT_MAX, I32))
    last = jnp.where(real, last, -1)

    qt = qt_ref[0, 0]
    n_near = 3

    def raw(j):
        return jnp.dot(k_ref[0, j], qt, preferred_element_type=F32)

    def tile_of(j):
        return jnp.minimum((i * tq - j * tk) // tq, n_near)

    def select(j, s):
        key = key_sc[j]
        sel = (key > thr) | ((key == thr) & (j * tk + krow <= last))
        selb = jnp.where(sel, 0.0, UNSELECTED)
        return jnp.concatenate([head(s, h) + selb for h in range(nh)], axis=1)

    dv = o_ref.shape[2]
    def reset():
        acc_sc[...] = jnp.zeros(acc_sc.shape, F32)

    def accumulate(j, p):
        acc_sc[...] += jnp.dot(vt_ref[0, j], p.astype(BF16), preferred_element_type=F32)

    reset()
    s_last = raw(nkb - 1) + tile_ref[tile_of(nkb - 1)]
    m = jnp.max(s_last, axis=0, keepdims=True)
    accumulate(nkb - 1, jnp.exp2(select(nkb - 1, s_last) - m))
    m_far = m - tile_ref[n_near, 0:1, :]
    n_far = jnp.maximum(i - 1, 0) // 2

    def far_group(jj, carry):
        ps = [jnp.exp2(select(DSA_FAR_UNROLL * jj + u, raw(DSA_FAR_UNROLL * jj + u)) - m_far)
              for u in range(DSA_FAR_UNROLL)]
        for u in range(DSA_FAR_UNROLL):
            accumulate(DSA_FAR_UNROLL * jj + u, ps[u])
        return carry

    def far_single(j, carry):
        accumulate(j, jnp.exp2(select(j, raw(j)) - m_far))
        return carry

    def near(j, carry):
        accumulate(j, jnp.exp2(select(j, raw(j) + tile_ref[tile_of(j)]) - m))
        return carry

    n_grouped = (n_far // DSA_FAR_UNROLL) * DSA_FAR_UNROLL
    lax.fori_loop(0, n_far // DSA_FAR_UNROLL, far_group, 0)
    lax.fori_loop(n_grouped, n_far, far_single, 0)
    lax.fori_loop(n_far, nkb - 1, near, 0)

    l = acc_sc[dv:dv + 1, :]
    in_range = (jnp.max(l) < SUM_MAX) & (jnp.min(l) > SUM_MIN)

    @pl.when(jnp.logical_not(in_range))
    def _():
        reset()
        m_sc[...] = jnp.full(m_sc.shape, M_INIT, F32)

        def attend(j, carry):
            s = select(j, raw(j) + tile_ref[tile_of(j)])
            m_prev = m_sc[...]
            m_new = jnp.maximum(m_prev, jnp.max(s, axis=0, keepdims=True))
            alpha = jnp.exp2(m_prev - m_new)
            p = jnp.exp2(s - m_new)
            acc_sc[...] = alpha * acc_sc[...] + jnp.dot(vt_ref[0, j], p.astype(BF16), preferred_element_type=F32)
            m_sc[...] = m_new
            return carry

        lax.fori_loop(0, nkb, attend, 0)

    o_ref[0, 0] = (acc_sc[0:dv, :] / acc_sc[dv:dv + 1, :]).astype(o_ref.dtype)


def dsa(qit, w, ki, qt, k, vt, tiles, *, i_off, n_sel, out_dtype=F32):
    b, nq, kdim, cols = qit.shape
    nkb = ki.shape[1]
    dk, dva = k.shape[-1], vt.shape[2]
    dv = dva - NORM_ROWS
    idx_bits = max(1, (nkb * DSA_TK - 1).bit_length())
    return pl.pallas_call(
        functools.partial(_dsa_kernel, i_off=i_off, n_sel=n_sel, idx_bits=idx_bits),
        grid=(b, nq),
        in_specs=[pl.BlockSpec((1, 1, kdim, cols), lambda bi, i: (bi, i, 0, 0)),
                  pl.BlockSpec((1, 1, 1, cols), lambda bi, i: (bi, i, 0, 0)),
                  pl.BlockSpec((1, nkb, DSA_TK, kdim), lambda bi, i: (bi, 0, 0, 0)),
                  pl.BlockSpec((1, 1, dk, cols), lambda bi, i: (bi, i, 0, 0)),
                  pl.BlockSpec((1, nkb, DSA_TK, dk), lambda bi, i: (bi, 0, 0, 0)),
                  pl.BlockSpec((1, nkb, dva, DSA_TK), lambda bi, i: (bi, 0, 0, 0)),
                  pl.BlockSpec((4, DSA_TK, cols), lambda bi, i: (0, 0, 0))],
        out_specs=pl.BlockSpec((1, 1, dv, cols), lambda bi, i: (bi, i, 0, 0)),
        out_shape=jax.ShapeDtypeStruct((b, nq, dv, cols), out_dtype),
        scratch_shapes=[pltpu.VMEM((nkb, DSA_TK, DSA_TQ), I32), pltpu.VMEM((1, cols), F32),
                        pltpu.VMEM((dva, cols), F32)],
        compiler_params=_cparams(2), name="dsa")(qit, w, ki, qt, k, vt, tiles)


_NT = (((1,), (1,)), ((), ()))


def _heads(x, n, d):
    return x.reshape(n, d, x.shape[-1])


def _proj_even_kernel(x_ref, g_ref, w1_ref, w2t_ref, w3_ref, qn_ref, wuqt_ref, kvn_ref, wkb_ref, e1_ref, e2_ref,
                      wuvt_ref, cosn_ref, sinn_ref, cost_ref, sint_ref,
                      akv_ref, ckv_ref, y1_ref, y2_ref, qta_ref, ka_ref, vta_ref, qtb_ref, kb_ref, vtb_ref):
    nh, da, rope_half = A_HEADS, A_DIM, B_ROPE // 2
    h = _rms(x_ref[...], g_ref[...]).astype(BF16)
    h1 = jnp.dot(h, w1_ref[...], preferred_element_type=F32)
    akv_ref[...] = h1[:, :2 * nh * da]
    t2 = lax.dot_general(w2t_ref[...], h, _NT, preferred_element_type=F32)
    qta_ref[0, :, 0] = _heads(t2[:nh * da], nh, da).astype(BF16)
    vta_ref[0, :, 0] = _heads(t2[nh * da:], nh, da).astype(BF16)
    k3 = jnp.dot(h, w3_ref[...], preferred_element_type=F32)
    for hd in range(nh):
        ka_ref[0, hd, 0] = k3[:, hd * LANES:hd * LANES + da].astype(BF16)

    c0 = 2 * nh * da
    cqn = _rms(h1[:, c0:c0 + Q_LORA], qn_ref[...]).astype(BF16)
    tq = lax.dot_general(wuqt_ref[...], cqn, _NT, preferred_element_type=F32)
    n_nope, n_rope = B_HEADS * B_NOPE, B_HEADS * rope_half
    x1, x2 = tq[n_nope:n_nope + n_rope], tq[n_nope + n_rope:]
    ct, st = cost_ref[0], sint_ref[0]
    y1t, y2t = x1 * ct - x2 * st, x1 * st + x2 * ct
    for hd in range(B_HEADS):
        qtb_ref[0, hd, 0] = jnp.concatenate(
            [tq[hd * B_NOPE:(hd + 1) * B_NOPE], y1t[hd * rope_half:(hd + 1) * rope_half],
             y2t[hd * rope_half:(hd + 1) * rope_half]], axis=0).astype(BF16)

    c1 = c0 + Q_LORA
    ckv = _rms(h1[:, c1:c1 + KV_LORA], kvn_ref[...])
    ckv_ref[...] = ckv
    ckvb = ckv.astype(BF16)
    k1, k2 = h1[:, c1 + KV_LORA:c1 + KV_LORA + LANES], h1[:, c1 + KV_LORA + LANES:]
    cn, sn = cosn_ref[...], sinn_ref[...]
    y1, y2 = k1 * cn - k2 * sn, k1 * sn + k2 * cn
    y1_ref[...] = y1
    y2_ref[...] = y2
    kb = (jnp.dot(ckvb, wkb_ref[...], preferred_element_type=F32)
          + jnp.dot(y1.astype(BF16), e1_ref[...], preferred_element_type=F32)
          + jnp.dot(y2.astype(BF16), e2_ref[...], preferred_element_type=F32))
    for hd in range(B_HEADS):
        kb_ref[0, hd, 0] = kb[:, hd * LANES:hd * LANES + B_NOPE + B_ROPE].astype(BF16)
    tv = lax.dot_general(wuvt_ref[...], ckvb, _NT, preferred_element_type=F32)
    vtb_ref[0, :, 0] = _heads(tv, B_HEADS, B_V).astype(BF16)


def proj_even(x, g, w_in, q_norm, w_uq, kv_norm, w_ukv, cos, sin, blk):
    p, d = x.shape
    nq = p // blk
    nh, da, half = A_HEADS, A_DIM, B_ROPE // 2
    s_a = A_DIM ** -0.5 * LOG2E
    s_b = (B_NOPE + B_ROPE) ** -0.5 * LOG2E
    aq_w, ak_w, av_w, cq_w, ckv_w, kr_w = _split(w_in, EVEN_SPLITS)

    def lane_pad(w, n=LANES):
        return jnp.pad(w, ((0, 0), (0, n - w.shape[1])))

    def head_chunks(w, dh):
        k = w.shape[0]
        return jnp.pad(w.reshape(k, -1, dh), ((0, 0), (0, 0), (0, LANES - dh))).reshape(k, -1)

    w1 = jnp.concatenate([ak_w, av_w, cq_w, ckv_w, lane_pad(kr_w[:, :half]), lane_pad(kr_w[:, half:])], axis=1).astype(BF16)
    w2t = jnp.concatenate([aq_w * s_a, av_w], axis=1).T.astype(BF16)
    w3 = head_chunks(ak_w, da).astype(BF16)
    uq = (w_uq * s_b).reshape(Q_LORA, B_HEADS, B_NOPE + B_ROPE)
    wuqt = jnp.concatenate([uq[:, :, :B_NOPE].reshape(Q_LORA, -1), uq[:, :, B_NOPE:B_NOPE + half].reshape(Q_LORA, -1),
                            uq[:, :, B_NOPE + half:].reshape(Q_LORA, -1)], axis=1).T.astype(BF16)
    ukv = w_ukv.reshape(KV_LORA, B_HEADS, B_NOPE + B_V)
    wkb = head_chunks(ukv[:, :, :B_NOPE].reshape(KV_LORA, -1), B_NOPE).astype(BF16)
    wuvt = ukv[:, :, B_NOPE:].reshape(KV_LORA, -1).T.astype(BF16)
    eye = np.zeros((2, LANES, B_HEADS, LANES), np.float32)
    for r in range(half):
        eye[0, r, :, B_NOPE + r] = 1.0
        eye[1, r, :, B_NOPE + half + r] = 1.0
    e1, e2 = (jnp.asarray(e.reshape(LANES, B_HEADS * LANES), BF16) for e in eye)
    cosn, sinn = lane_pad(cos), lane_pad(sin)
    cost = jnp.tile(cos.reshape(nq, blk, half).transpose(0, 2, 1), (1, B_HEADS, 1))
    sint = jnp.tile(sin.reshape(nq, blk, half).transpose(0, 2, 1), (1, B_HEADS, 1))

    def full(a):
        return pl.BlockSpec(a.shape, lambda i: (0,) * a.ndim)

    def rows(n):
        return pl.BlockSpec((blk, n), lambda i: (i, 0))

    def tile3(a):
        return pl.BlockSpec((1,) + a.shape[1:], lambda i: (i, 0, 0))

    def per_block(g_, a, b_):
        return (jax.ShapeDtypeStruct((1, g_, nq, a, b_), BF16),
                pl.BlockSpec((1, g_, 1, a, b_), lambda i: (0, 0, i, 0, 0)))

    consts = [g.reshape(1, d).astype(F32), w1, w2t, w3, q_norm.reshape(1, -1).astype(F32), wuqt,
              kv_norm.reshape(1, -1).astype(F32), wkb, e1, e2, wuvt]
    blocks = [per_block(nh, da, blk), per_block(nh, blk, da), per_block(nh, da, blk),
              per_block(B_HEADS, B_NOPE + B_ROPE, blk), per_block(B_HEADS, blk, B_NOPE + B_ROPE),
              per_block(B_HEADS, B_V, blk)]
    out_shape = [jax.ShapeDtypeStruct((p, 2 * nh * da), F32), jax.ShapeDtypeStruct((p, KV_LORA), F32),
                 jax.ShapeDtypeStruct((p, LANES), F32), jax.ShapeDtypeStruct((p, LANES), F32)] + [s for s, _ in blocks]
    out_specs = [rows(2 * nh * da), rows(KV_LORA), rows(LANES), rows(LANES)] + [s for _, s in blocks]
    return pl.pallas_call(
        _proj_even_kernel, grid=(nq,),
        in_specs=[rows(d)] + [full(a) for a in consts] + [rows(LANES), rows(LANES), tile3(cost), tile3(sint)],
        out_specs=out_specs, out_shape=out_shape,
        compiler_params=_cparams(1), name="proj_even")(x, *consts, cosn, sinn, cost, sint)


def _proj_odd_kernel(x_ref, g_ref, w1_ref, w2t_ref, w3_ref,
                     st_ref, cqt_ref, kc_ref, cvt_ref, qit_ref, wq_ref, qt_ref):
    nmap, width = 2 * C_HEADS, C_HEADS * 2 * C_DIM
    h = _rms(x_ref[...], g_ref[...]).astype(BF16)
    st_ref[...] = jnp.dot(h, w1_ref[...], preferred_element_type=F32)
    t2 = lax.dot_general(w2t_ref[...], h, _NT, preferred_element_type=F32)
    cqt_ref[0, :, 0] = _heads(t2[:width], nmap, C_DIM).astype(BF16)
    cvt_ref[0, :, 0] = _heads(t2[width:2 * width], C_HEADS, 2 * C_DIM).astype(BF16)
    k3 = jnp.dot(h, w3_ref[...], preferred_element_type=F32)
    for m in range(nmap):
        kc_ref[0, m, 0] = k3[:, m * LANES:m * LANES + C_DIM].astype(BF16)

    dqt = t2[2 * width:2 * width + D_HEADS * D_DIM]
    dqit = t2[2 * width + D_HEADS * D_DIM:2 * width + D_HEADS * (D_DIM + IDX_DIM)]
    dwt = t2[2 * width + D_HEADS * (D_DIM + IDX_DIM):]

    def stack(x, rows_per_head, n_rows, b):
        cols = slice(b * DSA_TQ, (b + 1) * DSA_TQ)
        return jnp.concatenate([x[hd * rows_per_head:hd * rows_per_head + n_rows, cols] for hd in range(D_HEADS)], axis=1)

    for b in range(x_ref.shape[0] // DSA_TQ):
        qt_ref[0, b] = stack(dqt, D_DIM, D_DIM, b).astype(BF16)
        qi = stack(dqit, IDX_DIM, IDX_DIM, b)
        hi = qi.astype(BF16)
        lo = (qi - hi.astype(F32)).astype(BF16)
        qit_ref[0, b] = jnp.concatenate([hi, lo, hi], axis=0)
        wq_ref[0, b] = stack(dwt, SUBLANES, 1, b)


def proj_odd(x, g, w_in, blk):
    p, d = x.shape
    nq, nq_dsa = p // blk, p // DSA_TQ
    s_c = C_DIM ** -0.5 * LOG2E
    s_d = D_DIM ** -0.5 * LOG2E
    s_w = (IDX_HEADS ** -0.5) * (IDX_DIM ** -0.5)
    cq_w, ck_w, cv_w, dq_w, dk_w, dv_w, dqi_w, dki_w, dw_w = _split(w_in, ODD_SPLITS)

    def lane_pad(w):
        return jnp.pad(w, ((0, 0), (0, LANES - w.shape[1])))

    w1 = jnp.concatenate([ck_w, cv_w, lane_pad(dk_w), lane_pad(dv_w), lane_pad(dki_w)], axis=1).astype(BF16)
    dw_rows = jnp.pad((dw_w * s_w)[:, :, None], ((0, 0), (0, 0), (0, SUBLANES - 1))).reshape(d, -1)
    w2t = jnp.concatenate([cq_w * s_c, cv_w, dq_w * s_d, dqi_w, dw_rows], axis=1).T.astype(BF16)
    w3 = jnp.pad(ck_w.reshape(d, -1, C_DIM), ((0, 0), (0, 0), (0, LANES - C_DIM))).reshape(d, -1).astype(BF16)
    consts = [g.reshape(1, d).astype(F32), w1, w2t, w3]
    cols = D_HEADS * DSA_TQ
    per = blk // DSA_TQ

    def per_block(g_, a, b_):
        return (jax.ShapeDtypeStruct((1, g_, nq, a, b_), BF16),
                pl.BlockSpec((1, g_, 1, a, b_), lambda i: (0, 0, i, 0, 0)))

    def per_dsa(a, dtype):
        return (jax.ShapeDtypeStruct((1, nq_dsa, a, cols), dtype), pl.BlockSpec((1, per, a, cols), lambda i: (0, i, 0, 0)))

    blocks = [per_block(2 * C_HEADS, C_DIM, blk), per_block(2 * C_HEADS, blk, C_DIM), per_block(C_HEADS, 2 * C_DIM, blk),
              per_dsa(3 * IDX_DIM, BF16), per_dsa(1, F32), per_dsa(D_DIM, BF16)]
    return pl.pallas_call(
        _proj_odd_kernel, grid=(nq,),
        in_specs=[pl.BlockSpec((blk, d), lambda i: (i, 0))] + [pl.BlockSpec(a.shape, lambda i: (0, 0)) for a in consts],
        out_specs=[pl.BlockSpec((blk, w1.shape[1]), lambda i: (i, 0))] + [s for _, s in blocks],
        out_shape=[jax.ShapeDtypeStruct((p, w1.shape[1]), F32)] + [s for s, _ in blocks],
        compiler_params=_cparams(1), name="proj_odd")(x, *consts)


def _out_odd_kernel(x_ref, c_ref, d_ref, lam_ref, sub_ref, w_ref, o_ref, *, post_scale):
    blk = x_ref.shape[0]
    lam, sub = lam_ref[...], sub_ref[...]
    parts = []
    for hd in range(C_HEADS):
        c = c_ref[0, 2 * hd, 0] - lam * c_ref[0, 2 * hd + 1, 0]
        parts.append(c * lax.rsqrt(jnp.mean(c * c, axis=0, keepdims=True) + EPS) * sub * post_scale)
    for hd in range(D_HEADS):
        parts.append(jnp.concatenate([d_ref[0, b][:, hd * DSA_TQ:(hd + 1) * DSA_TQ]
                                      for b in range(blk // DSA_TQ)], axis=1).astype(F32))
    mix = jnp.concatenate(parts, axis=0).T.astype(BF16)
    o_ref[...] = x_ref[...] + jnp.dot(mix, w_ref[...], preferred_element_type=F32)


def out_odd(x, c_t, d_t, lam, subln, post_scale, w_out):
    p, d = x.shape
    _, nmap, nq, dc, blk = c_t.shape
    per = blk // DSA_TQ
    return pl.pallas_call(
        functools.partial(_out_odd_kernel, post_scale=post_scale), grid=(nq,),
        in_specs=[pl.BlockSpec((blk, d), lambda i: (i, 0)),
                  pl.BlockSpec((1, nmap, 1, dc, blk), lambda i: (0, 0, i, 0, 0)),
                  pl.BlockSpec((1, per) + d_t.shape[2:], lambda i: (0, i, 0, 0)),
                  pl.BlockSpec((1, blk), lambda i: (0, 0)),
                  pl.BlockSpec((dc, 1), lambda i: (0, 0)),
                  pl.BlockSpec(w_out.shape, lambda i: (0, 0))],
        out_specs=pl.BlockSpec((blk, d), lambda i: (i, 0)),
        out_shape=jax.ShapeDtypeStruct((p, d), F32),
        compiler_params=_cparams(1), name="out_odd")(
            x, c_t, d_t, jnp.full((1, blk), lam, F32), subln.reshape(dc, 1).astype(F32), w_out)


def _out_even_kernel(x_ref, a_ref, b_ref, w_ref, o_ref):
    blk = x_ref.shape[0]
    mix_t = jnp.concatenate([a_ref[0, :, 0].reshape(-1, blk), b_ref[0, :, 0].reshape(-1, blk)], axis=0)
    mix = mix_t.astype(F32).T.astype(BF16)
    o_ref[...] = x_ref[...] + jnp.dot(mix, w_ref[...], preferred_element_type=F32)


def out_even(x, a_t, b_t, w_out):
    p, d = x.shape
    _, g_, nq, dv, blk = a_t.shape
    blk_spec = pl.BlockSpec((1, g_, 1, dv, blk), lambda i: (0, 0, i, 0, 0))
    return pl.pallas_call(
        _out_even_kernel, grid=(nq,),
        in_specs=[pl.BlockSpec((blk, d), lambda i: (i, 0)), blk_spec, blk_spec,
                  pl.BlockSpec(w_out.shape, lambda i: (0, 0))],
        out_specs=pl.BlockSpec((blk, d), lambda i: (i, 0)),
        out_shape=jax.ShapeDtypeStruct((p, d), F32),
        compiler_params=_cparams(1), name="out_even")(x, a_t, b_t, w_out)


def _rope_tables(pos):
    half = B_ROPE // 2
    freqs = jnp.power(jnp.float32(ROPE_THETA), -jnp.arange(half, dtype=F32) / half)
    ang = pos.astype(F32)[:, None] * freqs[None, :]
    return jnp.cos(ang), jnp.sin(ang)


def _rope(x, cos, sin):
    half = x.shape[-1] // 2
    shape = (x.shape[0],) + (1,) * (x.ndim - 2) + (half,)
    c, s = cos.reshape(shape), sin.reshape(shape)
    x1, x2 = x[..., :half], x[..., half:]
    return jnp.concatenate([x1 * c - x2 * s, x1 * s + x2 * c], axis=-1)


def _t5_bucket(rel):
    nb = T5_BUCKETS // 2
    max_exact = nb // 2
    ret = jnp.where(rel > 0, nb, 0)
    n = jnp.abs(rel)
    nf = jnp.maximum(n, 1).astype(F32)
    large = max_exact + (jnp.log(nf / max_exact) / math.log(T5_MAX_DIST / max_exact)
                         * (nb - max_exact)).astype(I32)
    large = jnp.minimum(large, nb - 1)
    return ret + jnp.where(n < max_exact, n, large)


def _chunk_diff(tq, tk, n_tiles):
    r = np.arange(tq)[None, None, :]
    c = np.arange(tk)[None, :, None]
    t = np.arange(n_tiles)[:, None, None]
    return c // CHUNK - r // CHUNK - t * (tq // CHUNK)


def _strip_distances(tq, tk, n_tiles):
    y = np.arange(tq + tk)[None, :]
    t = np.arange(n_tiles)[:, None]
    return np.where(y < tq, -y, tq + tk - y) - t * tq


def _toeplitz(strip, tq, tk):
    n = tq + tk
    lead = strip.shape[:-1]
    rows = jnp.broadcast_to(strip[..., None, :], lead + (tk, n)).reshape(lead + (tk * n,))
    return rows[..., :tk * (n - 1)].reshape(lead + (tk, n - 1))[..., :tq]


def band_tiles(relbias, blk):
    n_tiles = (A_LEFT_CHUNKS * CHUNK) // blk + 1
    dchunk = _chunk_diff(blk, blk, n_tiles)
    adm = (dchunk <= 0) & (-dchunk <= A_LEFT_CHUNKS)
    idx = np.clip(-_strip_distances(blk, blk, n_tiles), -A_MAX_REL, A_MAX_REL) + A_MAX_REL
    bias = _toeplitz(relbias.astype(F32)[:, idx] * LOG2E, blk, blk)
    tiles = jnp.where(adm[None], bias, MASKED)
    far = jnp.full((relbias.shape[0], 1, blk, blk), MASKED, F32)
    return jnp.concatenate([tiles, far], axis=1), n_tiles


def causal_tiles(blk):
    t0 = np.where(_chunk_diff(blk, blk, 1) <= 0, 0.0, MASKED).astype(np.float32)
    return jnp.asarray(np.concatenate([t0, np.zeros_like(t0)], axis=0)[None])


def t5_tiles(tab, tq, tk, n_tiles):
    assert n_tiles * tq - tk + 1 >= T5_MAX_DIST
    tab = tab.astype(F32) * LOG2E
    strip = jnp.moveaxis(tab[_t5_bucket(jnp.asarray(_strip_distances(tq, tk, n_tiles), I32))], -1, 0)
    tiles = jnp.where((_chunk_diff(tq, tk, n_tiles) <= 0)[None], _toeplitz(strip, tq, tk), MASKED)
    far = tab[_t5_bucket(jnp.asarray([-(n_tiles * tq + tk)], I32))[0]]
    far = jnp.broadcast_to(far[:, None, None, None], (tab.shape[1], 1, tk, tq))
    return jnp.concatenate([tiles, far], axis=1)


def _row_blocks(x, blk):
    b, t, g, d = x.shape
    return jnp.transpose(x.reshape(b, t // blk, blk, g, d), (0, 3, 1, 2, 4)).astype(BF16)


def _col_blocks(x, blk):
    b, t, g, d = x.shape
    return jnp.transpose(x.reshape(b, t // blk, blk, g, d), (0, 3, 1, 4, 2)).astype(BF16)


def _from_col_blocks(o):
    b, g, nq, d, blk = o.shape
    return jnp.transpose(o, (0, 2, 4, 1, 3)).reshape(b, nq * blk, g * d)


def _pad_rows(x, n, front=0):
    back = n - x.shape[1] - front
    return jnp.pad(x, ((0, 0), (front, back)) + ((0, 0),) * (x.ndim - 2))


def _split(h, sizes):
    out, o = [], 0
    for s in sizes:
        out.append(h[:, o:o + s])
        o += s
    return out


def _split3(x):
    hi = x.astype(BF16)
    lo = (x - hi.astype(F32)).astype(BF16)
    return hi, lo


def _dsa_keys(dk_all, dv_all, dki_all):
    b, nkb = dk_all.shape[0], dk_all.shape[1] // DSA_TK
    kh, kl = _split3(dki_all)
    ki = jnp.concatenate([kh, kh, kl], axis=-1).reshape(b, nkb, DSA_TK, 3 * IDX_DIM)
    k = dk_all.astype(BF16).reshape(b, nkb, DSA_TK, D_DIM)
    vt = jnp.transpose(dv_all.astype(BF16).reshape(b, nkb, DSA_TK, D_DIM), (0, 1, 3, 2))
    norm = jnp.zeros((b, nkb, NORM_ROWS, DSA_TK), BF16).at[:, :, 0].set(1.0)
    return ki, k, jnp.concatenate([vt, norm], axis=2)


def _dsa_call(dq, dqi, dw, dk_all, dv_all, dki_all, tiles, i_off, n_sel):
    b, tq_all = dq.shape[:2]
    nq = tq_all // DSA_TQ

    def stack_t(x):
        d = x.shape[-1]
        return jnp.transpose(x.reshape(b, nq, DSA_TQ, D_HEADS, d), (0, 1, 4, 3, 2)).reshape(b, nq, d, D_HEADS * DSA_TQ)

    qh, ql = _split3(dqi)
    qit = stack_t(jnp.concatenate([qh, ql, qh], axis=-1))
    w = stack_t(dw[..., None])
    qt = stack_t((dq * (D_DIM ** -0.5 * LOG2E)).astype(BF16))
    ki, k, vt = _dsa_keys(dk_all, dv_all, dki_all)
    o = dsa(qit, w, ki, qt, k, vt, tiles, i_off=i_off, n_sel=n_sel)
    o = jnp.transpose(o.reshape(b, nq, D_DIM, D_HEADS, DSA_TQ), (0, 1, 4, 3, 2))
    return o.reshape(b, tq_all, D_HEADS * D_DIM)


P_BLK = 512
P_HEADS = 2
S_BLK = 128


def kernel(x_prompt, x_sample, cache_a_k, cache_a_v, cache_b_ckv, cache_b_krope, cache_c_k, cache_c_v, cache_d_k, cache_d_v, cache_d_kidx, t5_table, norm_attn, norm_ffn, final_norm, even_w_in, even_w_out, a_relbias, b_q_norm, b_kv_norm, b_w_uq, b_w_ukv, odd_w_in, odd_w_out, c_lambda_q1, c_lambda_k1, c_lambda_q2, c_lambda_k2, c_subln, ffn_w_gate, ffn_w_up, ffn_w_down):
    pb, pt, d = x_prompt.shape
    sb, st, _ = x_sample.shape
    assert pb == 1 and st == CHUNK
    past = cache_b_ckv.shape[2]
    n_a = cache_a_k.shape[2]
    depth = norm_attn.shape[0]
    n_p, n_s = pb * pt, sb * st
    s_len = past + st
    s_pad = -(-s_len // DSA_TK) * DSA_TK
    s_i = past // S_BLK
    s_i_dsa = past // DSA_TQ
    assert past % S_BLK == 0 and s_pad % S_BLK == 0 and past >= n_a
    p_blk = min(P_BLK, pt)

    cos_p, sin_p = _rope_tables(jnp.arange(pt))
    cos_s, sin_s = _rope_tables(jnp.tile(past + jnp.arange(st), sb))

    def rows_p(a):
        return a.reshape(pb, pt, *a.shape[1:])

    def rows_s(a):
        return a.reshape(sb, st, *a.shape[1:])

    def pad_cols(w):
        n = -(-w.shape[1] // LANES) * LANES
        return jnp.pad(w, ((0, 0), (0, n - w.shape[1]))).astype(BF16)

    def sample_q(a):
        return _pad_rows(a, S_BLK)

    def sample_keys(cache, new, front=0):
        return _pad_rows(jnp.concatenate([cache, new], axis=1), s_pad, front)

    def attend(q, k, v, tiles, blk, n_tiles, has_far, i_off=0):
        o = flash(_col_blocks(q, blk), _row_blocks(k, blk), _col_blocks(v, blk), tiles,
                  n_tiles=n_tiles, has_far=has_far, i_off=i_off, heads=q.shape[2])
        return _from_col_blocks(o)

    t5_c, t5_d = t5_table[:, :C_HEADS], t5_table[:, C_HEADS:]
    ctiles_p = causal_tiles(p_blk)
    ctiles_s = causal_tiles(S_BLK)
    c_tiles_p = t5_tiles(t5_c, p_blk, p_blk, 2)
    c_tiles_s = t5_tiles(t5_c, S_BLK, S_BLK, 2)
    d_tiles = t5_tiles(t5_d, DSA_TQ, DSA_TK, 3)
    d_tiles = jnp.transpose(d_tiles, (1, 2, 0, 3)).reshape(4, DSA_TK, D_HEADS * DSA_TQ)
    n_sel_p = min(TOPK_MAX, pt // 4)
    n_sel_s = min(TOPK_MAX, s_len // 4)

    outs_even = {k: [] for k in ("a_k_p", "a_k_s", "a_v_p", "a_v_s", "ckv_p", "ckv_s", "kr_p", "kr_s")}
    outs_odd = {k: [] for k in ("c_k_p", "c_k_s", "c_v_p", "c_v_s", "d_k_p", "d_k_s", "d_v_p", "d_v_s", "d_ki_p", "d_ki_s")}

    def mla_keys(kn, krope):
        return jnp.concatenate([kn, jnp.broadcast_to(krope[:, :, None, :], kn.shape[:3] + (B_ROPE,))], axis=-1)

    def even_prompt(x, l, i):
        a_tiles, nt = band_tiles(a_relbias[i], p_blk)
        akv, ckv, y1, y2, qta, ka, vta, qtb, kb, vtb = proj_even(
            x, norm_attn[l], even_w_in[i], b_q_norm[i], b_w_uq[i], b_kv_norm[i], b_w_ukv[i], cos_p, sin_p, p_blk)
        a_t = flash(qta, ka, vta, a_tiles, n_tiles=nt, has_far=False, out_dtype=BF16, heads=P_HEADS)
        b_t = flash(qtb, kb, vtb, ctiles_p, n_tiles=1, has_far=True, out_dtype=BF16, heads=P_HEADS)
        half = B_ROPE // 2
        hd = A_HEADS * A_DIM
        outs_even["a_k_p"].append(akv[pt - n_a:, :hd].reshape(pb, n_a, A_HEADS, A_DIM))
        outs_even["a_v_p"].append(akv[pt - n_a:, hd:].reshape(pb, n_a, A_HEADS, A_DIM))
        outs_even["ckv_p"].append(rows_p(ckv))
        outs_even["kr_p"].append(rows_p(jnp.concatenate([y1[:, :half], y2[:, :half]], axis=-1)))
        return out_even(x, a_t, b_t, even_w_out[i].astype(BF16))

    def even_sample(x, l, i):
        h = dense(x, pad_cols(even_w_in[i]), g=norm_attn[l])
        aq, ak, av, cq, ckv_raw, kr_raw = _split(h, EVEN_SPLITS)
        qb = dense(cq, b_w_uq[i].astype(BF16), g=b_q_norm[i]).reshape(-1, B_HEADS, B_NOPE + B_ROPE)
        kv_new, ckv = dense(ckv_raw, b_w_ukv[i].astype(BF16), g=b_kv_norm[i], emit_h=True)
        kr = _rope(kr_raw, cos_s, sin_s)
        q_mla = jnp.concatenate([qb[..., :B_NOPE], _rope(qb[..., B_NOPE:], cos_s, sin_s)], axis=-1)
        q_mla = q_mla * ((B_NOPE + B_ROPE) ** -0.5 * LOG2E)
        kv_s = rows_s(kv_new.reshape(-1, B_HEADS, B_NOPE + B_V))
        aq = (aq * (A_DIM ** -0.5 * LOG2E)).reshape(-1, A_HEADS, A_DIM)
        ak = ak.reshape(-1, A_HEADS, A_DIM)
        av = av.reshape(-1, A_HEADS, A_DIM)
        a_tiles, nt = band_tiles(a_relbias[i], S_BLK)
        ak_all = jnp.concatenate([cache_a_k[i], rows_s(ak)], axis=1)
        av_all = jnp.concatenate([cache_a_v[i], rows_s(av)], axis=1)
        a_out = attend(sample_q(rows_s(aq)), _pad_rows(ak_all, s_pad, past - n_a),
                       _pad_rows(av_all, s_pad, past - n_a), a_tiles, S_BLK, nt, False, s_i)[:, :st]
        kv_c = dense(cache_b_ckv[i].reshape(sb * past, KV_LORA), b_w_ukv[i].astype(BF16))
        kv_c = kv_c.reshape(sb, past, B_HEADS, B_NOPE + B_V)
        kn_all = sample_keys(kv_c[..., :B_NOPE], kv_s[..., :B_NOPE])
        v_all = sample_keys(kv_c[..., B_NOPE:], kv_s[..., B_NOPE:])
        kr_all = sample_keys(cache_b_krope[i], rows_s(kr))
        b_out = attend(sample_q(rows_s(q_mla)), mla_keys(kn_all, kr_all), v_all, ctiles_s, S_BLK, 1, True, s_i)[:, :st]
        mix = jnp.concatenate([a_out.reshape(n_s, -1), b_out.reshape(n_s, -1)], axis=-1)
        outs_even["a_k_s"].append(ak_all[:, -n_a:])
        outs_even["a_v_s"].append(av_all[:, -n_a:])
        outs_even["ckv_s"].append(rows_s(ckv))
        outs_even["kr_s"].append(rows_s(kr))
        return dense(mix.astype(BF16), even_w_out[i].astype(BF16), res=x)

    def diff_lambda(l, i):
        lam_init = 0.8 - 0.6 * math.exp(-0.3 * l)
        lam = (jnp.exp(jnp.sum(c_lambda_q1[i].astype(F32) * c_lambda_k1[i].astype(F32)))
               - jnp.exp(jnp.sum(c_lambda_q2[i].astype(F32) * c_lambda_k2[i].astype(F32))) + lam_init)
        return lam_init, lam

    def odd_prompt(x, l, i):
        lam_init, lam = diff_lambda(l, i)
        st_, cqt, kc, cvt, qit, wq, qt = proj_odd(x, norm_attn[l], odd_w_in[i], p_blk)
        wc = C_HEADS * 2 * C_DIM
        ck, cv = st_[:, :wc], st_[:, wc:2 * wc]
        dk, dv, dki = (st_[:, 2 * wc + n * LANES:2 * wc + n * LANES + D_DIM] for n in range(3))
        c_t = flash(cqt, kc, cvt, c_tiles_p, n_tiles=2, has_far=True, heads=P_HEADS)
        ki, k, vt = _dsa_keys(dk[None], dv[None], dki[None])
        d_t = dsa(qit, wq, ki, qt, k, vt, d_tiles, i_off=0, n_sel=n_sel_p, out_dtype=BF16)
        for name, arr, shp in (("c_k", ck, (C_HEADS, 2 * C_DIM)), ("c_v", cv, (C_HEADS, 2 * C_DIM)),
                               ("d_k", dk, (D_DIM,)), ("d_v", dv, (D_DIM,)), ("d_ki", dki, (IDX_DIM,))):
            outs_odd[name + "_p"].append(arr.reshape(pb, pt, *shp))
        return out_odd(x, c_t, d_t, lam, c_subln[i], 1.0 - lam_init, odd_w_out[i].astype(BF16))

    def odd_sample(x, l, i):
        lam_init, lam = diff_lambda(l, i)
        h = dense(x, pad_cols(odd_w_in[i]), g=norm_attn[l])
        cq, ck, cv, dq, dk, dv, dqi, dki, dw = _split(h, ODD_SPLITS)
        cq = (cq * (C_DIM ** -0.5 * LOG2E)).reshape(-1, 2 * C_HEADS, C_DIM)
        ck3 = ck.reshape(-1, 2 * C_HEADS, C_DIM)
        cv3 = cv.reshape(-1, C_HEADS, 2 * C_DIM)
        dq = dq.reshape(-1, D_HEADS, D_DIM)
        dqi = dqi.reshape(-1, IDX_HEADS, IDX_DIM)
        dw = dw * ((IDX_HEADS ** -0.5) * (IDX_DIM ** -0.5))

        def diff_combine(o):
            b_, t_, _ = o.shape
            o = o.reshape(b_, t_, C_HEADS, 2, 2 * C_DIM)
            c = o[:, :, :, 0] - lam * o[:, :, :, 1]
            return (_rms(c, c_subln[i].astype(F32)) * (1.0 - lam_init)).reshape(b_, t_, -1)

        rows, tag, nb_, nt_ = rows_s, "_s", sb, st
        ck_all = sample_keys(cache_c_k[i].reshape(sb, past, 2 * C_HEADS, C_DIM), rows(ck3))
        cv_all = sample_keys(cache_c_v[i], rows(cv3))
        c_out = attend(sample_q(rows(cq)), ck_all, cv_all, c_tiles_s, S_BLK, 2, True, s_i)[:, :st]
        d_out = _dsa_call(_pad_rows(rows(dq), DSA_TQ), _pad_rows(rows(dqi), DSA_TQ), _pad_rows(rows(dw), DSA_TQ),
                          sample_keys(cache_d_k[i], rows(dk)), sample_keys(cache_d_v[i], rows(dv)),
                          sample_keys(cache_d_kidx[i], rows(dki)), d_tiles, s_i_dsa, n_sel_s)[:, :st]
        mix = jnp.concatenate([diff_combine(c_out).reshape(nb_ * nt_, -1), d_out.reshape(nb_ * nt_, -1)], axis=-1)
        for name, arr, shp in (("c_k", ck, (C_HEADS, 2 * C_DIM)), ("c_v", cv, (C_HEADS, 2 * C_DIM)),
                               ("d_k", dk, (D_DIM,)), ("d_v", dv, (D_DIM,)), ("d_ki", dki, (IDX_DIM,))):
            outs_odd[name + tag].append(rows(arr).reshape(nb_, nt_, *shp))
        return dense(mix.astype(BF16), odd_w_out[i].astype(BF16), res=x)

    xp, xs = x_prompt.reshape(n_p, d), x_sample.reshape(n_s, d)
    for l in range(depth):
        i = l // 2
        if l % 2 == 0:
            xp, xs = even_prompt(xp, l, i), even_sample(xs, l, i)
        else:
            xp, xs = odd_prompt(xp, l, i), odd_sample(xs, l, i)
        wg, wu, wd = ffn_w_gate[l].astype(BF16), ffn_w_up[l].astype(BF16), ffn_w_down[l].astype(BF16)
        fg = final_norm if l == depth - 1 else None
        xp, xs = ffn(xp, norm_ffn[l], wg, wu, wd, final_g=fg), ffn(xs, norm_ffn[l], wg, wu, wd, final_g=fg)

    se = {k: jnp.stack(v, axis=0) for k, v in outs_even.items()}
    so = {k: jnp.stack(v, axis=0) for k, v in outs_odd.items()}
    return (xp.reshape(pb, pt, d), xs.reshape(sb, st, d), se["a_k_p"], se["a_k_s"], se["a_v_p"], se["a_v_s"],
            se["ckv_p"], se["ckv_s"], se["kr_p"], se["kr_s"],
            so["c_k_p"], so["c_k_s"], so["c_v_p"], so["c_v_s"],
            so["d_k_p"], so["d_k_s"], so["d_v_p"], so["d_v_s"], so["d_ki_p"], so["d_ki_s"])
```
